```python
import jax
import jax.numpy as jnp
from jax import lax
import numpy as np

D_MODEL = 1024
BATCH = 4
SEQ = 4096
DEPTH = 4

CTX_LEN = 256
GRID_W = 64
N_EVEN = (DEPTH + 1) // 2
N_ODD = DEPTH // 2
EPS = 1e-6

RET_HEADS = 4
RET_HEAD_DIM = D_MODEL // 8
RET_W = RET_HEADS * RET_HEAD_DIM
RET_CHUNK = 128
ROPE_BASE = 10000.0
ROPE_PAIRS = (RET_HEAD_DIM // 8, 3 * RET_HEAD_DIM // 16, 3 * RET_HEAD_DIM // 16)
CONV_CH = D_MODEL // 2
CONV_K = 31
EVEN_IN = 4 * RET_W + 2 * CONV_CH
EVEN_MIX = RET_W + CONV_CH
POOL_CH = D_MODEL // 2
POOL_WINDOWS = (2, 4, 8, 16)
POOL_GROUPS = len(POOL_WINDOWS)
POOL_GC = POOL_CH // POOL_GROUPS
SG_CH = D_MODEL // 2
SG_GROUPS = 4
SG_GC = SG_CH // SG_GROUPS
SG_CHUNK = 128
ODD_IN = POOL_CH + 2 * SG_CH
ODD_MIX = POOL_CH + SG_CH
D_FF = ((8 * D_MODEL // 3 + 255) // 256) * 256

kernel_name = 'hybrid_retention_conformer_pool_gmlp_prefix_dit'


def rms_norm(x, w):
    x32 = x.astype(jnp.float32)
    y = x32 * lax.rsqrt(jnp.mean(x32 * x32, axis=-1, keepdims=True) + EPS)
    return (y * w.astype(jnp.float32)).astype(x.dtype)


def layer_norm(x, w, b):
    x32 = x.astype(jnp.float32)
    mu = jnp.mean(x32, axis=-1, keepdims=True)
    xc = x32 - mu
    y = xc * lax.rsqrt(jnp.mean(xc * xc, axis=-1, keepdims=True) + EPS)
    return (y * w.astype(jnp.float32) + b.astype(jnp.float32)).astype(x.dtype)


def modulate(h, shift, scale):
    return h * (1.0 + scale) + shift


def swiglu(h, w_gate, w_up, w_down):
    return (jax.nn.silu(h @ w_gate) * (h @ w_up)) @ w_down


def rope_angles(p_seq, p_row, p_col):
    parts = []
    for p, n in zip((p_seq, p_row, p_col), ROPE_PAIRS):
        freq = ROPE_BASE ** (-jnp.arange(n, dtype=jnp.float32) / n)
        parts.append(p[:, None] * freq[None, :])
    return jnp.concatenate(parts, axis=-1)


def apply_rope(t, ang):
    t32 = t.astype(jnp.float32)
    half = t32.shape[-1] // 2
    t1, t2 = t32[..., :half], t32[..., half:]
    cos, sin = jnp.cos(ang), jnp.sin(ang)
    return jnp.concatenate([t1 * cos - t2 * sin, t1 * sin + t2 * cos], axis=-1)


def split_heads(t):
    b, l, _ = t.shape
    return t.reshape(b, l, RET_HEADS, RET_HEAD_DIM).transpose(0, 2, 1, 3)


def retention_chunkwise(q, k, v, log_g, s0, inclusive):
    b, h, l, dk = q.shape
    dv = v.shape[-1]
    n = l // RET_CHUNK
    qc = q.reshape(b, h, n, RET_CHUNK, dk)
    kc = k.reshape(b, h, n, RET_CHUNK, dk)
    vc = v.reshape(b, h, n, RET_CHUNK, dv)
    pos = jnp.arange(RET_CHUNK, dtype=jnp.float32)
    diff = pos[:, None] - pos[None, :]
    if inclusive:
        mask = diff >= 0
        expo = diff
        xi_exp = pos + 1.0
    else:
        mask = diff > 0
        expo = diff - 1.0
        xi_exp = pos
    lg = log_g[:, None, None]
    dmat = jnp.where(mask[None], jnp.exp(lg * jnp.where(mask, expo, 0.0)[None]), 0.0)
    xi = jnp.exp(log_g[:, None] * xi_exp[None, :])
    zeta = jnp.exp(log_g[:, None] * (RET_CHUNK - 1.0 - pos)[None, :])
    chunk_decay = jnp.exp(log_g * RET_CHUNK)[None, :, None, None]
    scores = jnp.einsum('bhncd,bhnmd->bhncm', qc, kc) * dmat[None, :, None]
    intra = jnp.einsum('bhncm,bhnme->bhnce', scores, vc)
    kv = jnp.einsum('bhnmd,hm,bhnme->bhnde', kc, zeta, vc)

    def step(s, kv_n):
        return s * chunk_decay + kv_n, s

    _, s_prev = lax.scan(step, s0, jnp.moveaxis(kv, 2, 0))
    inter = jnp.einsum('bhncd,hc,nbhde->bhnce', qc, xi, s_prev)
    return (intra + inter).reshape(b, h, l, dv)


def retention_final_states(k, v, log_g2):
    l = k.shape[2]
    pos = jnp.arange(l, dtype=jnp.float32)
    w_f = jnp.exp(log_g2[0][:, None] * (l - 1.0 - pos)[None, :])
    w_b = jnp.exp(log_g2[1][:, None] * pos[None, :])
    s_f = jnp.einsum('bhld,hl,bhle->bhde', k, w_f, v)
    s_b = jnp.einsum('bhld,hl,bhle->bhde', k, w_b, v)
    return s_f, s_b


def bidir_retention(q, k, v, log_g2, s_f, s_b):
    flip = lambda t: jnp.flip(t, axis=2)
    fwd = retention_chunkwise(q, k, v, log_g2[0], s_f, True)
    bwd = flip(retention_chunkwise(flip(q), flip(k), flip(v), log_g2[1], s_b, False))
    return fwd + bwd


def retention_kv(h, w_in, ang):
    k = apply_rope(split_heads(h @ w_in[:, RET_W:2 * RET_W]), ang)
    v = split_heads(h @ w_in[:, 2 * RET_W:3 * RET_W]).astype(jnp.float32)
    return k, v


def even_project(h, w_in, ang):
    p = h @ w_in
    q, k, v, g, a, gb = jnp.split(p, [RET_W, 2 * RET_W, 3 * RET_W, 4 * RET_W, 4 * RET_W + CONV_CH], axis=-1)
    q = apply_rope(split_heads(q), ang) * (RET_HEAD_DIM ** -0.5)
    k = apply_rope(split_heads(k), ang)
    v = split_heads(v).astype(jnp.float32)
    return q, k, v, g, a * jax.nn.sigmoid(gb)


def depthwise_conv(u, w):
    return lax.conv_general_dilated(
        u, w.astype(u.dtype)[:, None, :], window_strides=(1,),
        padding=[(CONV_K // 2, CONV_K // 2)],
        dimension_numbers=('NWC', 'WIO', 'NWC'), feature_group_count=u.shape[-1])


def even_mix(q, k, v, g, u, s_f, s_b, log_g2, conv_w, ln_w, ln_b, w_out):
    b, _, l, _ = q.shape
    y = bidir_retention(q, k, v, log_g2, s_f, s_b)
    y = y * lax.rsqrt(jnp.mean(y * y, axis=-1, keepdims=True) + EPS)
    y = y.transpose(0, 2, 1, 3).reshape(b, l, RET_W).astype(g.dtype)
    ret_out = jax.nn.silu(g) * y
    conv_out = jax.nn.silu(layer_norm(depthwise_conv(u, conv_w), ln_w, ln_b))
    return jnp.concatenate([ret_out, conv_out], axis=-1) @ w_out


def pool_minus_token(h32):
    b, l, _ = h32.shape
    cs = jnp.concatenate([jnp.zeros((b, 1, POOL_CH), jnp.float32), lax.cumsum(h32, axis=1)], axis=1)
    t = jnp.arange(l)
    outs = []
    for gi, w in enumerate(POOL_WINDOWS):
        left = w // 2
        right = w - 1 - left
        lo = jnp.clip(t - left, 0, l - 1)
        hi = jnp.clip(t + right, 0, l - 1)
        csg = cs[..., gi * POOL_GC:(gi + 1) * POOL_GC]
        mean = (csg[:, hi + 1] - csg[:, lo]) / (hi - lo + 1).astype(jnp.float32)[None, :, None]
        outs.append(mean - h32[..., gi * POOL_GC:(gi + 1) * POOL_GC])
    return jnp.stack(outs, axis=2)


def odd_stream(h, w_in, w_out, pool_w, pool_scale, sg_ln_w, sg_ln_b, sg_w, sg_b):
    b, l, _ = h.shape
    p = h @ w_in
    pc, pd = p[..., :POOL_CH], p[..., POOL_CH:]
    m = pool_minus_token(pc.astype(jnp.float32))
    pool_out = jnp.einsum('blgc,gcd->blgd', m, pool_w.astype(jnp.float32)).reshape(b, l, POOL_CH)
    pool_out = pool_out.astype(h.dtype) * pool_scale
    z = jax.nn.gelu(pd, approximate=False)
    u, v = jnp.split(z, 2, axis=-1)
    v = layer_norm(v, sg_ln_w, sg_ln_b).reshape(b, l // SG_CHUNK, SG_CHUNK, SG_GROUPS, SG_GC)
    s = jnp.einsum('bnpgc,gqp->bnqgc', v, sg_w) + sg_b.T[None, None, :, :, None]
    sg_out = u * s.reshape(b, l, SG_CH)
    return jnp.concatenate([pool_out, sg_out], axis=-1) @ w_out


def setup_inputs(seed: int = 0) -> dict:
    key = jax.random.key(seed)
    ks = jax.random.split(key, 25)
    f32 = jnp.float32
    nrm = lambda k, shape, s: jax.random.normal(k, shape, f32) * s
    gamma0 = 1.0 - 2.0 ** (-5.0 - np.arange(RET_HEADS))
    decay_base = jnp.asarray(np.log(gamma0 / (1.0 - gamma0)), f32)
    return {
        'x': nrm(ks[0], (BATCH, SEQ, D_MODEL), 1.0),
        'c': nrm(ks[1], (BATCH, D_MODEL), 1.0),
        'ctx': nrm(ks[2], (BATCH, CTX_LEN, D_MODEL), 1.0),
        'c_ctx': nrm(ks[3], (D_MODEL,), 1.0),
        'ada_w': nrm(ks[4], (DEPTH, D_MODEL, 6 * D_MODEL), 0.5 * D_MODEL ** -0.5),
        'ada_b': nrm(ks[5], (DEPTH, 6 * D_MODEL), 0.02),
        'norm_w': 1.0 + nrm(ks[6], (DEPTH, 2, D_MODEL), 0.02),
        'even_w_in': nrm(ks[7], (N_EVEN, D_MODEL, EVEN_IN), D_MODEL ** -0.5),
        'even_w_out': nrm(ks[8], (N_EVEN, EVEN_MIX, D_MODEL), EVEN_MIX ** -0.5),
        'ret_decay_logit': decay_base + nrm(ks[9], (N_EVEN, 2, RET_HEADS), 0.1),
        'conv_dw_w': nrm(ks[10], (N_EVEN, CONV_K, CONV_CH), CONV_K ** -0.5),
        'conv_ln_w': 1.0 + nrm(ks[11], (N_EVEN, CONV_CH), 0.02),
        'conv_ln_b': nrm(ks[12], (N_EVEN, CONV_CH), 0.02),
        'odd_w_in': nrm(ks[13], (N_ODD, D_MODEL, ODD_IN), D_MODEL ** -0.5),
        'odd_w_out': nrm(ks[14], (N_ODD, ODD_MIX, D_MODEL), ODD_MIX ** -0.5),
        'pool_w': nrm(ks[15], (N_ODD, POOL_GROUPS, POOL_GC, POOL_GC), POOL_GC ** -0.5),
        'pool_scale': 1.0 + nrm(ks[16], (N_ODD, POOL_CH), 0.02),
        'sg_ln_w': 1.0 + nrm(ks[17], (N_ODD, SG_CH), 0.02),
        'sg_ln_b': nrm(ks[18], (N_ODD, SG_CH), 0.02),
        'sg_w': nrm(ks[19], (N_ODD, SG_GROUPS, SG_CHUNK, SG_CHUNK), 0.5 * SG_CHUNK ** -0.5),
        'sg_b': 1.0 + nrm(ks[20], (N_ODD, SG_GROUPS, SG_CHUNK), 0.02),
        'ffn_w_gate': nrm(ks[21], (DEPTH, D_MODEL, D_FF), D_MODEL ** -0.5),
        'ffn_w_up': nrm(ks[22], (DEPTH, D_MODEL, D_FF), D_MODEL ** -0.5),
        'ffn_w_down': nrm(ks[23], (DEPTH, D_FF, D_MODEL), D_FF ** -0.5),
        'final_norm_w': 1.0 + nrm(ks[24], (D_MODEL,), 0.02),
    }


def reference(x, c, ctx, c_ctx, ada_w, ada_b, norm_w, even_w_in, even_w_out, ret_decay_logit,
              conv_dw_w, conv_ln_w, conv_ln_b, odd_w_in, odd_w_out, pool_w, pool_scale,
              sg_ln_w, sg_ln_b, sg_w, sg_b, ffn_w_gate, ffn_w_up, ffn_w_down, final_norm_w):
    b, l, d = x.shape
    lc = ctx.shape[1]
    rows = l // GRID_W
    grid_r = jnp.broadcast_to(jnp.arange(rows, dtype=jnp.float32)[:, None], (rows, GRID_W)).reshape(-1)
    grid_c = jnp.broadcast_to(jnp.arange(GRID_W, dtype=jnp.float32)[None, :], (rows, GRID_W)).reshape(-1)
    ang_x = rope_angles(jnp.full((l,), lc, jnp.float32), grid_r, grid_c)
    zeros_c = jnp.zeros((lc,), jnp.float32)
    ang_c = rope_angles(jnp.arange(lc, dtype=jnp.float32), zeros_c, zeros_c)
    silu_c = jax.nn.silu(c)
    silu_cc = jax.nn.silu(c_ctx)

    for i in range(DEPTH):
        j = i // 2
        even = i % 2 == 0
        ctx_after = any(m % 2 == 0 for m in range(i + 1, DEPTH))
        use_ctx = even or ctx_after
        mod_x = (silu_c @ ada_w[i] + ada_b[i]).reshape(b, 6, 1, d)
        hx = modulate(rms_norm(x, norm_w[i, 0]), mod_x[:, 0], mod_x[:, 1])
        if use_ctx:
            mod_c = (silu_cc @ ada_w[i] + ada_b[i]).reshape(6, 1, d)
            hc = modulate(rms_norm(ctx, norm_w[i, 0]), mod_c[0], mod_c[1])
        if even:
            log_g2 = jax.nn.log_sigmoid(ret_decay_logit[j].astype(jnp.float32))
            if ctx_after:
                parts_c = even_project(hc, even_w_in[j], ang_c)
                kc, vc = parts_c[1], parts_c[2]
            else:
                kc, vc = retention_kv(hc, even_w_in[j], ang_c)
            s_f, s_b = retention_final_states(kc, vc, log_g2)
            yx = even_mix(*even_project(hx, even_w_in[j], ang_x), s_f, s_b, log_g2,
                          conv_dw_w[j], conv_ln_w[j], conv_ln_b[j], even_w_out[j])
            if ctx_after:
                s_zero = jnp.zeros_like(s_f)
                yc = even_mix(*parts_c, s_zero, s_zero, log_g2,
                              conv_dw_w[j], conv_ln_w[j], conv_ln_b[j], even_w_out[j])
        else:
            yx = odd_stream(hx, odd_w_in[j], odd_w_out[j], pool_w[j], pool_scale[j],
                            sg_ln_w[j], sg_ln_b[j], sg_w[j], sg_b[j])
            if ctx_after:
                yc = odd_stream(hc, odd_w_in[j], odd_w_out[j], pool_w[j], pool_scale[j],
                                sg_ln_w[j], sg_ln_b[j], sg_w[j], sg_b[j])
        x = x + mod_x[:, 2] * yx
        x = x + mod_x[:, 5] * swiglu(modulate(rms_norm(x, norm_w[i, 1]), mod_x[:, 3], mod_x[:, 4]),
                                     ffn_w_gate[i], ffn_w_up[i], ffn_w_down[i])
        if ctx_after:
            ctx = ctx + mod_c[2] * yc
            ctx = ctx + mod_c[5] * swiglu(modulate(rms_norm(ctx, norm_w[i, 1]), mod_c[3], mod_c[4]),
                                          ffn_w_gate[i], ffn_w_up[i], ffn_w_down[i])
    return rms_norm(x, final_norm_w)
```

```python
import functools

import jax
import jax.numpy as jnp
from jax import lax
from jax.experimental import pallas as pl
from jax.experimental.pallas import tpu as pltpu

F32 = jnp.float32
BF16 = jnp.bfloat16

D_MODEL = 1024
DEPTH = 4
GRID_W = 64
EPS = 1e-6
HEADS = 4
HEAD_DIM = 128
RET_W = HEADS * HEAD_DIM
CHUNK = 128
ROPE_BASE = 10000.0
ROPE_PAIRS = (HEAD_DIM // 8, 3 * HEAD_DIM // 16, 3 * HEAD_DIM // 16)
CONV_CH = 512
CONV_K = 31
CONV_PAD = CONV_K // 2
EVEN_IN = 4 * RET_W + 2 * CONV_CH
POOL_CH = 512
POOL_WINDOWS = (2, 4, 8, 16)
POOL_GC = 128
SG_CH = 512
SG_GROUPS = 4
SG_GC = 128
ODD_IN = POOL_CH + 2 * SG_CH
D_FF = 2816
FF_BLOCK = 256
HALO = 16
N_DECAY_TABLES = 7
MOD_ROWS = 8
VMEM_LIMIT = 56 * 1024 * 1024


def _dot(a, b):
    return jnp.dot(a, b, preferred_element_type=F32)


def _dot_nt(a, b):
    return lax.dot_general(a, b, (((1,), (1,)), ((), ())), preferred_element_type=F32)


def _dot_tn(a, b):
    return lax.dot_general(a, b, (((0,), (0,)), ((), ())), preferred_element_type=F32)


def _silu(x):
    return x * jax.nn.sigmoid(x)


def _rms_mod(x32, nw, shift, scale):
    y = x32 * lax.rsqrt(jnp.mean(x32 * x32, axis=-1, keepdims=True) + EPS) * nw
    return y * (1.0 + scale) + shift


def _layer_norm(x32, w, b):
    mu = jnp.mean(x32, axis=-1, keepdims=True)
    xc = x32 - mu
    return xc * lax.rsqrt(jnp.mean(xc * xc, axis=-1, keepdims=True) + EPS) * w + b


def _const_spec(shape):
    return pl.BlockSpec(shape, lambda *_: (0,) * len(shape))


def _params(sem):
    return pltpu.CompilerParams(dimension_semantics=sem, vmem_limit_bytes=VMEM_LIMIT)


def _mod_kernel(c_ref, w_ref, b_ref, o_ref):
    s = _silu(c_ref[...])
    o_ref[0] = jnp.dot(s, w_ref[0], preferred_element_type=F32,
                       precision=lax.Precision.HIGHEST) + b_ref[0]


def _mod_call(cvec, ada_w, ada_b):
    depth, d, n = ada_w.shape
    tn = 1024
    return pl.pallas_call(
        _mod_kernel,
        grid=(depth, n // tn),
        in_specs=[_const_spec((MOD_ROWS, d)),
                  pl.BlockSpec((1, d, tn), lambda i, j: (i, 0, j)),
                  pl.BlockSpec((1, 1, tn), lambda i, j: (i, 0, j))],
        out_specs=pl.BlockSpec((1, MOD_ROWS, tn), lambda i, j: (i, 0, j)),
        out_shape=jax.ShapeDtypeStruct((depth, MOD_ROWS, n), F32),
        compiler_params=_params(("arbitrary", "arbitrary")),
        name="adaln_mod",
    )(cvec, ada_w, ada_b.reshape(depth, 1, n))


def _log_sigmoid(x):
    return -(jnp.maximum(-x, 0.0) + jnp.log1p(jnp.exp(-jnp.abs(x))))


def _decay_kernel(logit_ref, o_ref):
    j = pl.program_id(0)
    row = lax.broadcasted_iota(jnp.int32, (CHUNK, CHUNK), 0).astype(F32)
    col = lax.broadcasted_iota(jnp.int32, (CHUNK, CHUNK), 1).astype(F32)
    diff = row - col
    lower = diff >= 0.0
    for hd in range(HEADS):
        lgf = _log_sigmoid(jnp.full((CHUNK, CHUNK), logit_ref[(j * 2 + 0) * HEADS + hd], F32))
        lgb = _log_sigmoid(jnp.full((CHUNK, CHUNK), logit_ref[(j * 2 + 1) * HEADS + hd], F32))
        o_ref[0, 0, hd] = jnp.where(lower, jnp.exp(lgf * jnp.where(lower, diff, 0.0)),
                                    jnp.exp(lgb * jnp.where(lower, 0.0, -diff - 1.0)))
        o_ref[0, 1, hd] = jnp.exp(lgf * (CHUNK - 1.0 - row))
        o_ref[0, 2, hd] = jnp.exp(lgb * row)
        o_ref[0, 3, hd] = jnp.exp(lgf * (row + 1.0))
        o_ref[0, 4, hd] = jnp.exp(lgb * (CHUNK - 1.0 - row))
        o_ref[0, 5, hd] = jnp.exp(lgf * float(CHUNK))
        o_ref[0, 6, hd] = jnp.exp(lgb * float(CHUNK))


def _decay_call(ret_decay_logit):
    n_even = ret_decay_logit.shape[0]
    return pl.pallas_call(
        _decay_kernel,
        grid=(n_even,),
        in_specs=[pl.BlockSpec(memory_space=pltpu.SMEM)],
        out_specs=pl.BlockSpec((1, N_DECAY_TABLES, HEADS, CHUNK, CHUNK), lambda j: (j, 0, 0, 0, 0)),
        out_shape=jax.ShapeDtypeStruct((n_even, N_DECAY_TABLES, HEADS, CHUNK, CHUNK), F32),
        compiler_params=_params(("arbitrary",)),
        name="ret_decay",
    )(ret_decay_logit.reshape(-1).astype(F32))


def _even_in_kernel(x_ref, mod_ref, nw_ref, w_ref, cos_ref, sin_ref, s0_ref, dt_ref,
                    q_ref, k_ref, v_ref, gs_ref, u_ref, sprev_ref, sfin_ref, state_ref, *, tm):
    b = pl.program_id(0)
    i = pl.program_id(1)

    @pl.when(i == 0)
    def _():
        state_ref[...] = s0_ref[0]

    h = _rms_mod(x_ref[0], nw_ref[...], mod_ref[b * 6 + 0], mod_ref[b * 6 + 1]).astype(BF16)
    cosf = cos_ref[...]
    sinf = sin_ref[...]

    def rope(t):
        return t * cosf + pltpu.roll(t, HEAD_DIM // 2, 1) * sinf

    pq = _dot(h, w_ref[:, 0:RET_W])
    pk = _dot(h, w_ref[:, RET_W:2 * RET_W])
    pv = _dot(h, w_ref[:, 2 * RET_W:3 * RET_W])
    scale = HEAD_DIM ** -0.5
    for hd in range(HEADS):
        sl = slice(hd * HEAD_DIM, (hd + 1) * HEAD_DIM)
        q_ref[0, :, sl] = (rope(pq[:, sl]) * scale).astype(BF16)
        kr = rope(pk[:, sl]).astype(BF16)
        k_ref[0, :, sl] = kr
        vh = pv[:, sl]
        v_ref[0, :, sl] = vh.astype(BF16)
        zeta = dt_ref[1, hd]
        cdec = dt_ref[5, hd]
        for c in range(tm // CHUNK):
            rows = slice(c * CHUNK, (c + 1) * CHUNK)
            s_cur = state_ref[hd]
            sprev_ref[0, c, hd] = s_cur.astype(BF16)
            wv = (vh[rows] * zeta).astype(BF16)
            state_ref[hd] = s_cur * cdec + _dot_tn(kr[rows], wv)

    pg = _dot(h, w_ref[:, 3 * RET_W:4 * RET_W])
    gs_ref[0] = _silu(pg)
    pa = _dot(h, w_ref[:, 4 * RET_W:4 * RET_W + CONV_CH])
    pb = _dot(h, w_ref[:, 4 * RET_W + CONV_CH:EVEN_IN])
    u_ref[0] = pa * jax.nn.sigmoid(pb)

    @pl.when(i == pl.num_programs(1) - 1)
    def _():
        sfin_ref[0] = state_ref[...]


def _even_in_call(x, mod, nw, w_in, cosf, sinf, s0, dtab, tm):
    bsz, l, d = x.shape
    n_chunks = l // CHUNK
    cpt = tm // CHUNK
    tok = lambda w: pl.BlockSpec((1, tm, w), lambda b, i: (b, i, 0))
    state_spec = pl.BlockSpec((1, HEADS, CHUNK, CHUNK), lambda b, i: (b, 0, 0, 0))
    return pl.pallas_call(
        functools.partial(_even_in_kernel, tm=tm),
        grid=(bsz, l // tm),
        in_specs=[tok(d),
                  _const_spec(mod.shape),
                  _const_spec((1, d)),
                  _const_spec(w_in.shape),
                  pl.BlockSpec((tm, HEAD_DIM), lambda b, i: (i, 0)),
                  pl.BlockSpec((tm, HEAD_DIM), lambda b, i: (i, 0)),
                  state_spec,
                  _const_spec(dtab.shape)],
        out_specs=[tok(RET_W), tok(RET_W), tok(RET_W), tok(RET_W), tok(CONV_CH),
                   pl.BlockSpec((1, cpt, HEADS, CHUNK, CHUNK), lambda b, i: (b, i, 0, 0, 0)),
                   state_spec],
        out_shape=[jax.ShapeDtypeStruct((bsz, l, RET_W), BF16),
                   jax.ShapeDtypeStruct((bsz, l, RET_W), BF16),
                   jax.ShapeDtypeStruct((bsz, l, RET_W), BF16),
                   jax.ShapeDtypeStruct((bsz, l, RET_W), F32),
                   jax.ShapeDtypeStruct((bsz, l, CONV_CH), F32),
                   jax.ShapeDtypeStruct((bsz, n_chunks, HEADS, CHUNK, CHUNK), BF16),
                   jax.ShapeDtypeStruct((bsz, HEADS, CHUNK, CHUNK), F32)],
        scratch_shapes=[pltpu.VMEM((HEADS, CHUNK, CHUNK), F32)],
        compiler_params=_params(("arbitrary", "arbitrary")),
        name="even_in",
    )(x, mod, nw, w_in, cosf, sinf, s0, dtab)


def _even_mix_kernel(x_ref, q_ref, k_ref, v_ref, gs_ref, u_ref, up_ref, un_ref, sprev_ref, sb0_ref,
                     dt_ref, mod_ref, cw_ref, lnw_ref, lnb_ref, wo_ref,
                     o_ref, sbfin_ref, state_ref, uext_ref, mix_ref, *, tm):
    b = pl.program_id(0)
    i = pl.program_id(1)
    n_tiles = pl.num_programs(1)

    @pl.when(i == 0)
    def _():
        state_ref[...] = sb0_ref[0]

    for c in reversed(range(tm // CHUNK)):
        rows = slice(c * CHUNK, (c + 1) * CHUNK)
        for hd in range(HEADS):
            sl = slice(hd * HEAD_DIM, (hd + 1) * HEAD_DIM)
            qc = q_ref[0, rows, sl]
            kc = k_ref[0, rows, sl]
            vc = v_ref[0, rows, sl]
            s_b = state_ref[hd]
            p = (_dot_nt(qc, kc) * dt_ref[0, hd]).astype(BF16)
            y = _dot(p, vc)
            y = y + _dot(qc, sprev_ref[0, c, hd]) * dt_ref[3, hd]
            y = y + _dot(qc, s_b.astype(BF16)) * dt_ref[4, hd]
            y = y * lax.rsqrt(jnp.mean(y * y, axis=-1, keepdims=True) + EPS)
            mix_ref[rows, sl] = (gs_ref[0, rows, sl] * y).astype(BF16)
            wv = (vc.astype(F32) * dt_ref[2, hd]).astype(BF16)
            state_ref[hd] = s_b * dt_ref[6, hd] + _dot_tn(kc, wv)

    @pl.when(i == n_tiles - 1)
    def _():
        sbfin_ref[0] = state_ref[...]

    uext_ref[0:HALO] = jnp.where(i < n_tiles - 1, up_ref[0], 0.0)
    uext_ref[HALO:HALO + tm] = u_ref[0]
    uext_ref[HALO + tm:2 * HALO + tm] = jnp.where(i > 0, un_ref[0], 0.0)
    rb = 64
    for r in range(tm // rb):
        acc = jnp.zeros((rb, CONV_CH), F32)
        for t in range(CONV_K):
            start = HALO - CONV_PAD + t + r * rb
            acc = acc + uext_ref[start:start + rb] * cw_ref[t:t + 1]
        cv = _silu(_layer_norm(acc, lnw_ref[...], lnb_ref[...]))
        mix_ref[r * rb:(r + 1) * rb, RET_W:RET_W + CONV_CH] = cv.astype(BF16)

    out = _dot(mix_ref[...], wo_ref[...])
    o_ref[0] = x_ref[0] + mod_ref[b * 6 + 2] * out


def _even_mix_call(x, q, k, v, gs, u, sprev, sb0, dtab, mod, conv_w, ln_w, ln_b, w_out, tm):
    bsz, l, d = x.shape
    n_tiles = l // tm
    cpt = tm // CHUNK
    hpt = tm // HALO
    n_halo = l // HALO
    rev = lambda i: n_tiles - 1 - i
    tok = lambda w: pl.BlockSpec((1, tm, w), lambda b, i: (b, rev(i), 0))
    state_spec = pl.BlockSpec((1, HEADS, CHUNK, CHUNK), lambda b, i: (b, 0, 0, 0))
    return pl.pallas_call(
        functools.partial(_even_mix_kernel, tm=tm),
        grid=(bsz, n_tiles),
        in_specs=[tok(d), tok(RET_W), tok(RET_W), tok(RET_W), tok(RET_W), tok(CONV_CH),
                  pl.BlockSpec((1, HALO, CONV_CH), lambda b, i: (b, jnp.maximum(rev(i) * hpt - 1, 0), 0)),
                  pl.BlockSpec((1, HALO, CONV_CH),
                               lambda b, i: (b, jnp.minimum((rev(i) + 1) * hpt, n_halo - 1), 0)),
                  pl.BlockSpec((1, cpt, HEADS, CHUNK, CHUNK), lambda b, i: (b, rev(i), 0, 0, 0)),
                  state_spec,
                  _const_spec(dtab.shape),
                  _const_spec(mod.shape),
                  _const_spec(conv_w.shape),
                  _const_spec((1, CONV_CH)),
                  _const_spec((1, CONV_CH)),
                  _const_spec(w_out.shape)],
        out_specs=[tok(d), state_spec],
        out_shape=[jax.ShapeDtypeStruct((bsz, l, d), F32),
                   jax.ShapeDtypeStruct((bsz, HEADS, CHUNK, CHUNK), F32)],
        scratch_shapes=[pltpu.VMEM((HEADS, CHUNK, CHUNK), F32),
                        pltpu.VMEM((tm + 2 * HALO, CONV_CH), F32),
                        pltpu.VMEM((tm, RET_W + CONV_CH), BF16)],
        compiler_params=_params(("arbitrary", "arbitrary")),
        name="even_mix",
    )(x, q, k, v, gs, u, u, u, sprev, sb0, dtab, mod, conv_w, ln_w, ln_b, w_out)


def _odd_kernel(x_ref, xp_ref, xn_ref, mod_ref, nw_ref, wi_ref, pw_ref, ps_ref, lnw_ref, lnb_ref,
                sgw_ref, sgb_ref, wo_ref, o_ref, h_ref, pcx_ref, mix_ref, *, tm, seq_len):
    b = pl.program_id(0)
    i = pl.program_id(1)
    n_tiles = pl.num_programs(1)
    nw = nw_ref[...]
    shift = mod_ref[b * 6 + 0]
    scale = mod_ref[b * 6 + 1]

    h_ref[0:HALO] = _rms_mod(xp_ref[0], nw, shift, scale).astype(BF16)
    h_ref[HALO:HALO + tm] = _rms_mod(x_ref[0], nw, shift, scale).astype(BF16)
    h_ref[HALO + tm:2 * HALO + tm] = _rms_mod(xn_ref[0], nw, shift, scale).astype(BF16)

    pcx_ref[...] = _dot(h_ref[...], wi_ref[:, 0:POOL_CH])
    pcx_ref[0:HALO] = jnp.where(i > 0, pcx_ref[0:HALO], 0.0)
    pcx_ref[HALO + tm:2 * HALO + tm] = jnp.where(i < n_tiles - 1, pcx_ref[HALO + tm:2 * HALO + tm], 0.0)
    pos = i * tm + lax.broadcasted_iota(jnp.int32, (tm, POOL_GC), 0)
    for gi, w in enumerate(POOL_WINDOWS):
        left = w // 2
        right = w - 1 - left
        sl = slice(gi * POOL_GC, (gi + 1) * POOL_GC)
        win = pcx_ref[HALO - left:HALO - left + tm, sl]
        for o in range(-left + 1, right + 1):
            win = win + pcx_ref[HALO + o:HALO + o + tm, sl]
        cnt = jnp.minimum(pos + right, seq_len - 1) - jnp.maximum(pos - left, 0) + 1
        m = win / cnt.astype(F32) - pcx_ref[HALO:HALO + tm, sl]
        po = _dot(m.astype(BF16), pw_ref[gi]) * ps_ref[:, sl]
        mix_ref[:, sl] = po.astype(BF16)

    pd = _dot(h_ref[HALO:HALO + tm], wi_ref[:, POOL_CH:ODD_IN])
    z = 0.5 * pd * (1.0 + lax.erf(pd * (2.0 ** -0.5)))
    zu = z[:, 0:SG_CH]
    zv = _layer_norm(z[:, SG_CH:2 * SG_CH], lnw_ref[...], lnb_ref[...]).astype(BF16)
    for c in range(tm // CHUNK):
        rows = slice(c * CHUNK, (c + 1) * CHUNK)
        for g in range(SG_GROUPS):
            sl = slice(g * SG_GC, (g + 1) * SG_GC)
            s = _dot(sgw_ref[g], zv[rows, sl]) + sgb_ref[g]
            mix_ref[rows, POOL_CH + g * SG_GC:POOL_CH + (g + 1) * SG_GC] = (zu[rows, sl] * s).astype(BF16)

    out = _dot(mix_ref[...], wo_ref[...])
    o_ref[0] = x_ref[0] + mod_ref[b * 6 + 2] * out


def _odd_call(x, mod, nw, w_in, pool_w, pool_scale, ln_w, ln_b, sg_w, sg_b_full, w_out, tm):
    bsz, l, d = x.shape
    hpt = tm // HALO
    n_halo = l // HALO
    tok = pl.BlockSpec((1, tm, d), lambda b, i: (b, i, 0))
    return pl.pallas_call(
        functools.partial(_odd_kernel, tm=tm, seq_len=l),
        grid=(bsz, l // tm),
        in_specs=[tok,
                  pl.BlockSpec((1, HALO, d), lambda b, i: (b, jnp.maximum(i * hpt - 1, 0), 0)),
                  pl.BlockSpec((1, HALO, d), lambda b, i: (b, jnp.minimum((i + 1) * hpt, n_halo - 1), 0)),
                  _const_spec(mod.shape),
                  _const_spec((1, d)),
                  _const_spec(w_in.shape),
                  _const_spec(pool_w.shape),
                  _const_spec((1, POOL_CH)),
                  _const_spec((1, SG_CH)),
                  _const_spec((1, SG_CH)),
                  _const_spec(sg_w.shape),
                  _const_spec(sg_b_full.shape),
                  _const_spec(w_out.shape)],
        out_specs=tok,
        out_shape=jax.ShapeDtypeStruct((bsz, l, d), F32),
        scratch_shapes=[pltpu.VMEM((tm + 2 * HALO, d), BF16),
                        pltpu.VMEM((tm + 2 * HALO, POOL_CH), F32),
                        pltpu.VMEM((tm, POOL_CH + SG_CH), BF16)],
        compiler_params=_params(("arbitrary", "arbitrary")),
        name="odd_mix",
    )(x, x, x, mod, nw, w_in, pool_w, pool_scale, ln_w, ln_b, sg_w, sg_b_full, w_out)


def _ffn_kernel(x_ref, mod_ref, nw_ref, wg_ref, wu_ref, wd_ref, fnw_ref, o_ref, acc_ref, *, final_norm):
    b = pl.program_id(0)
    x32 = x_ref[0]
    h = _rms_mod(x32, nw_ref[...], mod_ref[b * 6 + 3], mod_ref[b * 6 + 4]).astype(BF16)
    acc_ref[...] = jnp.zeros_like(acc_ref)

    def step(j, carry):
        act = _silu(_dot(h, wg_ref[j])) * _dot(h, wu_ref[j])
        acc_ref[...] += _dot(act.astype(BF16), wd_ref[j])
        return carry

    lax.fori_loop(0, D_FF // FF_BLOCK, step, 0)
    y = x32 + mod_ref[b * 6 + 5] * acc_ref[...]
    if final_norm:
        y = y * lax.rsqrt(jnp.mean(y * y, axis=-1, keepdims=True) + EPS) * fnw_ref[...]
    o_ref[0] = y


def _ffn_call(x, mod, nw, wg, wu, wd, fnw, tm, final_norm):
    bsz, l, d = x.shape
    tok = pl.BlockSpec((1, tm, d), lambda b, i: (b, i, 0))
    return pl.pallas_call(
        functools.partial(_ffn_kernel, final_norm=final_norm),
        grid=(bsz, l // tm),
        in_specs=[tok,
                  _const_spec(mod.shape),
                  _const_spec((1, d)),
                  _const_spec(wg.shape),
                  _const_spec(wu.shape),
                  _const_spec(wd.shape),
                  _const_spec((1, d))],
        out_specs=tok,
        out_shape=jax.ShapeDtypeStruct((bsz, l, d), F32),
        scratch_shapes=[pltpu.VMEM((tm, d), F32)],
        compiler_params=_params(("arbitrary", "arbitrary")),
        name="ffn",
    )(x, mod, nw, wg, wu, wd, fnw)


def _rope_tables(p_seq, p_row, p_col):
    parts = []
    for p, n in zip((p_seq, p_row, p_col), ROPE_PAIRS):
        freq = ROPE_BASE ** (-jnp.arange(n, dtype=F32) / n)
        parts.append(p[:, None] * freq[None, :])
    ang = jnp.concatenate(parts, axis=-1)
    cos, sin = jnp.cos(ang), jnp.sin(ang)
    return jnp.concatenate([cos, cos], axis=-1), jnp.concatenate([-sin, sin], axis=-1)


def _tile(l):
    return min(l, 256)


def kernel(x, c, ctx, c_ctx, ada_w, ada_b, norm_w, even_w_in, even_w_out, ret_decay_logit, conv_dw_w, conv_ln_w, conv_ln_b, odd_w_in, odd_w_out, pool_w, pool_scale, sg_ln_w, sg_ln_b, sg_w, sg_b, ffn_w_gate, ffn_w_up, ffn_w_down, final_norm_w):
    bsz, l, d = x.shape
    lc = ctx.shape[1]
    assert d == D_MODEL and l % 256 == 0 and lc % CHUNK == 0 and bsz + 1 <= MOD_ROWS
    rows = l // GRID_W
    grid_r = jnp.broadcast_to(jnp.arange(rows, dtype=F32)[:, None], (rows, GRID_W)).reshape(-1)
    grid_c = jnp.broadcast_to(jnp.arange(GRID_W, dtype=F32)[None, :], (rows, GRID_W)).reshape(-1)
    cos_x, sin_x = _rope_tables(jnp.full((l,), lc, F32), grid_r, grid_c)
    zeros_c = jnp.zeros((lc,), F32)
    cos_c, sin_c = _rope_tables(jnp.arange(lc, dtype=F32), zeros_c, zeros_c)

    cvec = jnp.concatenate([c, c_ctx[None, :], jnp.zeros((MOD_ROWS - bsz - 1, d), F32)], axis=0)
    mod_all = _mod_call(cvec, ada_w, ada_b)
    dtabs = _decay_call(ret_decay_logit)
    zero_state = jnp.zeros((bsz, HEADS, CHUNK, CHUNK), F32)
    fnw = final_norm_w.reshape(1, d)
    tx, tc = _tile(l), _tile(lc)
    ffn_tx, ffn_tc = min(l, 512), min(lc, 512)

    for i in range(DEPTH):
        j = i // 2
        even = i % 2 == 0
        ctx_after = any(m % 2 == 0 for m in range(i + 1, DEPTH))
        mod_x = mod_all[i, :bsz].reshape(bsz * 6, 1, d)
        mod_c = jnp.broadcast_to(mod_all[i, bsz].reshape(1, 6, 1, d), (bsz, 6, 1, d)).reshape(bsz * 6, 1, d)
        nw1 = norm_w[i, 0].reshape(1, d)
        nw2 = norm_w[i, 1].reshape(1, d)
        nblk = D_FF // FF_BLOCK
        wg = ffn_w_gate[i].astype(BF16).reshape(d, nblk, FF_BLOCK).transpose(1, 0, 2)
        wu = ffn_w_up[i].astype(BF16).reshape(d, nblk, FF_BLOCK).transpose(1, 0, 2)
        wd = ffn_w_down[i].astype(BF16).reshape(nblk, FF_BLOCK, d)
        if even:
            w_in = even_w_in[j].astype(BF16)
            w_out = even_w_out[j].astype(BF16)
            conv_w = conv_dw_w[j]
            lnw = conv_ln_w[j].reshape(1, CONV_CH)
            lnb = conv_ln_b[j].reshape(1, CONV_CH)
            dtab = dtabs[j]
            qc, kc, vc, gc, uc, spc, s_f = _even_in_call(ctx, mod_c, nw1, w_in, cos_c, sin_c, zero_state, dtab, tc)
            yc, s_b = _even_mix_call(ctx, qc, kc, vc, gc, uc, spc, zero_state, dtab, mod_c,
                                     conv_w, lnw, lnb, w_out, tc)
            qx, kx, vx, gx, ux, spx, _ = _even_in_call(x, mod_x, nw1, w_in, cos_x, sin_x, s_f, dtab, tx)
            x, _ = _even_mix_call(x, qx, kx, vx, gx, ux, spx, s_b, dtab, mod_x, conv_w, lnw, lnb, w_out, tx)
            if ctx_after:
                ctx = yc
        else:
            w_in = odd_w_in[j].astype(BF16)
            w_out = odd_w_out[j].astype(BF16)
            args = (nw1, w_in, pool_w[j].astype(BF16), pool_scale[j].reshape(1, POOL_CH),
                    sg_ln_w[j].reshape(1, SG_CH), sg_ln_b[j].reshape(1, SG_CH), sg_w[j].astype(BF16),
                    jnp.broadcast_to(sg_b[j][:, :, None], (SG_GROUPS, CHUNK, SG_GC)), w_out)
            if ctx_after:
                ctx = _odd_call(ctx, mod_c, *args, tc)
            x = _odd_call(x, mod_x, *args, tx)
        if ctx_after:
            ctx = _ffn_call(ctx, mod_c, nw2, wg, wu, wd, fnw, ffn_tc, False)
        x = _ffn_call(x, mod_x, nw2, wg, wu, wd, fnw, ffn_tx, i == DEPTH - 1)
    return x
```

```python
import functools

import jax
import jax.numpy as jnp
from jax import lax
from jax.experimental import pallas as pl
from jax.experimental.pallas import tpu as pltpu

F32 = jnp.float32
BF16 = jnp.bfloat16

D_MODEL = 1024
DEPTH = 4
GRID_W = 64
EPS = 1e-6
HEADS = 4
HEAD_DIM = 128
RET_W = HEADS * HEAD_DIM
CHUNK = 128
ROPE_BASE = 10000.0
ROPE_PAIRS = (HEAD_DIM // 8, 3 * HEAD_DIM // 16, 3 * HEAD_DIM // 16)
CONV_CH = 512
CONV_K = 31
CONV_PAD = CONV_K // 2
EVEN_IN = 4 * RET_W + 2 * CONV_CH
POOL_CH = 512
POOL_WINDOWS = (2, 4, 8, 16)
POOL_GC = 128
SG_CH = 512
SG_GROUPS = 4
SG_GC = 128
ODD_IN = POOL_CH + 2 * SG_CH
D_FF = 2816
SUBLANES = 8
HALO = 16
N_DECAY_TABLES = 7
MOD_ROWS = 8
VMEM_LIMIT = 56 * 1024 * 1024


def _dot(a, b):
    return jnp.dot(a, b, preferred_element_type=F32)


def _dot_nt(a, b):
    return lax.dot_general(a, b, (((1,), (1,)), ((), ())), preferred_element_type=F32)


def _dot_tn(a, b):
    return lax.dot_general(a, b, (((0,), (0,)), ((), ())), preferred_element_type=F32)


def _silu(x):
    return x * jax.nn.sigmoid(x)


def _rms_mod(x32, nw, shift, scale):
    y = x32 * lax.rsqrt(jnp.mean(x32 * x32, axis=-1, keepdims=True) + EPS) * nw
    return y * (1.0 + scale) + shift


def _layer_norm(x32, w, b):
    mu = jnp.mean(x32, axis=-1, keepdims=True)
    xc = x32 - mu
    return xc * lax.rsqrt(jnp.mean(xc * xc, axis=-1, keepdims=True) + EPS) * w + b


def _const_spec(shape):
    return pl.BlockSpec(shape, lambda *_: (0,) * len(shape), pipeline_mode=pl.Buffered(1))


def _params(sem):
    return pltpu.CompilerParams(dimension_semantics=sem, vmem_limit_bytes=VMEM_LIMIT)


def _mod_kernel(c_ref, w_ref, b_ref, o_ref):
    s = _silu(c_ref[...]).astype(BF16)
    o_ref[0] = _dot(s, w_ref[0].astype(BF16)) + b_ref[0]


def _mod_call(cvec, ada_w, ada_b):
    depth, d, n = ada_w.shape
    tn = 1024
    return pl.pallas_call(
        _mod_kernel,
        grid=(depth, n // tn),
        in_specs=[_const_spec((MOD_ROWS, d)),
                  pl.BlockSpec((1, d, tn), lambda i, j: (i, 0, j)),
                  pl.BlockSpec((1, 1, tn), lambda i, j: (i, 0, j))],
        out_specs=pl.BlockSpec((1, MOD_ROWS, tn), lambda i, j: (i, 0, j)),
        out_shape=jax.ShapeDtypeStruct((depth, MOD_ROWS, n), F32),
        compiler_params=_params(("arbitrary", "arbitrary")),
        name="adaln_mod",
    )(cvec, ada_w, ada_b.reshape(depth, 1, n))


def _log_sigmoid(x):
    return -(jnp.maximum(-x, 0.0) + jnp.log1p(jnp.exp(-jnp.abs(x))))


def _decay_kernel(logit_ref, o_ref):
    j = pl.program_id(0)
    row = lax.broadcasted_iota(jnp.int32, (CHUNK, CHUNK), 0).astype(F32)
    col = lax.broadcasted_iota(jnp.int32, (CHUNK, CHUNK), 1).astype(F32)
    diff = row - col
    lower = diff >= 0.0
    for hd in range(HEADS):
        lgf = _log_sigmoid(jnp.full((CHUNK, CHUNK), logit_ref[(j * 2 + 0) * HEADS + hd], F32))
        lgb = _log_sigmoid(jnp.full((CHUNK, CHUNK), logit_ref[(j * 2 + 1) * HEADS + hd], F32))
        o_ref[0, 0, hd] = jnp.where(lower, jnp.exp(lgf * jnp.where(lower, diff, 0.0)),
                                    jnp.exp(lgb * jnp.where(lower, 0.0, -diff - 1.0)))
        o_ref[0, 1, hd] = jnp.exp(lgf * (CHUNK - 1.0 - row))
        o_ref[0, 2, hd] = jnp.exp(lgb * row)
        o_ref[0, 3, hd] = jnp.exp(lgf * (row + 1.0))
        o_ref[0, 4, hd] = jnp.exp(lgb * (CHUNK - 1.0 - row))
        o_ref[0, 5, hd] = jnp.exp(lgf * float(CHUNK))
        o_ref[0, 6, hd] = jnp.exp(lgb * float(CHUNK))


def _decay_call(ret_decay_logit):
    n_even = ret_decay_logit.shape[0]
    return pl.pallas_call(
        _decay_kernel,
        grid=(n_even,),
        in_specs=[pl.BlockSpec(memory_space=pltpu.SMEM)],
        out_specs=pl.BlockSpec((1, N_DECAY_TABLES, HEADS, CHUNK, CHUNK), lambda j: (j, 0, 0, 0, 0)),
        out_shape=jax.ShapeDtypeStruct((n_even, N_DECAY_TABLES, HEADS, CHUNK, CHUNK), F32),
        compiler_params=_params(("arbitrary",)),
        name="ret_decay",
    )(ret_decay_logit.reshape(-1).astype(F32))


def _even_in_kernel(x_ref, mod_ref, nw_ref, w_ref, cos_ref, sin_ref, s0_ref, dt_ref,
                    q_ref, k_ref, v_ref, gs_ref, u_ref, sprev_ref, sfin_ref, state_ref, *, tm):
    b = pl.program_id(0)
    i = pl.program_id(1)

    @pl.when(i == 0)
    def _():
        state_ref[...] = s0_ref[0]

    h = _rms_mod(x_ref[0], nw_ref[...], mod_ref[b * 6 + 0], mod_ref[b * 6 + 1]).astype(BF16)
    cosf = cos_ref[...]
    sinf = sin_ref[...]

    def rope(t):
        return t * cosf + pltpu.roll(t, HEAD_DIM // 2, 1) * sinf

    pq = _dot(h, w_ref[:, 0:RET_W])
    pk = _dot(h, w_ref[:, RET_W:2 * RET_W])
    pv = _dot(h, w_ref[:, 2 * RET_W:3 * RET_W])
    scale = HEAD_DIM ** -0.5
    for hd in range(HEADS):
        sl = slice(hd * HEAD_DIM, (hd + 1) * HEAD_DIM)
        q_ref[0, :, sl] = (rope(pq[:, sl]) * scale).astype(BF16)
        kr = rope(pk[:, sl]).astype(BF16)
        k_ref[0, :, sl] = kr
        vh = pv[:, sl]
        v_ref[0, :, sl] = vh.astype(BF16)
        zeta = dt_ref[1, hd]
        cdec = dt_ref[5, hd]
        for c in range(tm // CHUNK):
            rows = slice(c * CHUNK, (c + 1) * CHUNK)
            s_cur = state_ref[hd]
            sprev_ref[0, c, hd] = s_cur.astype(BF16)
            wv = (vh[rows] * zeta).astype(BF16)
            state_ref[hd] = s_cur * cdec + _dot_tn(kr[rows], wv)

    pg = _dot(h, w_ref[:, 3 * RET_W:4 * RET_W])
    gs_ref[0] = _silu(pg)
    pa = _dot(h, w_ref[:, 4 * RET_W:4 * RET_W + CONV_CH])
    pb = _dot(h, w_ref[:, 4 * RET_W + CONV_CH:EVEN_IN])
    u_ref[0] = pa * jax.nn.sigmoid(pb)

    @pl.when(i == pl.num_programs(1) - 1)
    def _():
        sfin_ref[0] = state_ref[...]


def _even_in_call(x, mod, nw, w_in, cosf, sinf, s0, dtab, tm):
    bsz, l, d = x.shape
    n_chunks = l // CHUNK
    cpt = tm // CHUNK
    tok = lambda w: pl.BlockSpec((1, tm, w), lambda b, i: (b, i, 0))
    state_spec = pl.BlockSpec((1, HEADS, CHUNK, CHUNK), lambda b, i: (b, 0, 0, 0))
    return pl.pallas_call(
        functools.partial(_even_in_kernel, tm=tm),
        grid=(bsz, l // tm),
        in_specs=[tok(d),
                  _const_spec(mod.shape),
                  _const_spec((1, d)),
                  _const_spec(w_in.shape),
                  pl.BlockSpec((tm, HEAD_DIM), lambda b, i: (i, 0)),
                  pl.BlockSpec((tm, HEAD_DIM), lambda b, i: (i, 0)),
                  state_spec,
                  _const_spec(dtab.shape)],
        out_specs=[tok(RET_W), tok(RET_W), tok(RET_W), tok(RET_W), tok(CONV_CH),
                   pl.BlockSpec((1, cpt, HEADS, CHUNK, CHUNK), lambda b, i: (b, i, 0, 0, 0)),
                   state_spec],
        out_shape=[jax.ShapeDtypeStruct((bsz, l, RET_W), BF16),
                   jax.ShapeDtypeStruct((bsz, l, RET_W), BF16),
                   jax.ShapeDtypeStruct((bsz, l, RET_W), BF16),
                   jax.ShapeDtypeStruct((bsz, l, RET_W), F32),
                   jax.ShapeDtypeStruct((bsz, l, CONV_CH), F32),
                   jax.ShapeDtypeStruct((bsz, n_chunks, HEADS, CHUNK, CHUNK), BF16),
                   jax.ShapeDtypeStruct((bsz, HEADS, CHUNK, CHUNK), F32)],
        scratch_shapes=[pltpu.VMEM((HEADS, CHUNK, CHUNK), F32)],
        compiler_params=_params(("arbitrary", "arbitrary")),
        name="even_in",
    )(x, mod, nw, w_in, cosf, sinf, s0, dtab)


def _even_mix_kernel(x_ref, q_ref, k_ref, v_ref, gs_ref, u_ref, up_ref, un_ref, sprev_ref, sb0_ref,
                     dt_ref, mod_ref, cw_ref, lnw_ref, lnb_ref, wo_ref,
                     o_ref, sbfin_ref, state_ref, uext_ref, mix_ref, *, tm):
    b = pl.program_id(0)
    i = pl.program_id(1)
    n_tiles = pl.num_programs(1)

    @pl.when(i == 0)
    def _():
        state_ref[...] = sb0_ref[0]

    for c in reversed(range(tm // CHUNK)):
        rows = slice(c * CHUNK, (c + 1) * CHUNK)
        for hd in range(HEADS):
            sl = slice(hd * HEAD_DIM, (hd + 1) * HEAD_DIM)
            qc = q_ref[0, rows, sl]
            kc = k_ref[0, rows, sl]
            vc = v_ref[0, rows, sl]
            s_b = state_ref[hd]
            p = (_dot_nt(qc, kc) * dt_ref[0, hd]).astype(BF16)
            y = _dot(p, vc)
            y = y + _dot(qc, sprev_ref[0, c, hd]) * dt_ref[3, hd]
            y = y + _dot(qc, s_b.astype(BF16)) * dt_ref[4, hd]
            y = y * lax.rsqrt(jnp.mean(y * y, axis=-1, keepdims=True) + EPS)
            mix_ref[rows, sl] = (gs_ref[0, rows, sl] * y).astype(BF16)
            wv = (vc.astype(F32) * dt_ref[2, hd]).astype(BF16)
            state_ref[hd] = s_b * dt_ref[6, hd] + _dot_tn(kc, wv)

    @pl.when(i == n_tiles - 1)
    def _():
        sbfin_ref[0] = state_ref[...]

    uext_ref[0:HALO] = jnp.where(i < n_tiles - 1, up_ref[0], 0.0)
    uext_ref[HALO:HALO + tm] = u_ref[0]
    uext_ref[HALO + tm:2 * HALO + tm] = jnp.where(i > 0, un_ref[0], 0.0)
    rb = 64
    for r in range(tm // rb):
        acc = None
        for sub in range(SUBLANES):
            part = None
            for t in range(sub, CONV_K, SUBLANES):
                start = r * rb + t - sub
                term = uext_ref[start:start + rb + SUBLANES] * cw_ref[t:t + 1]
                part = term if part is None else part + term
            shifted = part[sub + 1:sub + 1 + rb]
            acc = shifted if acc is None else acc + shifted
        cv = _silu(_layer_norm(acc, lnw_ref[...], lnb_ref[...]))
        mix_ref[r * rb:(r + 1) * rb, RET_W:RET_W + CONV_CH] = cv.astype(BF16)

    out = _dot(mix_ref[...], wo_ref[...])
    o_ref[0] = x_ref[0] + mod_ref[b * 6 + 2] * out


def _even_mix_call(x, q, k, v, gs, u, sprev, sb0, dtab, mod, conv_w, ln_w, ln_b, w_out, tm):
    bsz, l, d = x.shape
    n_tiles = l // tm
    cpt = tm // CHUNK
    hpt = tm // HALO
    n_halo = l // HALO
    rev = lambda i: n_tiles - 1 - i
    tok = lambda w: pl.BlockSpec((1, tm, w), lambda b, i: (b, rev(i), 0))
    state_spec = pl.BlockSpec((1, HEADS, CHUNK, CHUNK), lambda b, i: (b, 0, 0, 0))
    return pl.pallas_call(
        functools.partial(_even_mix_kernel, tm=tm),
        grid=(bsz, n_tiles),
        in_specs=[tok(d), tok(RET_W), tok(RET_W), tok(RET_W), tok(RET_W), tok(CONV_CH),
                  pl.BlockSpec((1, HALO, CONV_CH), lambda b, i: (b, jnp.maximum(rev(i) * hpt - 1, 0), 0)),
                  pl.BlockSpec((1, HALO, CONV_CH),
                               lambda b, i: (b, jnp.minimum((rev(i) + 1) * hpt, n_halo - 1), 0)),
                  pl.BlockSpec((1, cpt, HEADS, CHUNK, CHUNK), lambda b, i: (b, rev(i), 0, 0, 0)),
                  state_spec,
                  _const_spec(dtab.shape),
                  _const_spec(mod.shape),
                  _const_spec(conv_w.shape),
                  _const_spec((1, CONV_CH)),
                  _const_spec((1, CONV_CH)),
                  _const_spec(w_out.shape)],
        out_specs=[tok(d), state_spec],
        out_shape=[jax.ShapeDtypeStruct((bsz, l, d), F32),
                   jax.ShapeDtypeStruct((bsz, HEADS, CHUNK, CHUNK), F32)],
        scratch_shapes=[pltpu.VMEM((HEADS, CHUNK, CHUNK), F32),
                        pltpu.VMEM((tm + 2 * HALO, CONV_CH), F32),
                        pltpu.VMEM((tm, RET_W + CONV_CH), BF16)],
        compiler_params=_params(("arbitrary", "arbitrary")),
        name="even_mix",
    )(x, q, k, v, gs, u, u, u, sprev, sb0, dtab, mod, conv_w, ln_w, ln_b, w_out)


def _odd_kernel(x_ref, xp_ref, xn_ref, mod_ref, nw_ref, wi_ref, pw_ref, ps_ref, lnw_ref, lnb_ref,
                sgw_ref, sgb_ref, wo_ref, o_ref, h_ref, pcx_ref, mix_ref, *, tm, seq_len):
    b = pl.program_id(0)
    i = pl.program_id(1)
    n_tiles = pl.num_programs(1)
    nw = nw_ref[...]
    shift = mod_ref[b * 6 + 0]
    scale = mod_ref[b * 6 + 1]

    h_ref[0:HALO] = _rms_mod(xp_ref[0], nw, shift, scale).astype(BF16)
    h_ref[HALO:HALO + tm] = _rms_mod(x_ref[0], nw, shift, scale).astype(BF16)
    h_ref[HALO + tm:2 * HALO + tm] = _rms_mod(xn_ref[0], nw, shift, scale).astype(BF16)

    pcx_ref[...] = _dot(h_ref[...], wi_ref[:, 0:POOL_CH])
    pcx_ref[0:HALO] = jnp.where(i > 0, pcx_ref[0:HALO], 0.0)
    pcx_ref[HALO + tm:2 * HALO + tm] = jnp.where(i < n_tiles - 1, pcx_ref[HALO + tm:2 * HALO + tm], 0.0)
    pos = i * tm + lax.broadcasted_iota(jnp.int32, (tm, POOL_GC), 0)
    for gi, w in enumerate(POOL_WINDOWS):
        left = w // 2
        right = w - 1 - left
        sl = slice(gi * POOL_GC, (gi + 1) * POOL_GC)
        win = pcx_ref[HALO - left:HALO - left + tm, sl]
        for o in range(-left + 1, right + 1):
            win = win + pcx_ref[HALO + o:HALO + o + tm, sl]
        cnt = jnp.minimum(pos + right, seq_len - 1) - jnp.maximum(pos - left, 0) + 1
        m = win / cnt.astype(F32) - pcx_ref[HALO:HALO + tm, sl]
        po = _dot(m.astype(BF16), pw_ref[gi]) * ps_ref[:, sl]
        mix_ref[:, sl] = po.astype(BF16)

    pd = _dot(h_ref[HALO:HALO + tm], wi_ref[:, POOL_CH:ODD_IN])
    z = 0.5 * pd * (1.0 + lax.erf(pd * (2.0 ** -0.5)))
    zu = z[:, 0:SG_CH]
    zv = _layer_norm(z[:, SG_CH:2 * SG_CH], lnw_ref[...], lnb_ref[...]).astype(BF16)
    for c in range(tm // CHUNK):
        rows = slice(c * CHUNK, (c + 1) * CHUNK)
        for g in range(SG_GROUPS):
            sl = slice(g * SG_GC, (g + 1) * SG_GC)
            s = _dot(sgw_ref[g], zv[rows, sl]) + sgb_ref[g]
            mix_ref[rows, POOL_CH + g * SG_GC:POOL_CH + (g + 1) * SG_GC] = (zu[rows, sl] * s).astype(BF16)

    out = _dot(mix_ref[...], wo_ref[...])
    o_ref[0] = x_ref[0] + mod_ref[b * 6 + 2] * out


def _odd_call(x, mod, nw, w_in, pool_w, pool_scale, ln_w, ln_b, sg_w, sg_b_full, w_out, tm):
    bsz, l, d = x.shape
    hpt = tm // HALO
    n_halo = l // HALO
    tok = pl.BlockSpec((1, tm, d), lambda b, i: (b, i, 0))
    return pl.pallas_call(
        functools.partial(_odd_kernel, tm=tm, seq_len=l),
        grid=(bsz, l // tm),
        in_specs=[tok,
                  pl.BlockSpec((1, HALO, d), lambda b, i: (b, jnp.maximum(i * hpt - 1, 0), 0)),
                  pl.BlockSpec((1, HALO, d), lambda b, i: (b, jnp.minimum((i + 1) * hpt, n_halo - 1), 0)),
                  _const_spec(mod.shape),
                  _const_spec((1, d)),
                  _const_spec(w_in.shape),
                  _const_spec(pool_w.shape),
                  _const_spec((1, POOL_CH)),
                  _const_spec((1, SG_CH)),
                  _const_spec((1, SG_CH)),
                  _const_spec(sg_w.shape),
                  _const_spec(sg_b_full.shape),
                  _const_spec(w_out.shape)],
        out_specs=tok,
        out_shape=jax.ShapeDtypeStruct((bsz, l, d), F32),
        scratch_shapes=[pltpu.VMEM((tm + 2 * HALO, d), BF16),
                        pltpu.VMEM((tm + 2 * HALO, POOL_CH), F32),
                        pltpu.VMEM((tm, POOL_CH + SG_CH), BF16)],
        compiler_params=_params(("arbitrary", "arbitrary")),
        name="odd_mix",
    )(x, x, x, mod, nw, w_in, pool_w, pool_scale, ln_w, ln_b, sg_w, sg_b_full, w_out)


def _ffn_kernel(x_ref, mod_ref, nw_ref, wg_ref, wu_ref, wd_ref, fnw_ref, o_ref, *, final_norm):
    b = pl.program_id(0)
    x32 = x_ref[0]
    h = _rms_mod(x32, nw_ref[...], mod_ref[b * 6 + 3], mod_ref[b * 6 + 4]).astype(BF16)
    act = (_silu(_dot(h, wg_ref[...])) * _dot(h, wu_ref[...])).astype(BF16)
    y = x32 + mod_ref[b * 6 + 5] * _dot(act, wd_ref[...])
    if final_norm:
        y = y * lax.rsqrt(jnp.mean(y * y, axis=-1, keepdims=True) + EPS) * fnw_ref[...]
    o_ref[0] = y


def _ffn_call(x, mod, nw, wg, wu, wd, fnw, tm, final_norm):
    bsz, l, d = x.shape
    tok = pl.BlockSpec((1, tm, d), lambda b, i: (b, i, 0))
    return pl.pallas_call(
        functools.partial(_ffn_kernel, final_norm=final_norm),
        grid=(bsz, l // tm),
        in_specs=[tok,
                  _const_spec(mod.shape),
                  _const_spec((1, d)),
                  _const_spec(wg.shape),
                  _const_spec(wu.shape),
                  _const_spec(wd.shape),
                  _const_spec((1, d))],
        out_specs=tok,
        out_shape=jax.ShapeDtypeStruct((bsz, l, d), F32),
        compiler_params=_params(("arbitrary", "arbitrary")),
        name="ffn",
    )(x, mod, nw, wg, wu, wd, fnw)


def _rope_tables(p_seq, p_row, p_col):
    parts = []
    for p, n in zip((p_seq, p_row, p_col), ROPE_PAIRS):
        freq = ROPE_BASE ** (-jnp.arange(n, dtype=F32) / n)
        parts.append(p[:, None] * freq[None, :])
    ang = jnp.concatenate(parts, axis=-1)
    cos, sin = jnp.cos(ang), jnp.sin(ang)
    return jnp.concatenate([cos, cos], axis=-1), jnp.concatenate([-sin, sin], axis=-1)


def _tile(l):
    return min(l, 512)


def kernel(x, c, ctx, c_ctx, ada_w, ada_b, norm_w, even_w_in, even_w_out, ret_decay_logit, conv_dw_w, conv_ln_w, conv_ln_b, odd_w_in, odd_w_out, pool_w, pool_scale, sg_ln_w, sg_ln_b, sg_w, sg_b, ffn_w_gate, ffn_w_up, ffn_w_down, final_norm_w):
    bsz, l, d = x.shape
    lc = ctx.shape[1]
    assert d == D_MODEL and l % 256 == 0 and lc % CHUNK == 0 and bsz + 1 <= MOD_ROWS
    rows = l // GRID_W
    grid_r = jnp.broadcast_to(jnp.arange(rows, dtype=F32)[:, None], (rows, GRID_W)).reshape(-1)
    grid_c = jnp.broadcast_to(jnp.arange(GRID_W, dtype=F32)[None, :], (rows, GRID_W)).reshape(-1)
    cos_x, sin_x = _rope_tables(jnp.full((l,), lc, F32), grid_r, grid_c)
    zeros_c = jnp.zeros((lc,), F32)
    cos_c, sin_c = _rope_tables(jnp.arange(lc, dtype=F32), zeros_c, zeros_c)

    cvec = jnp.concatenate([c, c_ctx[None, :], jnp.zeros((MOD_ROWS - bsz - 1, d), F32)], axis=0)
    mod_all = _mod_call(cvec, ada_w, ada_b)
    dtabs = _decay_call(ret_decay_logit)
    zero_state = jnp.zeros((bsz, HEADS, CHUNK, CHUNK), F32)
    fnw = final_norm_w.reshape(1, d)
    tx, tc = _tile(l), _tile(lc)
    ffn_tx, ffn_tc = min(l, 512), min(lc, 512)

    for i in range(DEPTH):
        j = i // 2
        even = i % 2 == 0
        ctx_after = any(m % 2 == 0 for m in range(i + 1, DEPTH))
        mod_x = mod_all[i, :bsz].reshape(bsz * 6, 1, d)
        mod_c = jnp.broadcast_to(mod_all[i, bsz].reshape(1, 6, 1, d), (bsz, 6, 1, d)).reshape(bsz * 6, 1, d)
        nw1 = norm_w[i, 0].reshape(1, d)
        nw2 = norm_w[i, 1].reshape(1, d)
        wg = ffn_w_gate[i].astype(BF16)
        wu = ffn_w_up[i].astype(BF16)
        wd = ffn_w_down[i].astype(BF16)
        if even:
            w_in = even_w_in[j].astype(BF16)
            w_out = even_w_out[j].astype(BF16)
            conv_w = conv_dw_w[j]
            lnw = conv_ln_w[j].reshape(1, CONV_CH)
            lnb = conv_ln_b[j].reshape(1, CONV_CH)
            dtab = dtabs[j]
            qc, kc, vc, gc, uc, spc, s_f = _even_in_call(ctx, mod_c, nw1, w_in, cos_c, sin_c, zero_state, dtab, tc)
            yc, s_b = _even_mix_call(ctx, qc, kc, vc, gc, uc, spc, zero_state, dtab, mod_c,
                                     conv_w, lnw, lnb, w_out, tc)
            qx, kx, vx, gx, ux, spx, _ = _even_in_call(x, mod_x, nw1, w_in, cos_x, sin_x, s_f, dtab, tx)
            x, _ = _even_mix_call(x, qx, kx, vx, gx, ux, spx, s_b, dtab, mod_x, conv_w, lnw, lnb, w_out, tx)
            if ctx_after:
                ctx = yc
        else:
            w_in = odd_w_in[j].astype(BF16)
            w_out = odd_w_out[j].astype(BF16)
            args = (nw1, w_in, pool_w[j].astype(BF16), pool_scale[j].reshape(1, POOL_CH),
                    sg_ln_w[j].reshape(1, SG_CH), sg_ln_b[j].reshape(1, SG_CH), sg_w[j].astype(BF16),
                    jnp.broadcast_to(sg_b[j][:, :, None], (SG_GROUPS, CHUNK, SG_GC)), w_out)
            if ctx_after:
                ctx = _odd_call(ctx, mod_c, *args, tc)
            x = _odd_call(x, mod_x, *args, tx)
        if ctx_after:
            ctx = _ffn_call(ctx, mod_c, nw2, wg, wu, wd, fnw, ffn_tc, False)
        x = _ffn_call(x, mod_x, nw2, wg, wu, wd, fnw, ffn_tx, i == DEPTH - 1)
    return x
```

```python
import functools

import jax
import jax.numpy as jnp
from jax import lax
from jax.experimental import pallas as pl
from jax.experimental.pallas import tpu as pltpu

F32 = jnp.float32
BF16 = jnp.bfloat16

D_MODEL = 1024
DEPTH = 4
GRID_W = 64
EPS = 1e-6
HEADS = 4
HEAD_DIM = 128
RET_W = HEADS * HEAD_DIM
CHUNK = 128
ROPE_BASE = 10000.0
ROPE_PAIRS = (HEAD_DIM // 8, 3 * HEAD_DIM // 16, 3 * HEAD_DIM // 16)
CONV_CH = 512
CONV_K = 31
CONV_PAD = CONV_K // 2
EVEN_IN = 4 * RET_W + 2 * CONV_CH
POOL_CH = 512
POOL_WINDOWS = (2, 4, 8, 16)
POOL_GC = 128
SG_CH = 512
SG_GROUPS = 4
SG_GC = 128
ODD_IN = POOL_CH + 2 * SG_CH
D_FF = 2816
SUBLANES = 8
LANES = 128
FF_BLOCK = 256
OUT_BLOCK = 256
CONV_ROWS = 64
HALO = 16
N_DECAY_TABLES = 7
MOD_ROWS = 8
VMEM_LIMIT = 56 * 1024 * 1024


def _dot(a, b):
    return jnp.dot(a, b, preferred_element_type=F32)


def _dot_nt(a, b):
    return lax.dot_general(a, b, (((1,), (1,)), ((), ())), preferred_element_type=F32)


def _dot_tn(a, b):
    return lax.dot_general(a, b, (((0,), (0,)), ((), ())), preferred_element_type=F32)


def _silu(x):
    return x * jax.nn.sigmoid(x)


def _rms_mod(x32, nw, shift, scale):
    y = x32 * lax.rsqrt(jnp.mean(x32 * x32, axis=-1, keepdims=True) + EPS) * nw
    return y * (1.0 + scale) + shift


def _layer_norm(x32, w, b):
    mu = jnp.mean(x32, axis=-1, keepdims=True)
    xc = x32 - mu
    return xc * lax.rsqrt(jnp.mean(xc * xc, axis=-1, keepdims=True) + EPS) * w + b


def _const_spec(shape):
    return pl.BlockSpec(shape, lambda *_: (0,) * len(shape), pipeline_mode=pl.Buffered(1))


def _params(sem):
    return pltpu.CompilerParams(dimension_semantics=sem, vmem_limit_bytes=VMEM_LIMIT)


def _mod_kernel(c_ref, w_ref, b_ref, o_ref):
    s = _silu(c_ref[...]).astype(BF16)
    o_ref[0] = _dot(s, w_ref[0].astype(BF16)) + b_ref[0]


def _mod_call(cvec, ada_w, ada_b):
    depth, d, n = ada_w.shape
    tn = 1024
    return pl.pallas_call(
        _mod_kernel,
        grid=(depth, n // tn),
        in_specs=[_const_spec((MOD_ROWS, d)),
                  pl.BlockSpec((1, d, tn), lambda i, j: (i, 0, j)),
                  pl.BlockSpec((1, 1, tn), lambda i, j: (i, 0, j))],
        out_specs=pl.BlockSpec((1, MOD_ROWS, tn), lambda i, j: (i, 0, j)),
        out_shape=jax.ShapeDtypeStruct((depth, MOD_ROWS, n), F32),
        compiler_params=_params(("arbitrary", "arbitrary")),
        name="adaln_mod",
    )(cvec, ada_w, ada_b.reshape(depth, 1, n))


def _log_sigmoid(x):
    return -(jnp.maximum(-x, 0.0) + jnp.log1p(jnp.exp(-jnp.abs(x))))


def _decay_kernel(logit_ref, o_ref):
    j = pl.program_id(0)
    row = lax.broadcasted_iota(jnp.int32, (CHUNK, CHUNK), 0).astype(F32)
    col = lax.broadcasted_iota(jnp.int32, (CHUNK, CHUNK), 1).astype(F32)
    diff = row - col
    lower = diff >= 0.0
    for hd in range(HEADS):
        lgf = _log_sigmoid(jnp.full((CHUNK, CHUNK), logit_ref[(j * 2 + 0) * HEADS + hd], F32))
        lgb = _log_sigmoid(jnp.full((CHUNK, CHUNK), logit_ref[(j * 2 + 1) * HEADS + hd], F32))
        o_ref[0, 0, hd] = jnp.where(lower, jnp.exp(lgf * jnp.where(lower, diff, 0.0)),
                                    jnp.exp(lgb * jnp.where(lower, 0.0, -diff - 1.0)))
        o_ref[0, 1, hd] = jnp.exp(lgf * (CHUNK - 1.0 - row))
        o_ref[0, 2, hd] = jnp.exp(lgb * row)
        o_ref[0, 3, hd] = jnp.exp(lgf * (row + 1.0))
        o_ref[0, 4, hd] = jnp.exp(lgb * (CHUNK - 1.0 - row))
        o_ref[0, 5, hd] = jnp.exp(lgf * float(CHUNK))
        o_ref[0, 6, hd] = jnp.exp(lgb * float(CHUNK))


def _decay_call(ret_decay_logit):
    n_even = ret_decay_logit.shape[0]
    return pl.pallas_call(
        _decay_kernel,
        grid=(n_even,),
        in_specs=[pl.BlockSpec(memory_space=pltpu.SMEM)],
        out_specs=pl.BlockSpec((1, N_DECAY_TABLES, HEADS, CHUNK, CHUNK), lambda j: (j, 0, 0, 0, 0)),
        out_shape=jax.ShapeDtypeStruct((n_even, N_DECAY_TABLES, HEADS, CHUNK, CHUNK), F32),
        compiler_params=_params(("arbitrary",)),
        name="ret_decay",
    )(ret_decay_logit.reshape(-1).astype(F32))


def _even_in_kernel(x_ref, mod_ref, nw_ref, w_ref, cos_ref, sin_ref, s0_ref, dt_ref,
                    q_ref, k_ref, v_ref, gs_ref, u_ref, sprev_ref, sfin_ref, state_ref, *, tm):
    b = pl.program_id(0)
    i = pl.program_id(1)

    @pl.when(i == 0)
    def _():
        state_ref[...] = s0_ref[0]

    h = _rms_mod(x_ref[0], nw_ref[...], mod_ref[b * 6 + 0], mod_ref[b * 6 + 1]).astype(BF16)
    cosf = cos_ref[...]
    sinf = sin_ref[...]

    def rope(t):
        return t * cosf + pltpu.roll(t, HEAD_DIM // 2, 1) * sinf

    pq = _dot(h, w_ref[:, 0:RET_W])
    pk = _dot(h, w_ref[:, RET_W:2 * RET_W])
    pv = _dot(h, w_ref[:, 2 * RET_W:3 * RET_W])
    scale = HEAD_DIM ** -0.5
    for hd in range(HEADS):
        sl = slice(hd * HEAD_DIM, (hd + 1) * HEAD_DIM)
        q_ref[0, :, sl] = (rope(pq[:, sl]) * scale).astype(BF16)
        kr = rope(pk[:, sl]).astype(BF16)
        k_ref[0, :, sl] = kr
        vh = pv[:, sl]
        v_ref[0, :, sl] = vh.astype(BF16)
        zeta = dt_ref[1, hd]
        cdec = dt_ref[5, hd]
        for c in range(tm // CHUNK):
            rows = slice(c * CHUNK, (c + 1) * CHUNK)
            s_cur = state_ref[hd]
            sprev_ref[0, c, hd] = s_cur.astype(BF16)
            wv = (vh[rows] * zeta).astype(BF16)
            state_ref[hd] = s_cur * cdec + _dot_tn(kr[rows], wv)

    pg = _dot(h, w_ref[:, 3 * RET_W:4 * RET_W])
    gs_ref[0] = _silu(pg)
    pa = _dot(h, w_ref[:, 4 * RET_W:4 * RET_W + CONV_CH])
    pb = _dot(h, w_ref[:, 4 * RET_W + CONV_CH:EVEN_IN])
    u_ref[0] = pa * jax.nn.sigmoid(pb)

    @pl.when(i == pl.num_programs(1) - 1)
    def _():
        sfin_ref[0] = state_ref[...]


def _even_in_call(x, mod, nw, w_in, cosf, sinf, s0, dtab, tm):
    bsz, l, d = x.shape
    n_chunks = l // CHUNK
    cpt = tm // CHUNK
    tok = lambda w: pl.BlockSpec((1, tm, w), lambda b, i: (b, i, 0))
    state_spec = pl.BlockSpec((1, HEADS, CHUNK, CHUNK), lambda b, i: (b, 0, 0, 0))
    return pl.pallas_call(
        functools.partial(_even_in_kernel, tm=tm),
        grid=(bsz, l // tm),
        in_specs=[tok(d),
                  _const_spec(mod.shape),
                  _const_spec((1, d)),
                  _const_spec(w_in.shape),
                  pl.BlockSpec((tm, HEAD_DIM), lambda b, i: (i, 0)),
                  pl.BlockSpec((tm, HEAD_DIM), lambda b, i: (i, 0)),
                  state_spec,
                  _const_spec(dtab.shape)],
        out_specs=[tok(RET_W), tok(RET_W), tok(RET_W), tok(RET_W), tok(CONV_CH),
                   pl.BlockSpec((1, cpt, HEADS, CHUNK, CHUNK), lambda b, i: (b, i, 0, 0, 0)),
                   state_spec],
        out_shape=[jax.ShapeDtypeStruct((bsz, l, RET_W), BF16),
                   jax.ShapeDtypeStruct((bsz, l, RET_W), BF16),
                   jax.ShapeDtypeStruct((bsz, l, RET_W), BF16),
                   jax.ShapeDtypeStruct((bsz, l, RET_W), F32),
                   jax.ShapeDtypeStruct((bsz, l, CONV_CH), F32),
                   jax.ShapeDtypeStruct((bsz, n_chunks, HEADS, CHUNK, CHUNK), BF16),
                   jax.ShapeDtypeStruct((bsz, HEADS, CHUNK, CHUNK), F32)],
        scratch_shapes=[pltpu.VMEM((HEADS, CHUNK, CHUNK), F32)],
        compiler_params=_params(("arbitrary", "arbitrary")),
        name="even_in",
    )(x, mod, nw, w_in, cosf, sinf, s0, dtab)


def _even_mix_pieces(b, i, n_tiles, x_ref, q_ref, k_ref, v_ref, gs_ref, u_ref, up_ref, un_ref, sprev_ref,
                     dt_ref, mod_ref, cw_ref, lnw_ref, lnb_ref, wo_ref, state_ref, uext_ref, mix_ref, xo_ref, tm):
    def retention(c, hd):
        rows = slice(c * CHUNK, (c + 1) * CHUNK)
        sl = slice(hd * HEAD_DIM, (hd + 1) * HEAD_DIM)
        qc = q_ref[0, rows, sl]
        kc = k_ref[0, rows, sl]
        vc = v_ref[0, rows, sl]
        s_b = state_ref[hd]
        p = (_dot_nt(qc, kc) * dt_ref[0, hd]).astype(BF16)
        y = _dot(p, vc)
        y = y + _dot(qc, sprev_ref[0, c, hd]) * dt_ref[3, hd]
        y = y + _dot(qc, s_b.astype(BF16)) * dt_ref[4, hd]
        y = y * lax.rsqrt(jnp.mean(y * y, axis=-1, keepdims=True) + EPS)
        mix_ref[rows, sl] = (gs_ref[0, rows, sl] * y).astype(BF16)
        wv = (vc.astype(F32) * dt_ref[2, hd]).astype(BF16)
        state_ref[hd] = s_b * dt_ref[6, hd] + _dot_tn(kc, wv)

    def halo():
        uext_ref[0:HALO] = jnp.where(i < n_tiles - 1, up_ref[0], 0.0)
        uext_ref[HALO:HALO + tm] = u_ref[0]
        uext_ref[HALO + tm:2 * HALO + tm] = jnp.where(i > 0, un_ref[0], 0.0)

    def conv(r):
        blocks = []
        for cb in range(CONV_CH // LANES):
            cols = slice(cb * LANES, (cb + 1) * LANES)
            acc = None
            for sub in range(SUBLANES):
                part = None
                for t in range(sub, CONV_K, SUBLANES):
                    start = r * CONV_ROWS + t - sub
                    term = uext_ref[start:start + CONV_ROWS + SUBLANES, cols] * cw_ref[t:t + 1, cols]
                    part = term if part is None else part + term
                shifted = part[sub + 1:sub + 1 + CONV_ROWS]
                acc = shifted if acc is None else acc + shifted
            blocks.append(acc)
        cv = _silu(_layer_norm(jnp.concatenate(blocks, axis=-1), lnw_ref[...], lnb_ref[...]))
        mix_ref[r * CONV_ROWS:(r + 1) * CONV_ROWS, RET_W:RET_W + CONV_CH] = cv.astype(BF16)

    def out_proj(n):
        cols = slice(n * OUT_BLOCK, (n + 1) * OUT_BLOCK)
        out = _dot(mix_ref[...], wo_ref[:, cols])
        xo_ref[:, cols] = x_ref[0, :, cols] + mod_ref[b * 6 + 2][:, cols] * out

    pieces = [halo]
    convs = [functools.partial(conv, r) for r in range(tm // CONV_ROWS)]
    rets = [functools.partial(retention, c, hd) for c in reversed(range(tm // CHUNK)) for hd in range(HEADS)]
    per = len(rets) // len(convs)
    for r, cv in enumerate(convs):
        pieces += rets[r * per:(r + 1) * per] + [cv]
    pieces += rets[len(convs) * per:]
    pieces += [functools.partial(out_proj, n) for n in range(D_MODEL // OUT_BLOCK)]
    return pieces


def _ffn_pieces(xs_ref, o_ref, b, mod_ref, nw_ref, wg_ref, wu_ref, wd_ref, fnw_ref, h_ref, act_ref, final_norm):
    def prologue():
        x32 = xs_ref[...]
        o_ref[...] = x32
        h_ref[...] = _rms_mod(x32, nw_ref[...], mod_ref[b * 6 + 3], mod_ref[b * 6 + 4]).astype(BF16)

    def gate_up(j):
        cols = slice(j * FF_BLOCK, (j + 1) * FF_BLOCK)
        h = h_ref[...]
        act_ref[:, cols] = (_silu(_dot(h, wg_ref[:, cols])) * _dot(h, wu_ref[:, cols])).astype(BF16)

    def down(n):
        cols = slice(n * OUT_BLOCK, (n + 1) * OUT_BLOCK)
        o_ref[:, cols] = o_ref[:, cols] + mod_ref[b * 6 + 5][:, cols] * _dot(act_ref[...], wd_ref[:, cols])

    def final():
        y = o_ref[...]
        o_ref[...] = y * lax.rsqrt(jnp.mean(y * y, axis=-1, keepdims=True) + EPS) * fnw_ref[...]

    pieces = [prologue]
    pieces += [functools.partial(gate_up, j) for j in range(D_FF // FF_BLOCK)]
    pieces += [functools.partial(down, n) for n in range(D_MODEL // OUT_BLOCK)]
    if final_norm:
        pieces.append(final)
    return pieces


def _even_mix_kernel(x_ref, q_ref, k_ref, v_ref, gs_ref, u_ref, up_ref, un_ref, sprev_ref, sb0_ref,
                     dt_ref, mod_ref, cw_ref, lnw_ref, lnb_ref, wo_ref,
                     o_ref, sbfin_ref, state_ref, uext_ref, mix_ref, *, tm):
    b = pl.program_id(0)
    i = pl.program_id(1)
    n_tiles = pl.num_programs(1)

    @pl.when(i == 0)
    def _():
        state_ref[...] = sb0_ref[0]

    for piece in _even_mix_pieces(b, i, n_tiles, x_ref, q_ref, k_ref, v_ref, gs_ref, u_ref, up_ref, un_ref,
                                  sprev_ref, dt_ref, mod_ref, cw_ref, lnw_ref, lnb_ref, wo_ref,
                                  state_ref, uext_ref, mix_ref, o_ref.at[0], tm):
        piece()

    @pl.when(i == n_tiles - 1)
    def _():
        sbfin_ref[0] = state_ref[...]


def _even_mix_call(x, q, k, v, gs, u, sprev, sb0, dtab, mod, conv_w, ln_w, ln_b, w_out, tm):
    bsz, l, d = x.shape
    n_tiles = l // tm
    cpt = tm // CHUNK
    hpt = tm // HALO
    n_halo = l // HALO
    rev = lambda i: n_tiles - 1 - i
    tok = lambda w: pl.BlockSpec((1, tm, w), lambda b, i: (b, rev(i), 0))
    state_spec = pl.BlockSpec((1, HEADS, CHUNK, CHUNK), lambda b, i: (b, 0, 0, 0))
    return pl.pallas_call(
        functools.partial(_even_mix_kernel, tm=tm),
        grid=(bsz, n_tiles),
        in_specs=[tok(d), tok(RET_W), tok(RET_W), tok(RET_W), tok(RET_W), tok(CONV_CH),
                  pl.BlockSpec((1, HALO, CONV_CH), lambda b, i: (b, jnp.maximum(rev(i) * hpt - 1, 0), 0)),
                  pl.BlockSpec((1, HALO, CONV_CH),
                               lambda b, i: (b, jnp.minimum((rev(i) + 1) * hpt, n_halo - 1), 0)),
                  pl.BlockSpec((1, cpt, HEADS, CHUNK, CHUNK), lambda b, i: (b, rev(i), 0, 0, 0)),
                  state_spec,
                  _const_spec(dtab.shape),
                  _const_spec(mod.shape),
                  _const_spec(conv_w.shape),
                  _const_spec((1, CONV_CH)),
                  _const_spec((1, CONV_CH)),
                  _const_spec(w_out.shape)],
        out_specs=[tok(d), state_spec],
        out_shape=[jax.ShapeDtypeStruct((bsz, l, d), F32),
                   jax.ShapeDtypeStruct((bsz, HEADS, CHUNK, CHUNK), F32)],
        scratch_shapes=[pltpu.VMEM((HEADS, CHUNK, CHUNK), F32),
                        pltpu.VMEM((tm + 2 * HALO, CONV_CH), F32),
                        pltpu.VMEM((tm, RET_W + CONV_CH), BF16)],
        compiler_params=_params(("arbitrary", "arbitrary")),
        name="even_mix",
    )(x, q, k, v, gs, u, u, u, sprev, sb0, dtab, mod, conv_w, ln_w, ln_b, w_out)


def _odd_kernel(x_ref, xp_ref, xn_ref, mod_ref, nw_ref, wi_ref, pw_ref, ps_ref, lnw_ref, lnb_ref,
                sgw_ref, sgb_ref, wo_ref, o_ref, h_ref, pcx_ref, mix_ref, *, tm, seq_len):
    b = pl.program_id(0)
    i = pl.program_id(1)
    n_tiles = pl.num_programs(1)
    nw = nw_ref[...]
    shift = mod_ref[b * 6 + 0]
    scale = mod_ref[b * 6 + 1]

    h_ref[0:HALO] = _rms_mod(xp_ref[0], nw, shift, scale).astype(BF16)
    h_ref[HALO:HALO + tm] = _rms_mod(x_ref[0], nw, shift, scale).astype(BF16)
    h_ref[HALO + tm:2 * HALO + tm] = _rms_mod(xn_ref[0], nw, shift, scale).astype(BF16)

    pcx_ref[...] = _dot(h_ref[...], wi_ref[:, 0:POOL_CH])
    pcx_ref[0:HALO] = jnp.where(i > 0, pcx_ref[0:HALO], 0.0)
    pcx_ref[HALO + tm:2 * HALO + tm] = jnp.where(i < n_tiles - 1, pcx_ref[HALO + tm:2 * HALO + tm], 0.0)
    pos = i * tm + lax.broadcasted_iota(jnp.int32, (tm, POOL_GC), 0)
    for gi, w in enumerate(POOL_WINDOWS):
        left = w // 2
        right = w - 1 - left
        sl = slice(gi * POOL_GC, (gi + 1) * POOL_GC)
        win = pcx_ref[HALO - left:HALO - left + tm, sl]
        for o in range(-left + 1, right + 1):
            win = win + pcx_ref[HALO + o:HALO + o + tm, sl]
        cnt = jnp.minimum(pos + right, seq_len - 1) - jnp.maximum(pos - left, 0) + 1
        m = win / cnt.astype(F32) - pcx_ref[HALO:HALO + tm, sl]
        po = _dot(m.astype(BF16), pw_ref[gi]) * ps_ref[:, sl]
        mix_ref[:, sl] = po.astype(BF16)

    pd = _dot(h_ref[HALO:HALO + tm], wi_ref[:, POOL_CH:ODD_IN])
    z = 0.5 * pd * (1.0 + lax.erf(pd * (2.0 ** -0.5)))
    zu = z[:, 0:SG_CH]
    zv = _layer_norm(z[:, SG_CH:2 * SG_CH], lnw_ref[...], lnb_ref[...]).astype(BF16)
    for c in range(tm // CHUNK):
        rows = slice(c * CHUNK, (c + 1) * CHUNK)
        for g in range(SG_GROUPS):
            sl = slice(g * SG_GC, (g + 1) * SG_GC)
            s = _dot(sgw_ref[g], zv[rows, sl]) + sgb_ref[g]
            mix_ref[rows, POOL_CH + g * SG_GC:POOL_CH + (g + 1) * SG_GC] = (zu[rows, sl] * s).astype(BF16)

    out = _dot(mix_ref[...], wo_ref[...])
    o_ref[0] = x_ref[0] + mod_ref[b * 6 + 2] * out


def _odd_call(x, mod, nw, w_in, pool_w, pool_scale, ln_w, ln_b, sg_w, sg_b_full, w_out, tm):
    bsz, l, d = x.shape
    hpt = tm // HALO
    n_halo = l // HALO
    tok = pl.BlockSpec((1, tm, d), lambda b, i: (b, i, 0))
    return pl.pallas_call(
        functools.partial(_odd_kernel, tm=tm, seq_len=l),
        grid=(bsz, l // tm),
        in_specs=[tok,
                  pl.BlockSpec((1, HALO, d), lambda b, i: (b, jnp.maximum(i * hpt - 1, 0), 0)),
                  pl.BlockSpec((1, HALO, d), lambda b, i: (b, jnp.minimum((i + 1) * hpt, n_halo - 1), 0)),
                  _const_spec(mod.shape),
                  _const_spec((1, d)),
                  _const_spec(w_in.shape),
                  _const_spec(pool_w.shape),
                  _const_spec((1, POOL_CH)),
                  _const_spec((1, SG_CH)),
                  _const_spec((1, SG_CH)),
                  _const_spec(sg_w.shape),
                  _const_spec(sg_b_full.shape),
                  _const_spec(w_out.shape)],
        out_specs=tok,
        out_shape=jax.ShapeDtypeStruct((bsz, l, d), F32),
        scratch_shapes=[pltpu.VMEM((tm + 2 * HALO, d), BF16),
                        pltpu.VMEM((tm + 2 * HALO, POOL_CH), F32),
                        pltpu.VMEM((tm, POOL_CH + SG_CH), BF16)],
        compiler_params=_params(("arbitrary", "arbitrary")),
        name="odd_mix",
    )(x, x, x, mod, nw, w_in, pool_w, pool_scale, ln_w, ln_b, sg_w, sg_b_full, w_out)


def _ffn_kernel(x_ref, mod_ref, nw_ref, wg_ref, wu_ref, wd_ref, fnw_ref, *rest, n_cast, final_norm):
    cast_in = rest[:n_cast]
    o_ref = rest[n_cast]
    cast_out = rest[n_cast + 1:2 * n_cast + 1]
    h_ref, act_ref = rest[2 * n_cast + 1:]
    for src, dst in zip(cast_in, cast_out):
        dst[...] = src[...].astype(BF16)
    for piece in _ffn_pieces(x_ref.at[0], o_ref.at[0], pl.program_id(0), mod_ref, nw_ref, wg_ref, wu_ref, wd_ref,
                             fnw_ref, h_ref, act_ref, final_norm):
        piece()


def _ffn_call(x, mod, nw, wg, wu, wd, fnw, tm, final_norm, cast=()):
    bsz, l, d = x.shape
    n_tiles = l // tm
    n_steps = bsz * n_tiles
    tok = pl.BlockSpec((1, tm, d), lambda b, i: (b, i, 0))

    def band_specs(stack, layer):
        _, rows, cols = stack.shape
        hold = next(h for h in range(1, n_steps + 1)
                    if n_steps % h == 0 and rows % (n_steps // h) == 0 and (rows * h // n_steps) % 16 == 0)
        band = rows * hold // n_steps
        return (pl.BlockSpec((None, band, cols), lambda b, i: (layer, (b * n_tiles + i) // hold, 0)),
                pl.BlockSpec((band, cols), lambda b, i: ((b * n_tiles + i) // hold, 0)))

    specs = [band_specs(stack, layer) for stack, layer in cast]
    outs = pl.pallas_call(
        functools.partial(_ffn_kernel, n_cast=len(cast), final_norm=final_norm),
        grid=(bsz, n_tiles),
        in_specs=[tok,
                  _const_spec(mod.shape),
                  _const_spec((1, d)),
                  _const_spec(wg.shape),
                  _const_spec(wu.shape),
                  _const_spec(wd.shape),
                  _const_spec((1, d))] + [s[0] for s in specs],
        out_specs=[tok] + [s[1] for s in specs],
        out_shape=[jax.ShapeDtypeStruct((bsz, l, d), F32)]
        + [jax.ShapeDtypeStruct(stack.shape[1:], BF16) for stack, _ in cast],
        scratch_shapes=[pltpu.VMEM((tm, d), BF16), pltpu.VMEM((tm, D_FF), BF16)],
        compiler_params=_params(("arbitrary", "arbitrary")),
        name="ffn",
    )(x, mod, nw, wg, wu, wd, fnw, *[stack for stack, _ in cast])
    return outs[0], tuple(outs[1:])


def _rope_tables(p_seq, p_row, p_col):
    parts = []
    for p, n in zip((p_seq, p_row, p_col), ROPE_PAIRS):
        freq = ROPE_BASE ** (-jnp.arange(n, dtype=F32) / n)
        parts.append(p[:, None] * freq[None, :])
    ang = jnp.concatenate(parts, axis=-1)
    cos, sin = jnp.cos(ang), jnp.sin(ang)
    return jnp.concatenate([cos, cos], axis=-1), jnp.concatenate([-sin, sin], axis=-1)


def _tile(l):
    return min(l, 512)


def kernel(x, c, ctx, c_ctx, ada_w, ada_b, norm_w, even_w_in, even_w_out, ret_decay_logit, conv_dw_w, conv_ln_w, conv_ln_b, odd_w_in, odd_w_out, pool_w, pool_scale, sg_ln_w, sg_ln_b, sg_w, sg_b, ffn_w_gate, ffn_w_up, ffn_w_down, final_norm_w):
    bsz, l, d = x.shape
    lc = ctx.shape[1]
    assert d == D_MODEL and l % 256 == 0 and lc % CHUNK == 0 and bsz + 1 <= MOD_ROWS
    rows = l // GRID_W
    grid_r = jnp.broadcast_to(jnp.arange(rows, dtype=F32)[:, None], (rows, GRID_W)).reshape(-1)
    grid_c = jnp.broadcast_to(jnp.arange(GRID_W, dtype=F32)[None, :], (rows, GRID_W)).reshape(-1)
    cos_x, sin_x = _rope_tables(jnp.full((l,), lc, F32), grid_r, grid_c)
    zeros_c = jnp.zeros((lc,), F32)
    cos_c, sin_c = _rope_tables(jnp.arange(lc, dtype=F32), zeros_c, zeros_c)

    cvec = jnp.concatenate([c, c_ctx[None, :], jnp.zeros((MOD_ROWS - bsz - 1, d), F32)], axis=0)
    mod_all = _mod_call(cvec, ada_w, ada_b)
    dtabs = _decay_call(ret_decay_logit)
    zero_state = jnp.zeros((bsz, HEADS, CHUNK, CHUNK), F32)
    fnw = final_norm_w.reshape(1, d)
    tx, tc = _tile(l), _tile(lc)

    def layer_matrices(i):
        mix_in, mix_out = (even_w_in, even_w_out) if i % 2 == 0 else (odd_w_in, odd_w_out)
        return (mix_in, i // 2), (mix_out, i // 2), (ffn_w_gate, i), (ffn_w_up, i), (ffn_w_down, i)

    bf16_mats = tuple(stack[layer].astype(BF16) for stack, layer in layer_matrices(0))
    for i in range(DEPTH):
        j = i // 2
        even = i % 2 == 0
        ctx_after = any(m % 2 == 0 for m in range(i + 1, DEPTH))
        mod_x = mod_all[i, :bsz].reshape(bsz * 6, 1, d)
        mod_c = jnp.broadcast_to(mod_all[i, bsz].reshape(1, 6, 1, d), (bsz, 6, 1, d)).reshape(bsz * 6, 1, d)
        nw1 = norm_w[i, 0].reshape(1, d)
        nw2 = norm_w[i, 1].reshape(1, d)
        w_in, w_out, wg, wu, wd = bf16_mats
        if even:
            conv_w = conv_dw_w[j]
            lnw = conv_ln_w[j].reshape(1, CONV_CH)
            lnb = conv_ln_b[j].reshape(1, CONV_CH)
            dtab = dtabs[j]
            qc, kc, vc, gc, uc, spc, s_f = _even_in_call(ctx, mod_c, nw1, w_in, cos_c, sin_c, zero_state, dtab, tc)
            yc, s_b = _even_mix_call(ctx, qc, kc, vc, gc, uc, spc, zero_state, dtab, mod_c,
                                     conv_w, lnw, lnb, w_out, tc)
            qx, kx, vx, gx, ux, spx, _ = _even_in_call(x, mod_x, nw1, w_in, cos_x, sin_x, s_f, dtab, tx)
            x, _ = _even_mix_call(x, qx, kx, vx, gx, ux, spx, s_b, dtab, mod_x, conv_w, lnw, lnb, w_out, tx)
            if ctx_after:
                ctx = yc
        else:
            args = (nw1, w_in, pool_w[j].astype(BF16), pool_scale[j].reshape(1, POOL_CH),
                    sg_ln_w[j].reshape(1, SG_CH), sg_ln_b[j].reshape(1, SG_CH), sg_w[j].astype(BF16),
                    jnp.broadcast_to(sg_b[j][:, :, None], (SG_GROUPS, CHUNK, SG_GC)), w_out)
            if ctx_after:
                ctx = _odd_call(ctx, mod_c, *args, tc)
            x = _odd_call(x, mod_x, *args, tx)
        if ctx_after:
            ctx, _ = _ffn_call(ctx, mod_c, nw2, wg, wu, wd, fnw, tc, False)
        later = layer_matrices(i + 1) if i + 1 < DEPTH else ()
        x, bf16_mats = _ffn_call(x, mod_x, nw2, wg, wu, wd, fnw, tx, i == DEPTH - 1, later)
    return x
```

```python
import functools

import jax
import jax.numpy as jnp
from jax import lax
from jax.experimental import pallas as pl
from jax.experimental.pallas import tpu as pltpu

F32 = jnp.float32
BF16 = jnp.bfloat16

D_MODEL = 1024
DEPTH = 4
GRID_W = 64
EPS = 1e-6
HEADS = 4
HEAD_DIM = 128
RET_W = HEADS * HEAD_DIM
CHUNK = 128
ROPE_BASE = 10000.0
ROPE_PAIRS = (HEAD_DIM // 8, 3 * HEAD_DIM // 16, 3 * HEAD_DIM // 16)
CONV_CH = 512
CONV_K = 31
CONV_PAD = CONV_K // 2
EVEN_IN = 4 * RET_W + 2 * CONV_CH
POOL_CH = 512
POOL_WINDOWS = (2, 4, 8, 16)
POOL_GC = 128
SG_CH = 512
SG_GROUPS = 4
SG_GC = 128
ODD_IN = POOL_CH + 2 * SG_CH
D_FF = 2816
SUBLANES = 8
LANES = 128
FF_BLOCK = 256
OUT_BLOCK = 256
CONV_ROWS = 128
FFN_ROWS = 512
HALO = 16
N_DECAY_TABLES = 7
MOD_ROWS = 8
VMEM_LIMIT = 56 * 1024 * 1024


def _dot(a, b):
    return jnp.dot(a, b, preferred_element_type=F32)


def _dot_nt(a, b):
    return lax.dot_general(a, b, (((1,), (1,)), ((), ())), preferred_element_type=F32)


def _dot_tn(a, b):
    return lax.dot_general(a, b, (((0,), (0,)), ((), ())), preferred_element_type=F32)


def _silu(x):
    return x * jax.nn.sigmoid(x)


def _rms_mod(x32, nw, shift, scale):
    y = x32 * lax.rsqrt(jnp.mean(x32 * x32, axis=-1, keepdims=True) + EPS) * nw
    return y * (1.0 + scale) + shift


def _layer_norm(x32, w, b):
    mu = jnp.mean(x32, axis=-1, keepdims=True)
    xc = x32 - mu
    return xc * lax.rsqrt(jnp.mean(xc * xc, axis=-1, keepdims=True) + EPS) * w + b


def _const_spec(shape):
    return pl.BlockSpec(shape, lambda *_: (0,) * len(shape), pipeline_mode=pl.Buffered(1))


def _params(sem):
    return pltpu.CompilerParams(dimension_semantics=sem, vmem_limit_bytes=VMEM_LIMIT)


def _mod_kernel(c_ref, w_ref, b_ref, o_ref):
    s = _silu(c_ref[...]).astype(BF16)
    o_ref[0] = _dot(s, w_ref[0].astype(BF16)) + b_ref[0]


def _mod_call(cvec, ada_w, ada_b):
    depth, d, n = ada_w.shape
    tn = 1024
    return pl.pallas_call(
        _mod_kernel,
        grid=(depth, n // tn),
        in_specs=[_const_spec((MOD_ROWS, d)),
                  pl.BlockSpec((1, d, tn), lambda i, j: (i, 0, j)),
                  pl.BlockSpec((1, 1, tn), lambda i, j: (i, 0, j))],
        out_specs=pl.BlockSpec((1, MOD_ROWS, tn), lambda i, j: (i, 0, j)),
        out_shape=jax.ShapeDtypeStruct((depth, MOD_ROWS, n), F32),
        compiler_params=_params(("arbitrary", "arbitrary")),
        name="adaln_mod",
    )(cvec, ada_w, ada_b.reshape(depth, 1, n))


def _log_sigmoid(x):
    return -(jnp.maximum(-x, 0.0) + jnp.log1p(jnp.exp(-jnp.abs(x))))


def _decay_kernel(logit_ref, o_ref):
    j = pl.program_id(0)
    row = lax.broadcasted_iota(jnp.int32, (CHUNK, CHUNK), 0).astype(F32)
    col = lax.broadcasted_iota(jnp.int32, (CHUNK, CHUNK), 1).astype(F32)
    diff = row - col
    lower = diff >= 0.0
    for hd in range(HEADS):
        lgf = _log_sigmoid(jnp.full((CHUNK, CHUNK), logit_ref[(j * 2 + 0) * HEADS + hd], F32))
        lgb = _log_sigmoid(jnp.full((CHUNK, CHUNK), logit_ref[(j * 2 + 1) * HEADS + hd], F32))
        o_ref[0, 0, hd] = jnp.where(lower, jnp.exp(lgf * jnp.where(lower, diff, 0.0)),
                                    jnp.exp(lgb * jnp.where(lower, 0.0, -diff - 1.0)))
        o_ref[0, 1, hd] = jnp.exp(lgf * (CHUNK - 1.0 - row))
        o_ref[0, 2, hd] = jnp.exp(lgb * row)
        o_ref[0, 3, hd] = jnp.exp(lgf * (row + 1.0))
        o_ref[0, 4, hd] = jnp.exp(lgb * (CHUNK - 1.0 - row))
        o_ref[0, 5, hd] = jnp.exp(lgf * float(CHUNK))
        o_ref[0, 6, hd] = jnp.exp(lgb * float(CHUNK))


def _decay_call(ret_decay_logit):
    n_even = ret_decay_logit.shape[0]
    return pl.pallas_call(
        _decay_kernel,
        grid=(n_even,),
        in_specs=[pl.BlockSpec(memory_space=pltpu.SMEM)],
        out_specs=pl.BlockSpec((1, N_DECAY_TABLES, HEADS, CHUNK, CHUNK), lambda j: (j, 0, 0, 0, 0)),
        out_shape=jax.ShapeDtypeStruct((n_even, N_DECAY_TABLES, HEADS, CHUNK, CHUNK), F32),
        compiler_params=_params(("arbitrary",)),
        name="ret_decay",
    )(ret_decay_logit.reshape(-1).astype(F32))


def _even_in_kernel(x_ref, mod_ref, nw_ref, w_ref, cos_ref, sin_ref, s0_ref, dt_ref,
                    q_ref, k_ref, v_ref, gs_ref, u_ref, sprev_ref, sfin_ref, state_ref, *, tm):
    b = pl.program_id(0)
    i = pl.program_id(1)

    @pl.when(i == 0)
    def _():
        state_ref[...] = s0_ref[0]

    h = _rms_mod(x_ref[0], nw_ref[...], mod_ref[b * 6 + 0], mod_ref[b * 6 + 1]).astype(BF16)
    cosf = cos_ref[...]
    sinf = sin_ref[...]

    def rope(t):
        return t * cosf + pltpu.roll(t, HEAD_DIM // 2, 1) * sinf

    pq = _dot(h, w_ref[:, 0:RET_W])
    pk = _dot(h, w_ref[:, RET_W:2 * RET_W])
    pv = _dot(h, w_ref[:, 2 * RET_W:3 * RET_W])
    scale = HEAD_DIM ** -0.5
    for hd in range(HEADS):
        sl = slice(hd * HEAD_DIM, (hd + 1) * HEAD_DIM)
        q_ref[0, :, sl] = (rope(pq[:, sl]) * scale).astype(BF16)
        kr = rope(pk[:, sl]).astype(BF16)
        k_ref[0, :, sl] = kr
        vh = pv[:, sl]
        v_ref[0, :, sl] = vh.astype(BF16)
        zeta = dt_ref[1, hd]
        cdec = dt_ref[5, hd]
        for c in range(tm // CHUNK):
            rows = slice(c * CHUNK, (c + 1) * CHUNK)
            s_cur = state_ref[hd]
            sprev_ref[0, c, hd] = s_cur.astype(BF16)
            wv = (vh[rows] * zeta).astype(BF16)
            state_ref[hd] = s_cur * cdec + _dot_tn(kr[rows], wv)

    pg = _dot(h, w_ref[:, 3 * RET_W:4 * RET_W])
    gs_ref[0] = _silu(pg)
    pa = _dot(h, w_ref[:, 4 * RET_W:4 * RET_W + CONV_CH])
    pb = _dot(h, w_ref[:, 4 * RET_W + CONV_CH:EVEN_IN])
    u_ref[0] = pa * jax.nn.sigmoid(pb)

    @pl.when(i == pl.num_programs(1) - 1)
    def _():
        sfin_ref[0] = state_ref[...]


def _even_in_call(x, mod, nw, w_in, cosf, sinf, s0, dtab, tm):
    bsz, l, d = x.shape
    n_chunks = l // CHUNK
    cpt = tm // CHUNK
    tok = lambda w: pl.BlockSpec((1, tm, w), lambda b, i: (b, i, 0))
    state_spec = pl.BlockSpec((1, HEADS, CHUNK, CHUNK), lambda b, i: (b, 0, 0, 0))
    return pl.pallas_call(
        functools.partial(_even_in_kernel, tm=tm),
        grid=(bsz, l // tm),
        in_specs=[tok(d),
                  _const_spec(mod.shape),
                  _const_spec((1, d)),
                  _const_spec(w_in.shape),
                  pl.BlockSpec((tm, HEAD_DIM), lambda b, i: (i, 0)),
                  pl.BlockSpec((tm, HEAD_DIM), lambda b, i: (i, 0)),
                  state_spec,
                  _const_spec(dtab.shape)],
        out_specs=[tok(RET_W), tok(RET_W), tok(RET_W), tok(RET_W), tok(CONV_CH),
                   pl.BlockSpec((1, cpt, HEADS, CHUNK, CHUNK), lambda b, i: (b, i, 0, 0, 0)),
                   state_spec],
        out_shape=[jax.ShapeDtypeStruct((bsz, l, RET_W), BF16),
                   jax.ShapeDtypeStruct((bsz, l, RET_W), BF16),
                   jax.ShapeDtypeStruct((bsz, l, RET_W), BF16),
                   jax.ShapeDtypeStruct((bsz, l, RET_W), F32),
                   jax.ShapeDtypeStruct((bsz, l, CONV_CH), F32),
                   jax.ShapeDtypeStruct((bsz, n_chunks, HEADS, CHUNK, CHUNK), BF16),
                   jax.ShapeDtypeStruct((bsz, HEADS, CHUNK, CHUNK), F32)],
        scratch_shapes=[pltpu.VMEM((HEADS, CHUNK, CHUNK), F32)],
        compiler_params=_params(("arbitrary", "arbitrary")),
        name="even_in",
    )(x, mod, nw, w_in, cosf, sinf, s0, dtab)


def _even_mix_pieces(b, i, n_tiles, x_ref, q_ref, k_ref, v_ref, gs_ref, u_ref, up_ref, un_ref, sprev_ref,
                     dt_ref, mod_ref, cw_ref, lnw_ref, lnb_ref, wo_ref, state_ref, uext_ref, mix_ref, xo_ref, tm):
    def retention(c, hd):
        rows = slice(c * CHUNK, (c + 1) * CHUNK)
        sl = slice(hd * HEAD_DIM, (hd + 1) * HEAD_DIM)
        qc = q_ref[0, rows, sl]
        kc = k_ref[0, rows, sl]
        vc = v_ref[0, rows, sl]
        s_b = state_ref[hd]
        p = (_dot_nt(qc, kc) * dt_ref[0, hd]).astype(BF16)
        y = _dot(p, vc)
        y = y + _dot(qc, sprev_ref[0, c, hd]) * dt_ref[3, hd]
        y = y + _dot(qc, s_b.astype(BF16)) * dt_ref[4, hd]
        y = y * lax.rsqrt(jnp.mean(y * y, axis=-1, keepdims=True) + EPS)
        mix_ref[rows, sl] = (gs_ref[0, rows, sl] * y).astype(BF16)
        wv = (vc.astype(F32) * dt_ref[2, hd]).astype(BF16)
        state_ref[hd] = s_b * dt_ref[6, hd] + _dot_tn(kc, wv)

    def halo():
        uext_ref[0:HALO] = jnp.where(i < n_tiles - 1, up_ref[0], 0.0)
        uext_ref[HALO:HALO + tm] = u_ref[0]
        uext_ref[HALO + tm:2 * HALO + tm] = jnp.where(i > 0, un_ref[0], 0.0)

    def conv(r):
        blocks = []
        for cb in range(CONV_CH // LANES):
            cols = slice(cb * LANES, (cb + 1) * LANES)
            acc = None
            for sub in range(SUBLANES):
                part = None
                for t in range(sub, CONV_K, SUBLANES):
                    start = r * CONV_ROWS + t - sub
                    term = uext_ref[start:start + CONV_ROWS + SUBLANES, cols] * cw_ref[t:t + 1, cols]
                    part = term if part is None else part + term
                shifted = part[sub + 1:sub + 1 + CONV_ROWS]
                acc = shifted if acc is None else acc + shifted
            blocks.append(acc)
        cv = _silu(_layer_norm(jnp.concatenate(blocks, axis=-1), lnw_ref[...], lnb_ref[...]))
        mix_ref[r * CONV_ROWS:(r + 1) * CONV_ROWS, RET_W:RET_W + CONV_CH] = cv.astype(BF16)

    def out_proj(n):
        cols = slice(n * OUT_BLOCK, (n + 1) * OUT_BLOCK)
        out = _dot(mix_ref[...], wo_ref[:, cols])
        xo_ref[:, cols] = x_ref[0, :, cols] + mod_ref[b * 6 + 2][:, cols] * out

    pieces = [halo]
    convs = [functools.partial(conv, r) for r in range(tm // CONV_ROWS)]
    rets = [functools.partial(retention, c, hd) for c in reversed(range(tm // CHUNK)) for hd in range(HEADS)]
    per = len(rets) // len(convs)
    for r, cv in enumerate(convs):
        pieces += rets[r * per:(r + 1) * per] + [cv]
    pieces += rets[len(convs) * per:]
    pieces += [functools.partial(out_proj, n) for n in range(D_MODEL // OUT_BLOCK)]
    return pieces


def _ffn_pieces(xs_ref, o_ref, b, mod_ref, nw_ref, wg_ref, wu_ref, wd_ref, fnw_ref, h_ref, act_ref, final_norm):
    def prologue():
        x32 = xs_ref[...]
        o_ref[...] = x32
        h_ref[...] = _rms_mod(x32, nw_ref[...], mod_ref[b * 6 + 3], mod_ref[b * 6 + 4]).astype(BF16)

    def gate_up(j):
        cols = slice(j * FF_BLOCK, (j + 1) * FF_BLOCK)
        h = h_ref[...]
        act_ref[:, cols] = (_silu(_dot(h, wg_ref[:, cols])) * _dot(h, wu_ref[:, cols])).astype(BF16)

    def down(n):
        cols = slice(n * OUT_BLOCK, (n + 1) * OUT_BLOCK)
        o_ref[:, cols] = o_ref[:, cols] + mod_ref[b * 6 + 5][:, cols] * _dot(act_ref[...], wd_ref[:, cols])

    def final():
        y = o_ref[...]
        o_ref[...] = y * lax.rsqrt(jnp.mean(y * y, axis=-1, keepdims=True) + EPS) * fnw_ref[...]

    pieces = [prologue]
    pieces += [functools.partial(gate_up, j) for j in range(D_FF // FF_BLOCK)]
    pieces += [functools.partial(down, n) for n in range(D_MODEL // OUT_BLOCK)]
    if final_norm:
        pieces.append(final)
    return pieces


def _even_mix_kernel(x_ref, q_ref, k_ref, v_ref, gs_ref, u_ref, up_ref, un_ref, sprev_ref, sb0_ref,
                     dt_ref, mod_ref, cw_ref, lnw_ref, lnb_ref, wo_ref,
                     o_ref, sbfin_ref, state_ref, uext_ref, mix_ref, *, tm):
    b = pl.program_id(0)
    i = pl.program_id(1)
    n_tiles = pl.num_programs(1)

    @pl.when(i == 0)
    def _():
        state_ref[...] = sb0_ref[0]

    for piece in _even_mix_pieces(b, i, n_tiles, x_ref, q_ref, k_ref, v_ref, gs_ref, u_ref, up_ref, un_ref,
                                  sprev_ref, dt_ref, mod_ref, cw_ref, lnw_ref, lnb_ref, wo_ref,
                                  state_ref, uext_ref, mix_ref, o_ref.at[0], tm):
        piece()

    @pl.when(i == n_tiles - 1)
    def _():
        sbfin_ref[0] = state_ref[...]


def _even_mix_call(x, q, k, v, gs, u, sprev, sb0, dtab, mod, conv_w, ln_w, ln_b, w_out, tm):
    bsz, l, d = x.shape
    n_tiles = l // tm
    cpt = tm // CHUNK
    hpt = tm // HALO
    n_halo = l // HALO
    rev = lambda i: n_tiles - 1 - i
    tok = lambda w: pl.BlockSpec((1, tm, w), lambda b, i: (b, rev(i), 0))
    state_spec = pl.BlockSpec((1, HEADS, CHUNK, CHUNK), lambda b, i: (b, 0, 0, 0))
    return pl.pallas_call(
        functools.partial(_even_mix_kernel, tm=tm),
        grid=(bsz, n_tiles),
        in_specs=[tok(d), tok(RET_W), tok(RET_W), tok(RET_W), tok(RET_W), tok(CONV_CH),
                  pl.BlockSpec((1, HALO, CONV_CH), lambda b, i: (b, jnp.maximum(rev(i) * hpt - 1, 0), 0)),
                  pl.BlockSpec((1, HALO, CONV_CH),
                               lambda b, i: (b, jnp.minimum((rev(i) + 1) * hpt, n_halo - 1), 0)),
                  pl.BlockSpec((1, cpt, HEADS, CHUNK, CHUNK), lambda b, i: (b, rev(i), 0, 0, 0)),
                  state_spec,
                  _const_spec(dtab.shape),
                  _const_spec(mod.shape),
                  _const_spec(conv_w.shape),
                  _const_spec((1, CONV_CH)),
                  _const_spec((1, CONV_CH)),
                  _const_spec(w_out.shape)],
        out_specs=[tok(d), state_spec],
        out_shape=[jax.ShapeDtypeStruct((bsz, l, d), F32),
                   jax.ShapeDtypeStruct((bsz, HEADS, CHUNK, CHUNK), F32)],
        scratch_shapes=[pltpu.VMEM((HEADS, CHUNK, CHUNK), F32),
                        pltpu.VMEM((tm + 2 * HALO, CONV_CH), F32),
                        pltpu.VMEM((tm, RET_W + CONV_CH), BF16)],
        compiler_params=_params(("arbitrary", "arbitrary")),
        name="even_mix",
    )(x, q, k, v, gs, u, u, u, sprev, sb0, dtab, mod, conv_w, ln_w, ln_b, w_out)


def _odd_kernel(x_ref, xp_ref, xn_ref, mod_ref, nw_ref, wi_ref, pw_ref, ps_ref, lnw_ref, lnb_ref,
                sgw_ref, sgb_ref, wo_ref, o_ref, h_ref, pcx_ref, mix_ref, *, tm, seq_len):
    b = pl.program_id(0)
    i = pl.program_id(1)
    n_tiles = pl.num_programs(1)
    nw = nw_ref[...]
    shift = mod_ref[b * 6 + 0]
    scale = mod_ref[b * 6 + 1]

    h_ref[0:HALO] = _rms_mod(xp_ref[0], nw, shift, scale).astype(BF16)
    h_ref[HALO:HALO + tm] = _rms_mod(x_ref[0], nw, shift, scale).astype(BF16)
    h_ref[HALO + tm:2 * HALO + tm] = _rms_mod(xn_ref[0], nw, shift, scale).astype(BF16)

    pcx_ref[0:tm + 2 * HALO] = _dot(h_ref[...], wi_ref[:, 0:POOL_CH])
    pcx_ref[0:HALO] = jnp.where(i > 0, pcx_ref[0:HALO], 0.0)
    pcx_ref[HALO + tm:2 * HALO + tm] = jnp.where(i < n_tiles - 1, pcx_ref[HALO + tm:2 * HALO + tm], 0.0)
    pos = i * tm + lax.broadcasted_iota(jnp.int32, (tm, POOL_GC), 0)
    n_ext = tm + 2 * HALO + SUBLANES
    pcx_ref[tm + 2 * HALO:n_ext] = jnp.zeros((SUBLANES, POOL_CH), F32)
    for gi, w in enumerate(POOL_WINDOWS):
        left = w // 2
        right = w - 1 - left
        sl = slice(gi * POOL_GC, (gi + 1) * POOL_GC)
        run = pcx_ref[:, sl]
        span = 1
        while span < w:
            keep = run.shape[0] - SUBLANES
            run = run[0:keep] + run[span:span + keep]
            span *= 2
        win = run[HALO - left:HALO - left + tm]
        cnt = jnp.minimum(pos + right, seq_len - 1) - jnp.maximum(pos - left, 0) + 1
        m = win / cnt.astype(F32) - pcx_ref[HALO:HALO + tm, sl]
        po = _dot(m.astype(BF16), pw_ref[gi]) * ps_ref[:, sl]
        mix_ref[:, sl] = po.astype(BF16)

    pd = _dot(h_ref[HALO:HALO + tm], wi_ref[:, POOL_CH:ODD_IN])
    z = 0.5 * pd * (1.0 + lax.erf(pd * (2.0 ** -0.5)))
    zu = z[:, 0:SG_CH]
    zv = _layer_norm(z[:, SG_CH:2 * SG_CH], lnw_ref[...], lnb_ref[...]).astype(BF16)
    for c in range(tm // CHUNK):
        rows = slice(c * CHUNK, (c + 1) * CHUNK)
        for g in range(SG_GROUPS):
            sl = slice(g * SG_GC, (g + 1) * SG_GC)
            s = _dot(sgw_ref[g], zv[rows, sl]) + sgb_ref[g]
            mix_ref[rows, POOL_CH + g * SG_GC:POOL_CH + (g + 1) * SG_GC] = (zu[rows, sl] * s).astype(BF16)

    out = _dot(mix_ref[...], wo_ref[...])
    o_ref[0] = x_ref[0] + mod_ref[b * 6 + 2] * out


def _odd_call(x, mod, nw, w_in, pool_w, pool_scale, ln_w, ln_b, sg_w, sg_b_full, w_out, tm):
    bsz, l, d = x.shape
    hpt = tm // HALO
    n_halo = l // HALO
    tok = pl.BlockSpec((1, tm, d), lambda b, i: (b, i, 0))
    return pl.pallas_call(
        functools.partial(_odd_kernel, tm=tm, seq_len=l),
        grid=(bsz, l // tm),
        in_specs=[tok,
                  pl.BlockSpec((1, HALO, d), lambda b, i: (b, jnp.maximum(i * hpt - 1, 0), 0)),
                  pl.BlockSpec((1, HALO, d), lambda b, i: (b, jnp.minimum((i + 1) * hpt, n_halo - 1), 0)),
                  _const_spec(mod.shape),
                  _const_spec((1, d)),
                  _const_spec(w_in.shape),
                  _const_spec(pool_w.shape),
                  _const_spec((1, POOL_CH)),
                  _const_spec((1, SG_CH)),
                  _const_spec((1, SG_CH)),
                  _const_spec(sg_w.shape),
                  _const_spec(sg_b_full.shape),
                  _const_spec(w_out.shape)],
        out_specs=tok,
        out_shape=jax.ShapeDtypeStruct((bsz, l, d), F32),
        scratch_shapes=[pltpu.VMEM((tm + 2 * HALO, d), BF16),
                        pltpu.VMEM((tm + 2 * HALO + SUBLANES, POOL_CH), F32),
                        pltpu.VMEM((tm, POOL_CH + SG_CH), BF16)],
        compiler_params=_params(("arbitrary", "arbitrary")),
        name="odd_mix",
    )(x, x, x, mod, nw, w_in, pool_w, pool_scale, ln_w, ln_b, sg_w, sg_b_full, w_out)


def _ffn_kernel(x_ref, mod_ref, nw_ref, wg_ref, wu_ref, wd_ref, fnw_ref, *rest, n_cast, final_norm):
    cast_in = rest[:n_cast]
    o_ref = rest[n_cast]
    cast_out = rest[n_cast + 1:2 * n_cast + 1]
    h_ref, act_ref = rest[2 * n_cast + 1:]
    for src, dst in zip(cast_in, cast_out):
        dst[...] = src[...].astype(BF16)
    tm = x_ref.shape[1]
    subs = []
    for r0 in range(0, tm, min(tm, FFN_ROWS)):
        rows = pl.ds(r0, min(tm, FFN_ROWS))
        subs.append(_ffn_pieces(x_ref.at[0, rows], o_ref.at[0, rows], pl.program_id(0), mod_ref, nw_ref,
                                wg_ref, wu_ref, wd_ref, fnw_ref, h_ref.at[rows], act_ref.at[rows], final_norm))
    order = [subs[0][0]]
    for k, pieces in enumerate(subs):
        order.append(pieces[1])
        if k + 1 < len(subs):
            order.append(subs[k + 1][0])
        order += pieces[2:]
    for piece in order:
        piece()


def _ffn_call(x, mod, nw, wg, wu, wd, fnw, tm, final_norm, cast=()):
    bsz, l, d = x.shape
    n_tiles = l // tm
    n_steps = bsz * n_tiles
    tok = pl.BlockSpec((1, tm, d), lambda b, i: (b, i, 0))

    def band_specs(stack, layer):
        _, rows, cols = stack.shape
        hold = next(h for h in range(1, n_steps + 1)
                    if n_steps % h == 0 and rows % (n_steps // h) == 0 and (rows * h // n_steps) % 16 == 0)
        band = rows * hold // n_steps
        return (pl.BlockSpec((None, band, cols), lambda b, i: (layer, (b * n_tiles + i) // hold, 0)),
                pl.BlockSpec((band, cols), lambda b, i: ((b * n_tiles + i) // hold, 0)))

    specs = [band_specs(stack, layer) for stack, layer in cast]
    outs = pl.pallas_call(
        functools.partial(_ffn_kernel, n_cast=len(cast), final_norm=final_norm),
        grid=(bsz, n_tiles),
        in_specs=[tok,
                  _const_spec(mod.shape),
                  _const_spec((1, d)),
                  _const_spec(wg.shape),
                  _const_spec(wu.shape),
                  _const_spec(wd.shape),
                  _const_spec((1, d))] + [s[0] for s in specs],
        out_specs=[tok] + [s[1] for s in specs],
        out_shape=[jax.ShapeDtypeStruct((bsz, l, d), F32)]
        + [jax.ShapeDtypeStruct(stack.shape[1:], BF16) for stack, _ in cast],
        scratch_shapes=[pltpu.VMEM((tm, d), BF16), pltpu.VMEM((tm, D_FF), BF16)],
        compiler_params=_params(("arbitrary", "arbitrary")),
        name="ffn",
    )(x, mod, nw, wg, wu, wd, fnw, *[stack for stack, _ in cast])
    return outs[0], tuple(outs[1:])


def _rope_tables(p_seq, p_row, p_col):
    parts = []
    for p, n in zip((p_seq, p_row, p_col), ROPE_PAIRS):
        freq = ROPE_BASE ** (-jnp.arange(n, dtype=F32) / n)
        parts.append(p[:, None] * freq[None, :])
    ang = jnp.concatenate(parts, axis=-1)
    cos, sin = jnp.cos(ang), jnp.sin(ang)
    return jnp.concatenate([cos, cos], axis=-1), jnp.concatenate([-sin, sin], axis=-1)


def _tile(l):
    return min(l, 512)


def kernel(x, c, ctx, c_ctx, ada_w, ada_b, norm_w, even_w_in, even_w_out, ret_decay_logit, conv_dw_w, conv_ln_w, conv_ln_b, odd_w_in, odd_w_out, pool_w, pool_scale, sg_ln_w, sg_ln_b, sg_w, sg_b, ffn_w_gate, ffn_w_up, ffn_w_down, final_norm_w):
    bsz, l, d = x.shape
    lc = ctx.shape[1]
    assert d == D_MODEL and l % 256 == 0 and lc % CHUNK == 0 and bsz + 1 <= MOD_ROWS
    rows = l // GRID_W
    grid_r = jnp.broadcast_to(jnp.arange(rows, dtype=F32)[:, None], (rows, GRID_W)).reshape(-1)
    grid_c = jnp.broadcast_to(jnp.arange(GRID_W, dtype=F32)[None, :], (rows, GRID_W)).reshape(-1)
    cos_x, sin_x = _rope_tables(jnp.full((l,), lc, F32), grid_r, grid_c)
    zeros_c = jnp.zeros((lc,), F32)
    cos_c, sin_c = _rope_tables(jnp.arange(lc, dtype=F32), zeros_c, zeros_c)

    cvec = jnp.concatenate([c, c_ctx[None, :], jnp.zeros((MOD_ROWS - bsz - 1, d), F32)], axis=0)
    mod_all = _mod_call(cvec, ada_w, ada_b)
    dtabs = _decay_call(ret_decay_logit)
    zero_state = jnp.zeros((bsz, HEADS, CHUNK, CHUNK), F32)
    fnw = final_norm_w.reshape(1, d)
    tx, tc = _tile(l), _tile(lc)

    def layer_matrices(i):
        mix_in, mix_out = (even_w_in, even_w_out) if i % 2 == 0 else (odd_w_in, odd_w_out)
        return (mix_in, i // 2), (mix_out, i // 2), (ffn_w_gate, i), (ffn_w_up, i), (ffn_w_down, i)

    bf16_mats = tuple(stack[layer].astype(BF16) for stack, layer in layer_matrices(0))
    for i in range(DEPTH):
        j = i // 2
        even = i % 2 == 0
        ctx_after = any(m % 2 == 0 for m in range(i + 1, DEPTH))
        mod_x = mod_all[i, :bsz].reshape(bsz * 6, 1, d)
        mod_c = jnp.broadcast_to(mod_all[i, bsz].reshape(1, 6, 1, d), (bsz, 6, 1, d)).reshape(bsz * 6, 1, d)
        nw1 = norm_w[i, 0].reshape(1, d)
        nw2 = norm_w[i, 1].reshape(1, d)
        w_in, w_out, wg, wu, wd = bf16_mats
        if even:
            conv_w = conv_dw_w[j]
            lnw = conv_ln_w[j].reshape(1, CONV_CH)
            lnb = conv_ln_b[j].reshape(1, CONV_CH)
            dtab = dtabs[j]
            qc, kc, vc, gc, uc, spc, s_f = _even_in_call(ctx, mod_c, nw1, w_in, cos_c, sin_c, zero_state, dtab, tc)
            yc, s_b = _even_mix_call(ctx, qc, kc, vc, gc, uc, spc, zero_state, dtab, mod_c,
                                     conv_w, lnw, lnb, w_out, tc)
            qx, kx, vx, gx, ux, spx, _ = _even_in_call(x, mod_x, nw1, w_in, cos_x, sin_x, s_f, dtab, tx)
            x, _ = _even_mix_call(x, qx, kx, vx, gx, ux, spx, s_b, dtab, mod_x, conv_w, lnw, lnb, w_out, tx)
            if ctx_after:
                ctx = yc
        else:
            args = (nw1, w_in, pool_w[j].astype(BF16), pool_scale[j].reshape(1, POOL_CH),
                    sg_ln_w[j].reshape(1, SG_CH), sg_ln_b[j].reshape(1, SG_CH), sg_w[j].astype(BF16),
                    jnp.broadcast_to(sg_b[j][:, :, None], (SG_GROUPS, CHUNK, SG_GC)), w_out)
            if ctx_after:
                ctx = _odd_call(ctx, mod_c, *args, tc)
            x = _odd_call(x, mod_x, *args, tx)
        if ctx_after:
            ctx, _ = _ffn_call(ctx, mod_c, nw2, wg, wu, wd, fnw, tc, False)
        later = layer_matrices(i + 1) if i + 1 < DEPTH else ()
        x, bf16_mats = _ffn_call(x, mod_x, nw2, wg, wu, wd, fnw, min(l, 2 * FFN_ROWS), i == DEPTH - 1, later)
    return x
```

```python
import functools

import jax
import jax.numpy as jnp
from jax import lax
from jax.experimental import pallas as pl
from jax.experimental.pallas import tpu as pltpu

F32 = jnp.float32
BF16 = jnp.bfloat16

D_MODEL = 1024
DEPTH = 4
GRID_W = 64
EPS = 1e-6
HEADS = 4
HEAD_DIM = 128
RET_W = HEADS * HEAD_DIM
CHUNK = 128
ROPE_BASE = 10000.0
ROPE_PAIRS = (HEAD_DIM // 8, 3 * HEAD_DIM // 16, 3 * HEAD_DIM // 16)
CONV_CH = 512
CONV_K = 31
CONV_PAD = CONV_K // 2
EVEN_IN = 4 * RET_W + 2 * CONV_CH
POOL_CH = 512
POOL_WINDOWS = (2, 4, 8, 16)
POOL_GC = 128
SG_CH = 512
SG_GROUPS = 4
SG_GC = 128
ODD_IN = POOL_CH + 2 * SG_CH
D_FF = 2816
SUBLANES = 8
LANES = 128
FF_BLOCK = 256
OUT_BLOCK = 256
CONV_ROWS = 128
FFN_ROWS = 512
HALO = 16
N_DECAY_TABLES = 7
MOD_ROWS = 8
VMEM_LIMIT = 56 * 1024 * 1024


def _dot(a, b):
    return jnp.dot(a, b, preferred_element_type=F32)


def _dot_nt(a, b):
    return lax.dot_general(a, b, (((1,), (1,)), ((), ())), preferred_element_type=F32)


def _dot_tn(a, b):
    return lax.dot_general(a, b, (((0,), (0,)), ((), ())), preferred_element_type=F32)


def _silu(x):
    return x * jax.nn.sigmoid(x)


def _rms_mod(x32, nw, shift, scale):
    y = x32 * lax.rsqrt(jnp.mean(x32 * x32, axis=-1, keepdims=True) + EPS) * nw
    return y * (1.0 + scale) + shift


def _layer_norm(x32, w, b):
    mu = jnp.mean(x32, axis=-1, keepdims=True)
    xc = x32 - mu
    return xc * lax.rsqrt(jnp.mean(xc * xc, axis=-1, keepdims=True) + EPS) * w + b


def _const_spec(shape):
    return pl.BlockSpec(shape, lambda *_: (0,) * len(shape), pipeline_mode=pl.Buffered(1))


def _params(sem):
    return pltpu.CompilerParams(dimension_semantics=sem, vmem_limit_bytes=VMEM_LIMIT)


def _mod_kernel(c_ref, w_ref, b_ref, o_ref):
    s = _silu(c_ref[...]).astype(BF16)
    o_ref[0] = _dot(s, w_ref[0].astype(BF16)) + b_ref[0]


def _mod_call(cvec, ada_w, ada_b):
    depth, d, n = ada_w.shape
    tn = 1024
    return pl.pallas_call(
        _mod_kernel,
        grid=(depth, n // tn),
        in_specs=[_const_spec((MOD_ROWS, d)),
                  pl.BlockSpec((1, d, tn), lambda i, j: (i, 0, j)),
                  pl.BlockSpec((1, 1, tn), lambda i, j: (i, 0, j))],
        out_specs=pl.BlockSpec((1, MOD_ROWS, tn), lambda i, j: (i, 0, j)),
        out_shape=jax.ShapeDtypeStruct((depth, MOD_ROWS, n), F32),
        compiler_params=_params(("arbitrary", "arbitrary")),
        name="adaln_mod",
    )(cvec, ada_w, ada_b.reshape(depth, 1, n))


def _log_sigmoid(x):
    return -(jnp.maximum(-x, 0.0) + jnp.log1p(jnp.exp(-jnp.abs(x))))


def _decay_kernel(logit_ref, o_ref):
    j = pl.program_id(0)
    row = lax.broadcasted_iota(jnp.int32, (CHUNK, CHUNK), 0).astype(F32)
    col = lax.broadcasted_iota(jnp.int32, (CHUNK, CHUNK), 1).astype(F32)
    diff = row - col
    lower = diff >= 0.0
    for hd in range(HEADS):
        lgf = _log_sigmoid(jnp.full((CHUNK, CHUNK), logit_ref[(j * 2 + 0) * HEADS + hd], F32))
        lgb = _log_sigmoid(jnp.full((CHUNK, CHUNK), logit_ref[(j * 2 + 1) * HEADS + hd], F32))
        o_ref[0, 0, hd] = jnp.where(lower, jnp.exp(lgf * jnp.where(lower, diff, 0.0)),
                                    jnp.exp(lgb * jnp.where(lower, 0.0, -diff - 1.0)))
        o_ref[0, 1, hd] = jnp.exp(lgf * (CHUNK - 1.0 - row))
        o_ref[0, 2, hd] = jnp.exp(lgb * row)
        o_ref[0, 3, hd] = jnp.exp(lgf * (row + 1.0))
        o_ref[0, 4, hd] = jnp.exp(lgb * (CHUNK - 1.0 - row))
        o_ref[0, 5, hd] = jnp.exp(lgf * float(CHUNK))
        o_ref[0, 6, hd] = jnp.exp(lgb * float(CHUNK))


def _decay_call(ret_decay_logit):
    n_even = ret_decay_logit.shape[0]
    return pl.pallas_call(
        _decay_kernel,
        grid=(n_even,),
        in_specs=[pl.BlockSpec(memory_space=pltpu.SMEM)],
        out_specs=pl.BlockSpec((1, N_DECAY_TABLES, HEADS, CHUNK, CHUNK), lambda j: (j, 0, 0, 0, 0)),
        out_shape=jax.ShapeDtypeStruct((n_even, N_DECAY_TABLES, HEADS, CHUNK, CHUNK), F32),
        compiler_params=_params(("arbitrary",)),
        name="ret_decay",
    )(ret_decay_logit.reshape(-1).astype(F32))


def _even_in_kernel(x_ref, mod_ref, nw_ref, w_ref, cos_ref, sin_ref, s0_ref, dt_ref,
                    q_ref, k_ref, v_ref, gs_ref, u_ref, sprev_ref, sfin_ref, state_ref, *, tm):
    b = pl.program_id(0)
    i = pl.program_id(1)

    @pl.when(i == 0)
    def _():
        state_ref[...] = s0_ref[0]

    h = _rms_mod(x_ref[0], nw_ref[...], mod_ref[b * 6 + 0], mod_ref[b * 6 + 1]).astype(BF16)
    cosf = cos_ref[...]
    sinf = sin_ref[...]

    def rope(t):
        return t * cosf + pltpu.roll(t, HEAD_DIM // 2, 1) * sinf

    pq = _dot(h, w_ref[:, 0:RET_W])
    pk = _dot(h, w_ref[:, RET_W:2 * RET_W])
    pv = _dot(h, w_ref[:, 2 * RET_W:3 * RET_W])
    scale = HEAD_DIM ** -0.5
    for hd in range(HEADS):
        sl = slice(hd * HEAD_DIM, (hd + 1) * HEAD_DIM)
        q_ref[0, :, sl] = (rope(pq[:, sl]) * scale).astype(BF16)
        kr = rope(pk[:, sl]).astype(BF16)
        k_ref[0, :, sl] = kr
        vh = pv[:, sl]
        v_ref[0, :, sl] = vh.astype(BF16)
        zeta = dt_ref[1, hd]
        cdec = dt_ref[5, hd]
        for c in range(tm // CHUNK):
            rows = slice(c * CHUNK, (c + 1) * CHUNK)
            s_cur = state_ref[hd]
            sprev_ref[0, c, hd] = s_cur.astype(BF16)
            wv = (vh[rows] * zeta).astype(BF16)
            state_ref[hd] = s_cur * cdec + _dot_tn(kr[rows], wv)

    pg = _dot(h, w_ref[:, 3 * RET_W:4 * RET_W])
    gs_ref[0] = _silu(pg)
    pa = _dot(h, w_ref[:, 4 * RET_W:4 * RET_W + CONV_CH])
    pb = _dot(h, w_ref[:, 4 * RET_W + CONV_CH:EVEN_IN])
    u_ref[0] = pa * jax.nn.sigmoid(pb)

    @pl.when(i == pl.num_programs(1) - 1)
    def _():
        sfin_ref[0] = state_ref[...]


def _even_in_call(x, mod, nw, w_in, cosf, sinf, s0, dtab, tm):
    bsz, l, d = x.shape
    n_chunks = l // CHUNK
    cpt = tm // CHUNK
    tok = lambda w: pl.BlockSpec((1, tm, w), lambda b, i: (b, i, 0))
    state_spec = pl.BlockSpec((1, HEADS, CHUNK, CHUNK), lambda b, i: (b, 0, 0, 0))
    return pl.pallas_call(
        functools.partial(_even_in_kernel, tm=tm),
        grid=(bsz, l // tm),
        in_specs=[tok(d),
                  _const_spec(mod.shape),
                  _const_spec((1, d)),
                  _const_spec(w_in.shape),
                  pl.BlockSpec((tm, HEAD_DIM), lambda b, i: (i, 0)),
                  pl.BlockSpec((tm, HEAD_DIM), lambda b, i: (i, 0)),
                  state_spec,
                  _const_spec(dtab.shape)],
        out_specs=[tok(RET_W), tok(RET_W), tok(RET_W), tok(RET_W), tok(CONV_CH),
                   pl.BlockSpec((1, cpt, HEADS, CHUNK, CHUNK), lambda b, i: (b, i, 0, 0, 0)),
                   state_spec],
        out_shape=[jax.ShapeDtypeStruct((bsz, l, RET_W), BF16),
                   jax.ShapeDtypeStruct((bsz, l, RET_W), BF16),
                   jax.ShapeDtypeStruct((bsz, l, RET_W), BF16),
                   jax.ShapeDtypeStruct((bsz, l, RET_W), F32),
                   jax.ShapeDtypeStruct((bsz, l, CONV_CH), F32),
                   jax.ShapeDtypeStruct((bsz, n_chunks, HEADS, CHUNK, CHUNK), BF16),
                   jax.ShapeDtypeStruct((bsz, HEADS, CHUNK, CHUNK), F32)],
        scratch_shapes=[pltpu.VMEM((HEADS, CHUNK, CHUNK), F32)],
        compiler_params=_params(("arbitrary", "arbitrary")),
        name="even_in",
    )(x, mod, nw, w_in, cosf, sinf, s0, dtab)


def _even_mix_pieces(b, i, n_tiles, x_ref, q_ref, k_ref, v_ref, gs_ref, u_ref, up_ref, un_ref, sprev_ref,
                     dt_ref, mod_ref, cw_ref, lnw_ref, lnb_ref, wo_ref, state_ref, uext_ref, mix_ref, xo_ref, tm):
    def retention(c, hd):
        rows = slice(c * CHUNK, (c + 1) * CHUNK)
        sl = slice(hd * HEAD_DIM, (hd + 1) * HEAD_DIM)
        qc = q_ref[0, rows, sl]
        kc = k_ref[0, rows, sl]
        vc = v_ref[0, rows, sl]
        s_b = state_ref[hd]
        p = (_dot_nt(qc, kc) * dt_ref[0, hd]).astype(BF16)
        y = _dot(p, vc)
        y = y + _dot(qc, sprev_ref[0, c, hd]) * dt_ref[3, hd]
        y = y + _dot(qc, s_b.astype(BF16)) * dt_ref[4, hd]
        y = y * lax.rsqrt(jnp.mean(y * y, axis=-1, keepdims=True) + EPS)
        mix_ref[rows, sl] = (gs_ref[0, rows, sl] * y).astype(BF16)
        wv = (vc.astype(F32) * dt_ref[2, hd]).astype(BF16)
        state_ref[hd] = s_b * dt_ref[6, hd] + _dot_tn(kc, wv)

    def halo():
        uext_ref[0:HALO] = jnp.where(i < n_tiles - 1, up_ref[0], 0.0)
        uext_ref[HALO:HALO + tm] = u_ref[0]
        uext_ref[HALO + tm:2 * HALO + tm] = jnp.where(i > 0, un_ref[0], 0.0)

    def conv(r):
        blocks = []
        for cb in range(CONV_CH // LANES):
            cols = slice(cb * LANES, (cb + 1) * LANES)
            acc = None
            for sub in range(SUBLANES):
                part = None
                for t in range(sub, CONV_K, SUBLANES):
                    start = r * CONV_ROWS + t - sub
                    term = uext_ref[start:start + CONV_ROWS + SUBLANES, cols] * cw_ref[t:t + 1, cols]
                    part = term if part is None else part + term
                shifted = part[sub + 1:sub + 1 + CONV_ROWS]
                acc = shifted if acc is None else acc + shifted
            blocks.append(acc)
        cv = _silu(_layer_norm(jnp.concatenate(blocks, axis=-1), lnw_ref[...], lnb_ref[...]))
        mix_ref[r * CONV_ROWS:(r + 1) * CONV_ROWS, RET_W:RET_W + CONV_CH] = cv.astype(BF16)

    def out_proj(n):
        cols = slice(n * OUT_BLOCK, (n + 1) * OUT_BLOCK)
        out = _dot(mix_ref[...], wo_ref[:, cols])
        xo_ref[:, cols] = x_ref[0, :, cols] + mod_ref[b * 6 + 2][:, cols] * out

    pieces = [halo]
    convs = [functools.partial(conv, r) for r in range(tm // CONV_ROWS)]
    rets = [functools.partial(retention, c, hd) for c in reversed(range(tm // CHUNK)) for hd in range(HEADS)]
    per = len(rets) // len(convs)
    for r, cv in enumerate(convs):
        pieces += rets[r * per:(r + 1) * per] + [cv]
    pieces += rets[len(convs) * per:]
    pieces += [functools.partial(out_proj, n) for n in range(D_MODEL // OUT_BLOCK)]
    return pieces


def _ffn_pieces(xs_ref, o_ref, b, mod_ref, nw_ref, wg_ref, wu_ref, wd_ref, fnw_ref, h_ref, act_ref, final_norm):
    def prologue():
        x32 = xs_ref[...]
        o_ref[...] = x32
        h_ref[...] = _rms_mod(x32, nw_ref[...], mod_ref[b * 6 + 3], mod_ref[b * 6 + 4]).astype(BF16)

    def gate_up(j):
        cols = slice(j * FF_BLOCK, (j + 1) * FF_BLOCK)
        h = h_ref[...]
        act_ref[:, cols] = (_silu(_dot(h, wg_ref[:, cols])) * _dot(h, wu_ref[:, cols])).astype(BF16)

    def down(n):
        cols = slice(n * OUT_BLOCK, (n + 1) * OUT_BLOCK)
        o_ref[:, cols] = o_ref[:, cols] + mod_ref[b * 6 + 5][:, cols] * _dot(act_ref[...], wd_ref[:, cols])

    def final():
        y = o_ref[...]
        o_ref[...] = y * lax.rsqrt(jnp.mean(y * y, axis=-1, keepdims=True) + EPS) * fnw_ref[...]

    pieces = [prologue]
    pieces += [functools.partial(gate_up, j) for j in range(D_FF // FF_BLOCK)]
    pieces += [functools.partial(down, n) for n in range(D_MODEL // OUT_BLOCK)]
    if final_norm:
        pieces.append(final)
    return pieces


def _even_mix_kernel(x_ref, q_ref, k_ref, v_ref, gs_ref, u_ref, up_ref, un_ref, sprev_ref, sb0_ref,
                     dt_ref, mod_ref, cw_ref, lnw_ref, lnb_ref, wo_ref,
                     o_ref, sbfin_ref, state_ref, uext_ref, mix_ref, *, tm):
    b = pl.program_id(0)
    i = pl.program_id(1)
    n_tiles = pl.num_programs(1)

    @pl.when(i == 0)
    def _():
        state_ref[...] = sb0_ref[0]

    for piece in _even_mix_pieces(b, i, n_tiles, x_ref, q_ref, k_ref, v_ref, gs_ref, u_ref, up_ref, un_ref,
                                  sprev_ref, dt_ref, mod_ref, cw_ref, lnw_ref, lnb_ref, wo_ref,
                                  state_ref, uext_ref, mix_ref, o_ref.at[0], tm):
        piece()

    @pl.when(i == n_tiles - 1)
    def _():
        sbfin_ref[0] = state_ref[...]


def _even_mix_call(x, q, k, v, gs, u, sprev, sb0, dtab, mod, conv_w, ln_w, ln_b, w_out, tm):
    bsz, l, d = x.shape
    n_tiles = l // tm
    cpt = tm // CHUNK
    hpt = tm // HALO
    n_halo = l // HALO
    rev = lambda i: n_tiles - 1 - i
    tok = lambda w: pl.BlockSpec((1, tm, w), lambda b, i: (b, rev(i), 0))
    state_spec = pl.BlockSpec((1, HEADS, CHUNK, CHUNK), lambda b, i: (b, 0, 0, 0))
    return pl.pallas_call(
        functools.partial(_even_mix_kernel, tm=tm),
        grid=(bsz, n_tiles),
        in_specs=[tok(d), tok(RET_W), tok(RET_W), tok(RET_W), tok(RET_W), tok(CONV_CH),
                  pl.BlockSpec((1, HALO, CONV_CH), lambda b, i: (b, jnp.maximum(rev(i) * hpt - 1, 0), 0)),
                  pl.BlockSpec((1, HALO, CONV_CH),
                               lambda b, i: (b, jnp.minimum((rev(i) + 1) * hpt, n_halo - 1), 0)),
                  pl.BlockSpec((1, cpt, HEADS, CHUNK, CHUNK), lambda b, i: (b, rev(i), 0, 0, 0)),
                  state_spec,
                  _const_spec(dtab.shape),
                  _const_spec(mod.shape),
                  _const_spec(conv_w.shape),
                  _const_spec((1, CONV_CH)),
                  _const_spec((1, CONV_CH)),
                  _const_spec(w_out.shape)],
        out_specs=[tok(d), state_spec],
        out_shape=[jax.ShapeDtypeStruct((bsz, l, d), F32),
                   jax.ShapeDtypeStruct((bsz, HEADS, CHUNK, CHUNK), F32)],
        scratch_shapes=[pltpu.VMEM((HEADS, CHUNK, CHUNK), F32),
                        pltpu.VMEM((tm + 2 * HALO, CONV_CH), F32),
                        pltpu.VMEM((tm, RET_W + CONV_CH), BF16)],
        compiler_params=_params(("arbitrary", "arbitrary")),
        name="even_mix",
    )(x, q, k, v, gs, u, u, u, sprev, sb0, dtab, mod, conv_w, ln_w, ln_b, w_out)


def _odd_kernel(x_ref, xp_ref, xn_ref, mod_ref, nw_ref, wi_ref, pw_ref, ps_ref, lnw_ref, lnb_ref,
                sgw_ref, sgb_ref, wo_ref, o_ref, h_ref, pcx_ref, mix_ref, *, tm, seq_len):
    b = pl.program_id(0)
    i = pl.program_id(1)
    n_tiles = pl.num_programs(1)
    nw = nw_ref[...]
    shift = mod_ref[b * 6 + 0]
    scale = mod_ref[b * 6 + 1]

    h_ref[0:HALO] = _rms_mod(xp_ref[0], nw, shift, scale).astype(BF16)
    h_ref[HALO:HALO + tm] = _rms_mod(x_ref[0], nw, shift, scale).astype(BF16)
    h_ref[HALO + tm:2 * HALO + tm] = _rms_mod(xn_ref[0], nw, shift, scale).astype(BF16)

    pcx_ref[0:tm + 2 * HALO] = _dot(h_ref[...], wi_ref[:, 0:POOL_CH])
    pcx_ref[0:HALO] = jnp.where(i > 0, pcx_ref[0:HALO], 0.0)
    pcx_ref[HALO + tm:2 * HALO + tm] = jnp.where(i < n_tiles - 1, pcx_ref[HALO + tm:2 * HALO + tm], 0.0)
    pos = i * tm + lax.broadcasted_iota(jnp.int32, (tm, POOL_GC), 0)
    n_ext = tm + 2 * HALO + SUBLANES
    pcx_ref[tm + 2 * HALO:n_ext] = jnp.zeros((SUBLANES, POOL_CH), F32)
    for gi, w in enumerate(POOL_WINDOWS):
        left = w // 2
        right = w - 1 - left
        sl = slice(gi * POOL_GC, (gi + 1) * POOL_GC)
        run = pcx_ref[:, sl]
        span = 1
        while span < w:
            keep = run.shape[0] - SUBLANES
            run = run[0:keep] + run[span:span + keep]
            span *= 2
        win = run[HALO - left:HALO - left + tm]
        cnt = jnp.minimum(pos + right, seq_len - 1) - jnp.maximum(pos - left, 0) + 1
        m = win / cnt.astype(F32) - pcx_ref[HALO:HALO + tm, sl]
        po = _dot(m.astype(BF16), pw_ref[gi]) * ps_ref[:, sl]
        mix_ref[:, sl] = po.astype(BF16)

    pd = _dot(h_ref[HALO:HALO + tm], wi_ref[:, POOL_CH:ODD_IN])
    z = 0.5 * pd * (1.0 + lax.erf(pd * (2.0 ** -0.5)))
    zu = z[:, 0:SG_CH]
    zv = _layer_norm(z[:, SG_CH:2 * SG_CH], lnw_ref[...], lnb_ref[...]).astype(BF16)
    for c in range(tm // CHUNK):
        rows = slice(c * CHUNK, (c + 1) * CHUNK)
        for g in range(SG_GROUPS):
            sl = slice(g * SG_GC, (g + 1) * SG_GC)
            s = _dot(sgw_ref[g], zv[rows, sl]) + sgb_ref[g]
            mix_ref[rows, POOL_CH + g * SG_GC:POOL_CH + (g + 1) * SG_GC] = (zu[rows, sl] * s).astype(BF16)

    out = _dot(mix_ref[...], wo_ref[...])
    o_ref[0] = x_ref[0] + mod_ref[b * 6 + 2] * out


def _odd_call(x, mod, nw, w_in, pool_w, pool_scale, ln_w, ln_b, sg_w, sg_b_full, w_out, tm):
    bsz, l, d = x.shape
    hpt = tm // HALO
    n_halo = l // HALO
    tok = pl.BlockSpec((1, tm, d), lambda b, i: (b, i, 0))
    return pl.pallas_call(
        functools.partial(_odd_kernel, tm=tm, seq_len=l),
        grid=(bsz, l // tm),
        in_specs=[tok,
                  pl.BlockSpec((1, HALO, d), lambda b, i: (b, jnp.maximum(i * hpt - 1, 0), 0)),
                  pl.BlockSpec((1, HALO, d), lambda b, i: (b, jnp.minimum((i + 1) * hpt, n_halo - 1), 0)),
                  _const_spec(mod.shape),
                  _const_spec((1, d)),
                  _const_spec(w_in.shape),
                  _const_spec(pool_w.shape),
                  _const_spec((1, POOL_CH)),
                  _const_spec((1, SG_CH)),
                  _const_spec((1, SG_CH)),
                  _const_spec(sg_w.shape),
                  _const_spec(sg_b_full.shape),
                  _const_spec(w_out.shape)],
        out_specs=tok,
        out_shape=jax.ShapeDtypeStruct((bsz, l, d), F32),
        scratch_shapes=[pltpu.VMEM((tm + 2 * HALO, d), BF16),
                        pltpu.VMEM((tm + 2 * HALO + SUBLANES, POOL_CH), F32),
                        pltpu.VMEM((tm, POOL_CH + SG_CH), BF16)],
        compiler_params=_params(("arbitrary", "arbitrary")),
        name="odd_mix",
    )(x, x, x, mod, nw, w_in, pool_w, pool_scale, ln_w, ln_b, sg_w, sg_b_full, w_out)


def _ffn_kernel(x_ref, mod_ref, nw_ref, wg_ref, wu_ref, wd_ref, fnw_ref, *rest, n_cast, final_norm):
    cast_in = rest[:n_cast]
    o_ref = rest[n_cast]
    cast_out = rest[n_cast + 1:2 * n_cast + 1]
    h_ref, act_ref = rest[2 * n_cast + 1:]
    for src, dst in zip(cast_in, cast_out):
        dst[...] = src[...].astype(BF16)
    tm = x_ref.shape[1]
    subs = []
    for r0 in range(0, tm, min(tm, FFN_ROWS)):
        rows = pl.ds(r0, min(tm, FFN_ROWS))
        subs.append(_ffn_pieces(x_ref.at[0, rows], o_ref.at[0, rows], pl.program_id(0), mod_ref, nw_ref,
                                wg_ref, wu_ref, wd_ref, fnw_ref, h_ref.at[rows], act_ref.at[rows], final_norm))
    order = [subs[0][0]]
    for k, pieces in enumerate(subs):
        order.append(pieces[1])
        if k + 1 < len(subs):
            order.append(subs[k + 1][0])
        order += pieces[2:]
    for piece in order:
        piece()


def _ffn_call(x, mod, nw, wg, wu, wd, fnw, tm, final_norm, cast=()):
    bsz, l, d = x.shape
    n_tiles = l // tm
    n_steps = bsz * n_tiles
    tok = pl.BlockSpec((1, tm, d), lambda b, i: (b, i, 0))

    def band_specs(stack, layer):
        _, rows, cols = stack.shape
        hold = next(h for h in range(1, n_steps + 1)
                    if n_steps % h == 0 and rows % (n_steps // h) == 0 and (rows * h // n_steps) % 16 == 0)
        band = rows * hold // n_steps
        return (pl.BlockSpec((None, band, cols), lambda b, i: (layer, (b * n_tiles + i) // hold, 0)),
                pl.BlockSpec((band, cols), lambda b, i: ((b * n_tiles + i) // hold, 0)))

    specs = [band_specs(stack, layer) for stack, layer in cast]
    outs = pl.pallas_call(
        functools.partial(_ffn_kernel, n_cast=len(cast), final_norm=final_norm),
        grid=(bsz, n_tiles),
        in_specs=[tok,
                  _const_spec(mod.shape),
                  _const_spec((1, d)),
                  _const_spec(wg.shape),
                  _const_spec(wu.shape),
                  _const_spec(wd.shape),
                  _const_spec((1, d))] + [s[0] for s in specs],
        out_specs=[tok] + [s[1] for s in specs],
        out_shape=[jax.ShapeDtypeStruct((bsz, l, d), F32)]
        + [jax.ShapeDtypeStruct(stack.shape[1:], BF16) for stack, _ in cast],
        scratch_shapes=[pltpu.VMEM((tm, d), BF16), pltpu.VMEM((tm, D_FF), BF16)],
        compiler_params=_params(("arbitrary", "arbitrary")),
        name="ffn",
    )(x, mod, nw, wg, wu, wd, fnw, *[stack for stack, _ in cast])
    return outs[0], tuple(outs[1:])


def _rope_tables(p_seq, p_row, p_col):
    parts = []
    for p, n in zip((p_seq, p_row, p_col), ROPE_PAIRS):
        freq = ROPE_BASE ** (-jnp.arange(n, dtype=F32) / n)
        parts.append(p[:, None] * freq[None, :])
    ang = jnp.concatenate(parts, axis=-1)
    cos, sin = jnp.cos(ang), jnp.sin(ang)
    return jnp.concatenate([cos, cos], axis=-1), jnp.concatenate([-sin, sin], axis=-1)


def _tile(l):
    return min(l, 1024)


def kernel(x, c, ctx, c_ctx, ada_w, ada_b, norm_w, even_w_in, even_w_out, ret_decay_logit, conv_dw_w, conv_ln_w, conv_ln_b, odd_w_in, odd_w_out, pool_w, pool_scale, sg_ln_w, sg_ln_b, sg_w, sg_b, ffn_w_gate, ffn_w_up, ffn_w_down, final_norm_w):
    bsz, l, d = x.shape
    lc = ctx.shape[1]
    assert d == D_MODEL and l % 256 == 0 and lc % CHUNK == 0 and bsz + 1 <= MOD_ROWS
    rows = l // GRID_W
    grid_r = jnp.broadcast_to(jnp.arange(rows, dtype=F32)[:, None], (rows, GRID_W)).reshape(-1)
    grid_c = jnp.broadcast_to(jnp.arange(GRID_W, dtype=F32)[None, :], (rows, GRID_W)).reshape(-1)
    cos_x, sin_x = _rope_tables(jnp.full((l,), lc, F32), grid_r, grid_c)
    zeros_c = jnp.zeros((lc,), F32)
    cos_c, sin_c = _rope_tables(jnp.arange(lc, dtype=F32), zeros_c, zeros_c)

    cvec = jnp.concatenate([c, c_ctx[None, :], jnp.zeros((MOD_ROWS - bsz - 1, d), F32)], axis=0)
    mod_all = _mod_call(cvec, ada_w, ada_b)
    dtabs = _decay_call(ret_decay_logit)
    zero_state = jnp.zeros((bsz, HEADS, CHUNK, CHUNK), F32)
    fnw = final_norm_w.reshape(1, d)
    tx, tc = _tile(l), _tile(lc)

    def layer_matrices(i):
        mix_in, mix_out = (even_w_in, even_w_out) if i % 2 == 0 else (odd_w_in, odd_w_out)
        return (mix_in, i // 2), (mix_out, i // 2), (ffn_w_gate, i), (ffn_w_up, i), (ffn_w_down, i)

    bf16_mats = tuple(stack[layer].astype(BF16) for stack, layer in layer_matrices(0))
    for i in range(DEPTH):
        j = i // 2
        even = i % 2 == 0
        ctx_after = any(m % 2 == 0 for m in range(i + 1, DEPTH))
        mod_x = mod_all[i, :bsz].reshape(bsz * 6, 1, d)
        mod_c = jnp.broadcast_to(mod_all[i, bsz].reshape(1, 6, 1, d), (bsz, 6, 1, d)).reshape(bsz * 6, 1, d)
        nw1 = norm_w[i, 0].reshape(1, d)
        nw2 = norm_w[i, 1].reshape(1, d)
        w_in, w_out, wg, wu, wd = bf16_mats
        if even:
            conv_w = conv_dw_w[j]
            lnw = conv_ln_w[j].reshape(1, CONV_CH)
            lnb = conv_ln_b[j].reshape(1, CONV_CH)
            dtab = dtabs[j]
            qc, kc, vc, gc, uc, spc, s_f = _even_in_call(ctx, mod_c, nw1, w_in, cos_c, sin_c, zero_state, dtab, tc)
            yc, s_b = _even_mix_call(ctx, qc, kc, vc, gc, uc, spc, zero_state, dtab, mod_c,
                                     conv_w, lnw, lnb, w_out, tc)
            qx, kx, vx, gx, ux, spx, _ = _even_in_call(x, mod_x, nw1, w_in, cos_x, sin_x, s_f, dtab, tx)
            x, _ = _even_mix_call(x, qx, kx, vx, gx, ux, spx, s_b, dtab, mod_x, conv_w, lnw, lnb, w_out, tx)
            if ctx_after:
                ctx = yc
        else:
            args = (nw1, w_in, pool_w[j].astype(BF16), pool_scale[j].reshape(1, POOL_CH),
                    sg_ln_w[j].reshape(1, SG_CH), sg_ln_b[j].reshape(1, SG_CH), sg_w[j].astype(BF16),
                    jnp.broadcast_to(sg_b[j][:, :, None], (SG_GROUPS, CHUNK, SG_GC)), w_out)
            if ctx_after:
                ctx = _odd_call(ctx, mod_c, *args, tc)
            x = _odd_call(x, mod_x, *args, tx)
        if ctx_after:
            ctx, _ = _ffn_call(ctx, mod_c, nw2, wg, wu, wd, fnw, tc, False)
        later = layer_matrices(i + 1) if i + 1 < DEPTH else ()
        x, bf16_mats = _ffn_call(x, mod_x, nw2, wg, wu, wd, fnw, min(l, 2 * FFN_ROWS), i == DEPTH - 1, later)
    return x
```

```python
import functools

import jax
import jax.numpy as jnp
from jax import lax
from jax.experimental import pallas as pl
from jax.experimental.pallas import tpu as pltpu

F32 = jnp.float32
BF16 = jnp.bfloat16

D_MODEL = 1024
DEPTH = 4
GRID_W = 64
EPS = 1e-6
HEADS = 4
HEAD_DIM = 128
RET_W = HEADS * HEAD_DIM
CHUNK = 128
ROPE_BASE = 10000.0
ROPE_PAIRS = (HEAD_DIM // 8, 3 * HEAD_DIM // 16, 3 * HEAD_DIM // 16)
CONV_CH = 512
CONV_K = 31
CONV_PAD = CONV_K // 2
EVEN_IN = 4 * RET_W + 2 * CONV_CH
POOL_CH = 512
POOL_WINDOWS = (2, 4, 8, 16)
POOL_GC = 128
SG_CH = 512
SG_GROUPS = 4
SG_GC = 128
ODD_IN = POOL_CH + 2 * SG_CH
D_FF = 2816
SUBLANES = 8
LANES = 128
FF_BLOCK = 256
OUT_BLOCK = 256
CONV_ROWS = 128
FFN_ROWS = 512
HALO = 16
N_DECAY_TABLES = 7
MOD_ROWS = 8
VMEM_LIMIT = 56 * 1024 * 1024


def _dot(a, b):
    return jnp.dot(a, b, preferred_element_type=F32)


def _dot_nt(a, b):
    return lax.dot_general(a, b, (((1,), (1,)), ((), ())), preferred_element_type=F32)


def _dot_tn(a, b):
    return lax.dot_general(a, b, (((0,), (0,)), ((), ())), preferred_element_type=F32)


def _silu(x):
    return x * jax.nn.sigmoid(x)


def _rms_mod(x32, nw, shift, scale):
    y = x32 * lax.rsqrt(jnp.mean(x32 * x32, axis=-1, keepdims=True) + EPS) * nw
    return y * (1.0 + scale) + shift


def _layer_norm(x32, w, b):
    mu = jnp.mean(x32, axis=-1, keepdims=True)
    xc = x32 - mu
    return xc * lax.rsqrt(jnp.mean(xc * xc, axis=-1, keepdims=True) + EPS) * w + b


def _const_spec(shape):
    return pl.BlockSpec(shape, lambda *_: (0,) * len(shape), pipeline_mode=pl.Buffered(1))


def _params(sem):
    return pltpu.CompilerParams(dimension_semantics=sem, vmem_limit_bytes=VMEM_LIMIT)


def _cast_band_specs(stack, layer, bsz, n_tiles):
    _, rows, cols = stack.shape
    n_steps = bsz * n_tiles
    hold = next(h for h in range(1, n_steps + 1)
                if n_steps % h == 0 and rows % (n_steps // h) == 0 and (rows * h // n_steps) % 16 == 0)
    band = rows * hold // n_steps
    return (pl.BlockSpec((None, band, cols), lambda b, i: (layer, (b * n_tiles + i) // hold, 0)),
            pl.BlockSpec((band, cols), lambda b, i: ((b * n_tiles + i) // hold, 0)))


def _cast_bands(srcs, dsts):
    for src, dst in zip(srcs, dsts):
        dst[...] = src[...].astype(BF16)


def _mod_kernel(c_ref, w_ref, b_ref, o_ref):
    s = _silu(c_ref[...]).astype(BF16)
    o_ref[0] = _dot(s, w_ref[0].astype(BF16)) + b_ref[0]


def _mod_call(cvec, ada_w, ada_b):
    depth, d, n = ada_w.shape
    tn = 1024
    return pl.pallas_call(
        _mod_kernel,
        grid=(depth, n // tn),
        in_specs=[_const_spec((MOD_ROWS, d)),
                  pl.BlockSpec((1, d, tn), lambda i, j: (i, 0, j)),
                  pl.BlockSpec((1, 1, tn), lambda i, j: (i, 0, j))],
        out_specs=pl.BlockSpec((1, MOD_ROWS, tn), lambda i, j: (i, 0, j)),
        out_shape=jax.ShapeDtypeStruct((depth, MOD_ROWS, n), F32),
        compiler_params=_params(("arbitrary", "arbitrary")),
        name="adaln_mod",
    )(cvec, ada_w, ada_b.reshape(depth, 1, n))


def _log_sigmoid(x):
    return -(jnp.maximum(-x, 0.0) + jnp.log1p(jnp.exp(-jnp.abs(x))))


def _decay_kernel(logit_ref, o_ref):
    j = pl.program_id(0)
    row = lax.broadcasted_iota(jnp.int32, (CHUNK, CHUNK), 0).astype(F32)
    col = lax.broadcasted_iota(jnp.int32, (CHUNK, CHUNK), 1).astype(F32)
    diff = row - col
    lower = diff >= 0.0
    for hd in range(HEADS):
        lgf = _log_sigmoid(jnp.full((CHUNK, CHUNK), logit_ref[(j * 2 + 0) * HEADS + hd], F32))
        lgb = _log_sigmoid(jnp.full((CHUNK, CHUNK), logit_ref[(j * 2 + 1) * HEADS + hd], F32))
        o_ref[0, 0, hd] = jnp.where(lower, jnp.exp(lgf * jnp.where(lower, diff, 0.0)),
                                    jnp.exp(lgb * jnp.where(lower, 0.0, -diff - 1.0)))
        o_ref[0, 1, hd] = jnp.exp(lgf * (CHUNK - 1.0 - row))
        o_ref[0, 2, hd] = jnp.exp(lgb * row)
        o_ref[0, 3, hd] = jnp.exp(lgf * (row + 1.0))
        o_ref[0, 4, hd] = jnp.exp(lgb * (CHUNK - 1.0 - row))
        o_ref[0, 5, hd] = jnp.exp(lgf * float(CHUNK))
        o_ref[0, 6, hd] = jnp.exp(lgb * float(CHUNK))


def _decay_call(ret_decay_logit):
    n_even = ret_decay_logit.shape[0]
    return pl.pallas_call(
        _decay_kernel,
        grid=(n_even,),
        in_specs=[pl.BlockSpec(memory_space=pltpu.SMEM)],
        out_specs=pl.BlockSpec((1, N_DECAY_TABLES, HEADS, CHUNK, CHUNK), lambda j: (j, 0, 0, 0, 0)),
        out_shape=jax.ShapeDtypeStruct((n_even, N_DECAY_TABLES, HEADS, CHUNK, CHUNK), F32),
        compiler_params=_params(("arbitrary",)),
        name="ret_decay",
    )(ret_decay_logit.reshape(-1).astype(F32))


def _even_in_kernel(x_ref, mod_ref, nw_ref, w_ref, cos_ref, sin_ref, s0_ref, dt_ref, *rest, n_cast, tm):
    cast_in = rest[:n_cast]
    q_ref, k_ref, v_ref, gs_ref, u_ref, sprev_ref, sfin_ref = rest[n_cast:n_cast + 7]
    cast_out = rest[n_cast + 7:2 * n_cast + 7]
    state_ref = rest[2 * n_cast + 7]
    b = pl.program_id(0)
    i = pl.program_id(1)
    _cast_bands(cast_in, cast_out)

    @pl.when(i == 0)
    def _():
        state_ref[...] = s0_ref[0]

    h = _rms_mod(x_ref[0], nw_ref[...], mod_ref[b * 6 + 0], mod_ref[b * 6 + 1]).astype(BF16)
    cosf = cos_ref[...]
    sinf = sin_ref[...]

    def rope(t):
        return t * cosf + pltpu.roll(t, HEAD_DIM // 2, 1) * sinf

    pq = _dot(h, w_ref[:, 0:RET_W])
    pk = _dot(h, w_ref[:, RET_W:2 * RET_W])
    pv = _dot(h, w_ref[:, 2 * RET_W:3 * RET_W])
    scale = HEAD_DIM ** -0.5
    for hd in range(HEADS):
        sl = slice(hd * HEAD_DIM, (hd + 1) * HEAD_DIM)
        q_ref[0, :, sl] = (rope(pq[:, sl]) * scale).astype(BF16)
        kr = rope(pk[:, sl]).astype(BF16)
        k_ref[0, :, sl] = kr
        vh = pv[:, sl]
        v_ref[0, :, sl] = vh.astype(BF16)
        zeta = dt_ref[1, hd]
        cdec = dt_ref[5, hd]
        for c in range(tm // CHUNK):
            rows = slice(c * CHUNK, (c + 1) * CHUNK)
            s_cur = state_ref[hd]
            sprev_ref[0, c, hd] = s_cur.astype(BF16)
            wv = (vh[rows] * zeta).astype(BF16)
            state_ref[hd] = s_cur * cdec + _dot_tn(kr[rows], wv)

    pg = _dot(h, w_ref[:, 3 * RET_W:4 * RET_W])
    gs_ref[0] = _silu(pg).astype(BF16)
    pa = _dot(h, w_ref[:, 4 * RET_W:4 * RET_W + CONV_CH])
    pb = _dot(h, w_ref[:, 4 * RET_W + CONV_CH:EVEN_IN])
    u_ref[0] = (pa * jax.nn.sigmoid(pb)).astype(BF16)

    @pl.when(i == pl.num_programs(1) - 1)
    def _():
        sfin_ref[0] = state_ref[...]


def _even_in_call(x, mod, nw, w_in, cosf, sinf, s0, dtab, tm, cast=()):
    bsz, l, d = x.shape
    n_chunks = l // CHUNK
    cpt = tm // CHUNK
    tok = lambda w: pl.BlockSpec((1, tm, w), lambda b, i: (b, i, 0))
    state_spec = pl.BlockSpec((1, HEADS, CHUNK, CHUNK), lambda b, i: (b, 0, 0, 0))
    specs = [_cast_band_specs(stack, layer, bsz, l // tm) for stack, layer in cast]
    outs = pl.pallas_call(
        functools.partial(_even_in_kernel, n_cast=len(cast), tm=tm),
        grid=(bsz, l // tm),
        in_specs=[tok(d),
                  _const_spec(mod.shape),
                  _const_spec((1, d)),
                  _const_spec(w_in.shape),
                  pl.BlockSpec((tm, HEAD_DIM), lambda b, i: (i, 0)),
                  pl.BlockSpec((tm, HEAD_DIM), lambda b, i: (i, 0)),
                  state_spec,
                  _const_spec(dtab.shape)] + [s[0] for s in specs],
        out_specs=[tok(RET_W), tok(RET_W), tok(RET_W), tok(RET_W), tok(CONV_CH),
                   pl.BlockSpec((1, cpt, HEADS, CHUNK, CHUNK), lambda b, i: (b, i, 0, 0, 0)),
                   state_spec] + [s[1] for s in specs],
        out_shape=[jax.ShapeDtypeStruct((bsz, l, RET_W), BF16),
                   jax.ShapeDtypeStruct((bsz, l, RET_W), BF16),
                   jax.ShapeDtypeStruct((bsz, l, RET_W), BF16),
                   jax.ShapeDtypeStruct((bsz, l, RET_W), BF16),
                   jax.ShapeDtypeStruct((bsz, l, CONV_CH), BF16),
                   jax.ShapeDtypeStruct((bsz, n_chunks, HEADS, CHUNK, CHUNK), BF16),
                   jax.ShapeDtypeStruct((bsz, HEADS, CHUNK, CHUNK), F32)]
        + [jax.ShapeDtypeStruct(stack.shape[1:], BF16) for stack, _ in cast],
        scratch_shapes=[pltpu.VMEM((HEADS, CHUNK, CHUNK), F32)],
        compiler_params=_params(("arbitrary", "arbitrary")),
        name="even_in",
    )(x, mod, nw, w_in, cosf, sinf, s0, dtab, *[stack for stack, _ in cast])
    return (*outs[:7], tuple(outs[7:]))


def _even_mix_pieces(b, i, n_tiles, x_ref, q_ref, k_ref, v_ref, gs_ref, u_ref, up_ref, un_ref, sprev_ref,
                     dt_ref, mod_ref, cw_ref, lnw_ref, lnb_ref, wo_ref, state_ref, uext_ref, mix_ref, xo_ref, tm):
    def retention(c, hd):
        rows = slice(c * CHUNK, (c + 1) * CHUNK)
        sl = slice(hd * HEAD_DIM, (hd + 1) * HEAD_DIM)
        qc = q_ref[0, rows, sl]
        kc = k_ref[0, rows, sl]
        vc = v_ref[0, rows, sl]
        s_b = state_ref[hd]
        p = (_dot_nt(qc, kc) * dt_ref[0, hd]).astype(BF16)
        y = _dot(p, vc)
        y = y + _dot(qc, sprev_ref[0, c, hd]) * dt_ref[3, hd]
        y = y + _dot(qc, s_b.astype(BF16)) * dt_ref[4, hd]
        y = y * lax.rsqrt(jnp.mean(y * y, axis=-1, keepdims=True) + EPS)
        mix_ref[rows, sl] = gs_ref[0, rows, sl] * y.astype(BF16)
        wv = (vc.astype(F32) * dt_ref[2, hd]).astype(BF16)
        state_ref[hd] = s_b * dt_ref[6, hd] + _dot_tn(kc, wv)

    def halo():
        uext_ref[0:HALO] = jnp.where(i < n_tiles - 1, up_ref[0].astype(F32), 0.0)
        uext_ref[HALO:HALO + tm] = u_ref[0].astype(F32)
        uext_ref[HALO + tm:2 * HALO + tm] = jnp.where(i > 0, un_ref[0].astype(F32), 0.0)

    def conv(r):
        blocks = []
        for cb in range(CONV_CH // LANES):
            cols = slice(cb * LANES, (cb + 1) * LANES)
            acc = None
            for sub in range(SUBLANES):
                part = None
                for t in range(sub, CONV_K, SUBLANES):
                    start = r * CONV_ROWS + t - sub
                    term = uext_ref[start:start + CONV_ROWS + SUBLANES, cols] * cw_ref[t:t + 1, cols]
                    part = term if part is None else part + term
                shifted = part[sub + 1:sub + 1 + CONV_ROWS]
                acc = shifted if acc is None else acc + shifted
            blocks.append(acc)
        cv = _silu(_layer_norm(jnp.concatenate(blocks, axis=-1), lnw_ref[...], lnb_ref[...]))
        mix_ref[r * CONV_ROWS:(r + 1) * CONV_ROWS, RET_W:RET_W + CONV_CH] = cv.astype(BF16)

    def out_proj(n):
        cols = slice(n * OUT_BLOCK, (n + 1) * OUT_BLOCK)
        out = _dot(mix_ref[...], wo_ref[:, cols])
        xo_ref[:, cols] = x_ref[0, :, cols] + mod_ref[b * 6 + 2][:, cols] * out

    pieces = [halo]
    convs = [functools.partial(conv, r) for r in range(tm // CONV_ROWS)]
    rets = [functools.partial(retention, c, hd) for c in reversed(range(tm // CHUNK)) for hd in range(HEADS)]
    per = len(rets) // len(convs)
    for r, cv in enumerate(convs):
        pieces += rets[r * per:(r + 1) * per] + [cv]
    pieces += rets[len(convs) * per:]
    pieces += [functools.partial(out_proj, n) for n in range(D_MODEL // OUT_BLOCK)]
    return pieces


def _ffn_pieces(xs_ref, o_ref, b, mod_ref, nw_ref, wg_ref, wu_ref, wd_ref, fnw_ref, h_ref, act_ref, final_norm):
    def prologue():
        x32 = xs_ref[...]
        o_ref[...] = x32
        h_ref[...] = _rms_mod(x32, nw_ref[...], mod_ref[b * 6 + 3], mod_ref[b * 6 + 4]).astype(BF16)

    def gate_up(j):
        cols = slice(j * FF_BLOCK, (j + 1) * FF_BLOCK)
        h = h_ref[...]
        act_ref[:, cols] = (_silu(_dot(h, wg_ref[:, cols])) * _dot(h, wu_ref[:, cols])).astype(BF16)

    def down(n):
        cols = slice(n * OUT_BLOCK, (n + 1) * OUT_BLOCK)
        o_ref[:, cols] = o_ref[:, cols] + mod_ref[b * 6 + 5][:, cols] * _dot(act_ref[...], wd_ref[:, cols])

    def final():
        y = o_ref[...]
        o_ref[...] = y * lax.rsqrt(jnp.mean(y * y, axis=-1, keepdims=True) + EPS) * fnw_ref[...]

    pieces = [prologue]
    pieces += [functools.partial(gate_up, j) for j in range(D_FF // FF_BLOCK)]
    pieces += [functools.partial(down, n) for n in range(D_MODEL // OUT_BLOCK)]
    if final_norm:
        pieces.append(final)
    return pieces


def _even_mix_kernel(x_ref, q_ref, k_ref, v_ref, gs_ref, u_ref, up_ref, un_ref, sprev_ref, sb0_ref,
                     dt_ref, mod_ref, cw_ref, lnw_ref, lnb_ref, wo_ref,
                     o_ref, sbfin_ref, state_ref, uext_ref, mix_ref, *, tm):
    b = pl.program_id(0)
    i = pl.program_id(1)
    n_tiles = pl.num_programs(1)

    @pl.when(i == 0)
    def _():
        state_ref[...] = sb0_ref[0]

    for piece in _even_mix_pieces(b, i, n_tiles, x_ref, q_ref, k_ref, v_ref, gs_ref, u_ref, up_ref, un_ref,
                                  sprev_ref, dt_ref, mod_ref, cw_ref, lnw_ref, lnb_ref, wo_ref,
                                  state_ref, uext_ref, mix_ref, o_ref.at[0], tm):
        piece()

    @pl.when(i == n_tiles - 1)
    def _():
        sbfin_ref[0] = state_ref[...]


def _even_mix_call(x, q, k, v, gs, u, sprev, sb0, dtab, mod, conv_w, ln_w, ln_b, w_out, tm):
    bsz, l, d = x.shape
    n_tiles = l // tm
    cpt = tm // CHUNK
    hpt = tm // HALO
    n_halo = l // HALO
    rev = lambda i: n_tiles - 1 - i
    tok = lambda w: pl.BlockSpec((1, tm, w), lambda b, i: (b, rev(i), 0))
    state_spec = pl.BlockSpec((1, HEADS, CHUNK, CHUNK), lambda b, i: (b, 0, 0, 0))
    return pl.pallas_call(
        functools.partial(_even_mix_kernel, tm=tm),
        grid=(bsz, n_tiles),
        in_specs=[tok(d), tok(RET_W), tok(RET_W), tok(RET_W), tok(RET_W), tok(CONV_CH),
                  pl.BlockSpec((1, HALO, CONV_CH), lambda b, i: (b, jnp.maximum(rev(i) * hpt - 1, 0), 0)),
                  pl.BlockSpec((1, HALO, CONV_CH),
                               lambda b, i: (b, jnp.minimum((rev(i) + 1) * hpt, n_halo - 1), 0)),
                  pl.BlockSpec((1, cpt, HEADS, CHUNK, CHUNK), lambda b, i: (b, rev(i), 0, 0, 0)),
                  state_spec,
                  _const_spec(dtab.shape),
                  _const_spec(mod.shape),
                  _const_spec(conv_w.shape),
                  _const_spec((1, CONV_CH)),
                  _const_spec((1, CONV_CH)),
                  _const_spec(w_out.shape)],
        out_specs=[tok(d), state_spec],
        out_shape=[jax.ShapeDtypeStruct((bsz, l, d), F32),
                   jax.ShapeDtypeStruct((bsz, HEADS, CHUNK, CHUNK), F32)],
        scratch_shapes=[pltpu.VMEM((HEADS, CHUNK, CHUNK), F32),
                        pltpu.VMEM((tm + 2 * HALO, CONV_CH), F32),
                        pltpu.VMEM((tm, RET_W + CONV_CH), BF16)],
        compiler_params=_params(("arbitrary", "arbitrary")),
        name="even_mix",
    )(x, q, k, v, gs, u, u, u, sprev, sb0, dtab, mod, conv_w, ln_w, ln_b, w_out)


def _odd_kernel(x_ref, xp_ref, xn_ref, mod_ref, nw_ref, wi_ref, pw_ref, ps_ref, lnw_ref, lnb_ref,
                sgw_ref, sgb_ref, wo_ref, o_ref, h_ref, pcx_ref, mix_ref, *, tm, seq_len):
    b = pl.program_id(0)
    i = pl.program_id(1)
    n_tiles = pl.num_programs(1)
    nw = nw_ref[...]
    shift = mod_ref[b * 6 + 0]
    scale = mod_ref[b * 6 + 1]

    h_ref[0:HALO] = _rms_mod(xp_ref[0], nw, shift, scale).astype(BF16)
    h_ref[HALO:HALO + tm] = _rms_mod(x_ref[0], nw, shift, scale).astype(BF16)
    h_ref[HALO + tm:2 * HALO + tm] = _rms_mod(xn_ref[0], nw, shift, scale).astype(BF16)

    pcx_ref[0:tm + 2 * HALO] = _dot(h_ref[...], wi_ref[:, 0:POOL_CH])
    pcx_ref[0:HALO] = jnp.where(i > 0, pcx_ref[0:HALO], 0.0)
    pcx_ref[HALO + tm:2 * HALO + tm] = jnp.where(i < n_tiles - 1, pcx_ref[HALO + tm:2 * HALO + tm], 0.0)
    pos = i * tm + lax.broadcasted_iota(jnp.int32, (tm, POOL_GC), 0)
    n_ext = tm + 2 * HALO + SUBLANES
    pcx_ref[tm + 2 * HALO:n_ext] = jnp.zeros((SUBLANES, POOL_CH), F32)
    for gi, w in enumerate(POOL_WINDOWS):
        left = w // 2
        right = w - 1 - left
        sl = slice(gi * POOL_GC, (gi + 1) * POOL_GC)
        run = pcx_ref[:, sl]
        span = 1
        while span < w:
            keep = run.shape[0] - SUBLANES
            run = run[0:keep] + run[span:span + keep]
            span *= 2
        win = run[HALO - left:HALO - left + tm]
        cnt = jnp.minimum(pos + right, seq_len - 1) - jnp.maximum(pos - left, 0) + 1
        m = win / cnt.astype(F32) - pcx_ref[HALO:HALO + tm, sl]
        po = _dot(m.astype(BF16), pw_ref[gi]) * ps_ref[:, sl]
        mix_ref[:, sl] = po.astype(BF16)

    pd = _dot(h_ref[HALO:HALO + tm], wi_ref[:, POOL_CH:ODD_IN])
    z = 0.5 * pd * (1.0 + lax.erf(pd * (2.0 ** -0.5)))
    zu = z[:, 0:SG_CH]
    zv = _layer_norm(z[:, SG_CH:2 * SG_CH], lnw_ref[...], lnb_ref[...]).astype(BF16)
    for c in range(tm // CHUNK):
        rows = slice(c * CHUNK, (c + 1) * CHUNK)
        for g in range(SG_GROUPS):
            sl = slice(g * SG_GC, (g + 1) * SG_GC)
            s = _dot(sgw_ref[g], zv[rows, sl]) + sgb_ref[g]
            mix_ref[rows, POOL_CH + g * SG_GC:POOL_CH + (g + 1) * SG_GC] = (zu[rows, sl] * s).astype(BF16)

    out = _dot(mix_ref[...], wo_ref[...])
    o_ref[0] = x_ref[0] + mod_ref[b * 6 + 2] * out


def _odd_call(x, mod, nw, w_in, pool_w, pool_scale, ln_w, ln_b, sg_w, sg_b_full, w_out, tm):
    bsz, l, d = x.shape
    hpt = tm // HALO
    n_halo = l // HALO
    tok = pl.BlockSpec((1, tm, d), lambda b, i: (b, i, 0))
    return pl.pallas_call(
        functools.partial(_odd_kernel, tm=tm, seq_len=l),
        grid=(bsz, l // tm),
        in_specs=[tok,
                  pl.BlockSpec((1, HALO, d), lambda b, i: (b, jnp.maximum(i * hpt - 1, 0), 0)),
                  pl.BlockSpec((1, HALO, d), lambda b, i: (b, jnp.minimum((i + 1) * hpt, n_halo - 1), 0)),
                  _const_spec(mod.shape),
                  _const_spec((1, d)),
                  _const_spec(w_in.shape),
                  _const_spec(pool_w.shape),
                  _const_spec((1, POOL_CH)),
                  _const_spec((1, SG_CH)),
                  _const_spec((1, SG_CH)),
                  _const_spec(sg_w.shape),
                  _const_spec(sg_b_full.shape),
                  _const_spec(w_out.shape)],
        out_specs=tok,
        out_shape=jax.ShapeDtypeStruct((bsz, l, d), F32),
        scratch_shapes=[pltpu.VMEM((tm + 2 * HALO, d), BF16),
                        pltpu.VMEM((tm + 2 * HALO + SUBLANES, POOL_CH), F32),
                        pltpu.VMEM((tm, POOL_CH + SG_CH), BF16)],
        compiler_params=_params(("arbitrary", "arbitrary")),
        name="odd_mix",
    )(x, x, x, mod, nw, w_in, pool_w, pool_scale, ln_w, ln_b, sg_w, sg_b_full, w_out)


def _ffn_kernel(x_ref, mod_ref, nw_ref, wg_ref, wu_ref, wd_ref, fnw_ref, *rest, n_cast, final_norm):
    cast_in = rest[:n_cast]
    o_ref = rest[n_cast]
    cast_out = rest[n_cast + 1:2 * n_cast + 1]
    h_ref, act_ref = rest[2 * n_cast + 1:]
    _cast_bands(cast_in, cast_out)
    tm = x_ref.shape[1]
    subs = []
    for r0 in range(0, tm, min(tm, FFN_ROWS)):
        rows = pl.ds(r0, min(tm, FFN_ROWS))
        subs.append(_ffn_pieces(x_ref.at[0, rows], o_ref.at[0, rows], pl.program_id(0), mod_ref, nw_ref,
                                wg_ref, wu_ref, wd_ref, fnw_ref, h_ref.at[rows], act_ref.at[rows], final_norm))
    order = [subs[0][0]]
    for k, pieces in enumerate(subs):
        order.append(pieces[1])
        if k + 1 < len(subs):
            order.append(subs[k + 1][0])
        order += pieces[2:]
    for piece in order:
        piece()


def _ffn_call(x, mod, nw, wg, wu, wd, fnw, tm, final_norm, cast=()):
    bsz, l, d = x.shape
    n_tiles = l // tm
    tok = pl.BlockSpec((1, tm, d), lambda b, i: (b, i, 0))
    specs = [_cast_band_specs(stack, layer, bsz, n_tiles) for stack, layer in cast]
    outs = pl.pallas_call(
        functools.partial(_ffn_kernel, n_cast=len(cast), final_norm=final_norm),
        grid=(bsz, n_tiles),
        in_specs=[tok,
                  _const_spec(mod.shape),
                  _const_spec((1, d)),
                  _const_spec(wg.shape),
                  _const_spec(wu.shape),
                  _const_spec(wd.shape),
                  _const_spec((1, d))] + [s[0] for s in specs],
        out_specs=[tok] + [s[1] for s in specs],
        out_shape=[jax.ShapeDtypeStruct((bsz, l, d), F32)]
        + [jax.ShapeDtypeStruct(stack.shape[1:], BF16) for stack, _ in cast],
        scratch_shapes=[pltpu.VMEM((tm, d), BF16), pltpu.VMEM((tm, D_FF), BF16)],
        compiler_params=_params(("arbitrary", "arbitrary")),
        name="ffn",
    )(x, mod, nw, wg, wu, wd, fnw, *[stack for stack, _ in cast])
    return outs[0], tuple(outs[1:])


def _rope_tables(p_seq, p_row, p_col):
    parts = []
    for p, n in zip((p_seq, p_row, p_col), ROPE_PAIRS):
        freq = ROPE_BASE ** (-jnp.arange(n, dtype=F32) / n)
        parts.append(p[:, None] * freq[None, :])
    ang = jnp.concatenate(parts, axis=-1)
    cos, sin = jnp.cos(ang), jnp.sin(ang)
    return jnp.concatenate([cos, cos], axis=-1), jnp.concatenate([-sin, sin], axis=-1)


def _tile(l):
    return min(l, 1024)


def kernel(x, c, ctx, c_ctx, ada_w, ada_b, norm_w, even_w_in, even_w_out, ret_decay_logit, conv_dw_w, conv_ln_w, conv_ln_b, odd_w_in, odd_w_out, pool_w, pool_scale, sg_ln_w, sg_ln_b, sg_w, sg_b, ffn_w_gate, ffn_w_up, ffn_w_down, final_norm_w):
    bsz, l, d = x.shape
    lc = ctx.shape[1]
    assert d == D_MODEL and l % 256 == 0 and lc % CHUNK == 0 and bsz + 1 <= MOD_ROWS
    rows = l // GRID_W
    grid_r = jnp.broadcast_to(jnp.arange(rows, dtype=F32)[:, None], (rows, GRID_W)).reshape(-1)
    grid_c = jnp.broadcast_to(jnp.arange(GRID_W, dtype=F32)[None, :], (rows, GRID_W)).reshape(-1)
    cos_x, sin_x = _rope_tables(jnp.full((l,), lc, F32), grid_r, grid_c)
    zeros_c = jnp.zeros((lc,), F32)
    cos_c, sin_c = _rope_tables(jnp.arange(lc, dtype=F32), zeros_c, zeros_c)

    cvec = jnp.concatenate([c, c_ctx[None, :], jnp.zeros((MOD_ROWS - bsz - 1, d), F32)], axis=0)
    mod_all = _mod_call(cvec, ada_w, ada_b)
    dtabs = _decay_call(ret_decay_logit)
    zero_state = jnp.zeros((bsz, HEADS, CHUNK, CHUNK), F32)
    fnw = final_norm_w.reshape(1, d)
    tx, tc = _tile(l), _tile(lc)

    def layer_matrices(i):
        mix_in, mix_out = (even_w_in, even_w_out) if i % 2 == 0 else (odd_w_in, odd_w_out)
        return (mix_in, i // 2), (mix_out, i // 2), (ffn_w_gate, i), (ffn_w_up, i), (ffn_w_down, i)

    first = layer_matrices(0)
    bf16_mats = (first[0][0][first[0][1]].astype(BF16),) + (None,) * (len(first) - 1)
    for i in range(DEPTH):
        j = i // 2
        even = i % 2 == 0
        ctx_after = any(m % 2 == 0 for m in range(i + 1, DEPTH))
        mod_x = mod_all[i, :bsz].reshape(bsz * 6, 1, d)
        mod_c = jnp.broadcast_to(mod_all[i, bsz].reshape(1, 6, 1, d), (bsz, 6, 1, d)).reshape(bsz * 6, 1, d)
        nw1 = norm_w[i, 0].reshape(1, d)
        nw2 = norm_w[i, 1].reshape(1, d)
        w_in, w_out, wg, wu, wd = bf16_mats
        if even:
            conv_w = conv_dw_w[j]
            lnw = conv_ln_w[j].reshape(1, CONV_CH)
            lnb = conv_ln_b[j].reshape(1, CONV_CH)
            dtab = dtabs[j]
            qc, kc, vc, gc, uc, spc, s_f, _ = _even_in_call(ctx, mod_c, nw1, w_in, cos_c, sin_c, zero_state, dtab, tc)
            pending = layer_matrices(i)[1:] if w_out is None else ()
            qx, kx, vx, gx, ux, spx, _, cast_now = _even_in_call(x, mod_x, nw1, w_in, cos_x, sin_x, s_f, dtab, tx,
                                                                 pending)
            if pending:
                w_out, wg, wu, wd = cast_now
            yc, s_b = _even_mix_call(ctx, qc, kc, vc, gc, uc, spc, zero_state, dtab, mod_c,
                                     conv_w, lnw, lnb, w_out, tc)
            x, _ = _even_mix_call(x, qx, kx, vx, gx, ux, spx, s_b, dtab, mod_x, conv_w, lnw, lnb, w_out, tx)
            if ctx_after:
                ctx = yc
        else:
            args = (nw1, w_in, pool_w[j].astype(BF16), pool_scale[j].reshape(1, POOL_CH),
                    sg_ln_w[j].reshape(1, SG_CH), sg_ln_b[j].reshape(1, SG_CH), sg_w[j].astype(BF16),
                    jnp.broadcast_to(sg_b[j][:, :, None], (SG_GROUPS, CHUNK, SG_GC)), w_out)
            if ctx_after:
                ctx = _odd_call(ctx, mod_c, *args, tc)
            x = _odd_call(x, mod_x, *args, tx)
        if ctx_after:
            ctx, _ = _ffn_call(ctx, mod_c, nw2, wg, wu, wd, fnw, tc, False)
        later = layer_matrices(i + 1) if i + 1 < DEPTH else ()
        x, bf16_mats = _ffn_call(x, mod_x, nw2, wg, wu, wd, fnw, min(l, 2 * FFN_ROWS), i == DEPTH - 1, later)
    return x
```

```python
import functools

import jax
import jax.numpy as jnp
from jax import lax
from jax.experimental import pallas as pl
from jax.experimental.pallas import tpu as pltpu

F32 = jnp.float32
BF16 = jnp.bfloat16

D_MODEL = 1024
DEPTH = 4
GRID_W = 64
EPS = 1e-6
HEADS = 4
HEAD_DIM = 128
RET_W = HEADS * HEAD_DIM
CHUNK = 128
ROPE_BASE = 10000.0
ROPE_PAIRS = (HEAD_DIM // 8, 3 * HEAD_DIM // 16, 3 * HEAD_DIM // 16)
CONV_CH = 512
CONV_K = 31
CONV_PAD = CONV_K // 2
EVEN_IN = 4 * RET_W + 2 * CONV_CH
POOL_CH = 512
POOL_WINDOWS = (2, 4, 8, 16)
POOL_GC = 128
SG_CH = 512
SG_GROUPS = 4
SG_GC = 128
ODD_IN = POOL_CH + 2 * SG_CH
D_FF = 2816
SUBLANES = 8
LANES = 128
FF_BLOCK = 256
OUT_BLOCK = 256
CONV_ROWS = 128
FFN_ROWS = 512
ODD_ROWS = 256
HALO = 16
N_DECAY_TABLES = 7
MOD_ROWS = 8
VMEM_LIMIT = 56 * 1024 * 1024


def _dot(a, b):
    return jnp.dot(a, b, preferred_element_type=F32)


def _dot_nt(a, b):
    return lax.dot_general(a, b, (((1,), (1,)), ((), ())), preferred_element_type=F32)


def _dot_tn(a, b):
    return lax.dot_general(a, b, (((0,), (0,)), ((), ())), preferred_element_type=F32)


def _silu(x):
    return x * jax.nn.sigmoid(x)


def _rms_mod(x32, nw, shift, scale):
    y = x32 * lax.rsqrt(jnp.mean(x32 * x32, axis=-1, keepdims=True) + EPS) * nw
    return y * (1.0 + scale) + shift


def _layer_norm(x32, w, b):
    mu = jnp.mean(x32, axis=-1, keepdims=True)
    xc = x32 - mu
    return xc * lax.rsqrt(jnp.mean(xc * xc, axis=-1, keepdims=True) + EPS) * w + b


def _const_spec(shape):
    return pl.BlockSpec(shape, lambda *_: (0,) * len(shape), pipeline_mode=pl.Buffered(1))


def _params(sem):
    return pltpu.CompilerParams(dimension_semantics=sem, vmem_limit_bytes=VMEM_LIMIT)


def _cast_band_specs(stack, layer, bsz, n_tiles):
    _, rows, cols = stack.shape
    n_steps = bsz * n_tiles
    hold = next(h for h in range(1, n_steps + 1)
                if n_steps % h == 0 and rows % (n_steps // h) == 0 and (rows * h // n_steps) % 16 == 0)
    band = rows * hold // n_steps
    return (pl.BlockSpec((None, band, cols), lambda b, i: (layer, (b * n_tiles + i) // hold, 0)),
            pl.BlockSpec((band, cols), lambda b, i: ((b * n_tiles + i) // hold, 0)))


def _cast_bands(srcs, dsts):
    for src, dst in zip(srcs, dsts):
        dst[...] = src[...].astype(BF16)


def _mod_kernel(c_ref, w_ref, b_ref, o_ref):
    s = _silu(c_ref[...]).astype(BF16)
    o_ref[0] = _dot(s, w_ref[0].astype(BF16)) + b_ref[0]


def _mod_call(cvec, ada_w, ada_b):
    depth, d, n = ada_w.shape
    tn = 1024
    return pl.pallas_call(
        _mod_kernel,
        grid=(depth, n // tn),
        in_specs=[_const_spec((MOD_ROWS, d)),
                  pl.BlockSpec((1, d, tn), lambda i, j: (i, 0, j)),
                  pl.BlockSpec((1, 1, tn), lambda i, j: (i, 0, j))],
        out_specs=pl.BlockSpec((1, MOD_ROWS, tn), lambda i, j: (i, 0, j)),
        out_shape=jax.ShapeDtypeStruct((depth, MOD_ROWS, n), F32),
        compiler_params=_params(("arbitrary", "arbitrary")),
        name="adaln_mod",
    )(cvec, ada_w, ada_b.reshape(depth, 1, n))


def _log_sigmoid(x):
    return -(jnp.maximum(-x, 0.0) + jnp.log1p(jnp.exp(-jnp.abs(x))))


def _decay_kernel(logit_ref, o_ref):
    j = pl.program_id(0)
    row = lax.broadcasted_iota(jnp.int32, (CHUNK, CHUNK), 0).astype(F32)
    col = lax.broadcasted_iota(jnp.int32, (CHUNK, CHUNK), 1).astype(F32)
    diff = row - col
    lower = diff >= 0.0
    for hd in range(HEADS):
        lgf = _log_sigmoid(jnp.full((CHUNK, CHUNK), logit_ref[(j * 2 + 0) * HEADS + hd], F32))
        lgb = _log_sigmoid(jnp.full((CHUNK, CHUNK), logit_ref[(j * 2 + 1) * HEADS + hd], F32))
        o_ref[0, 0, hd] = jnp.where(lower, jnp.exp(lgf * jnp.where(lower, diff, 0.0)),
                                    jnp.exp(lgb * jnp.where(lower, 0.0, -diff - 1.0)))
        o_ref[0, 1, hd] = jnp.exp(lgf * (CHUNK - 1.0 - row))
        o_ref[0, 2, hd] = jnp.exp(lgb * row)
        o_ref[0, 3, hd] = jnp.exp(lgf * (row + 1.0))
        o_ref[0, 4, hd] = jnp.exp(lgb * (CHUNK - 1.0 - row))
        o_ref[0, 5, hd] = jnp.exp(lgf * float(CHUNK))
        o_ref[0, 6, hd] = jnp.exp(lgb * float(CHUNK))


def _decay_call(ret_decay_logit):
    n_even = ret_decay_logit.shape[0]
    return pl.pallas_call(
        _decay_kernel,
        grid=(n_even,),
        in_specs=[pl.BlockSpec(memory_space=pltpu.SMEM)],
        out_specs=pl.BlockSpec((1, N_DECAY_TABLES, HEADS, CHUNK, CHUNK), lambda j: (j, 0, 0, 0, 0)),
        out_shape=jax.ShapeDtypeStruct((n_even, N_DECAY_TABLES, HEADS, CHUNK, CHUNK), F32),
        compiler_params=_params(("arbitrary",)),
        name="ret_decay",
    )(ret_decay_logit.reshape(-1).astype(F32))


def _even_in_kernel(x_ref, mod_ref, nw_ref, w_ref, cos_ref, sin_ref, s0_ref, dt_ref, *rest, n_cast, tm):
    cast_in = rest[:n_cast]
    q_ref, k_ref, v_ref, gs_ref, u_ref, sprev_ref, sfin_ref = rest[n_cast:n_cast + 7]
    cast_out = rest[n_cast + 7:2 * n_cast + 7]
    state_ref = rest[2 * n_cast + 7]
    b = pl.program_id(0)
    i = pl.program_id(1)
    _cast_bands(cast_in, cast_out)

    @pl.when(i == 0)
    def _():
        state_ref[...] = s0_ref[0]

    h = _rms_mod(x_ref[0], nw_ref[...], mod_ref[b * 6 + 0], mod_ref[b * 6 + 1]).astype(BF16)
    cosf = cos_ref[...]
    sinf = sin_ref[...]

    def rope(t):
        return t * cosf + pltpu.roll(t, HEAD_DIM // 2, 1) * sinf

    pq = _dot(h, w_ref[:, 0:RET_W])
    pk = _dot(h, w_ref[:, RET_W:2 * RET_W])
    pv = _dot(h, w_ref[:, 2 * RET_W:3 * RET_W])
    scale = HEAD_DIM ** -0.5
    for hd in range(HEADS):
        sl = slice(hd * HEAD_DIM, (hd + 1) * HEAD_DIM)
        q_ref[0, :, sl] = (rope(pq[:, sl]) * scale).astype(BF16)
        kr = rope(pk[:, sl]).astype(BF16)
        k_ref[0, :, sl] = kr
        vh = pv[:, sl]
        v_ref[0, :, sl] = vh.astype(BF16)
        zeta = dt_ref[1, hd]
        cdec = dt_ref[5, hd]
        for c in range(tm // CHUNK):
            rows = slice(c * CHUNK, (c + 1) * CHUNK)
            s_cur = state_ref[hd]
            sprev_ref[0, c, hd] = s_cur.astype(BF16)
            wv = (vh[rows] * zeta).astype(BF16)
            state_ref[hd] = s_cur * cdec + _dot_tn(kr[rows], wv)

    pg = _dot(h, w_ref[:, 3 * RET_W:4 * RET_W])
    gs_ref[0] = _silu(pg).astype(BF16)
    pa = _dot(h, w_ref[:, 4 * RET_W:4 * RET_W + CONV_CH])
    pb = _dot(h, w_ref[:, 4 * RET_W + CONV_CH:EVEN_IN])
    u_ref[0] = (pa * jax.nn.sigmoid(pb)).astype(BF16)

    @pl.when(i == pl.num_programs(1) - 1)
    def _():
        sfin_ref[0] = state_ref[...]


def _even_in_call(x, mod, nw, w_in, cosf, sinf, s0, dtab, tm, cast=()):
    bsz, l, d = x.shape
    n_chunks = l // CHUNK
    cpt = tm // CHUNK
    tok = lambda w: pl.BlockSpec((1, tm, w), lambda b, i: (b, i, 0))
    state_spec = pl.BlockSpec((1, HEADS, CHUNK, CHUNK), lambda b, i: (b, 0, 0, 0))
    specs = [_cast_band_specs(stack, layer, bsz, l // tm) for stack, layer in cast]
    outs = pl.pallas_call(
        functools.partial(_even_in_kernel, n_cast=len(cast), tm=tm),
        grid=(bsz, l // tm),
        in_specs=[tok(d),
                  _const_spec(mod.shape),
                  _const_spec((1, d)),
                  _const_spec(w_in.shape),
                  pl.BlockSpec((tm, HEAD_DIM), lambda b, i: (i, 0)),
                  pl.BlockSpec((tm, HEAD_DIM), lambda b, i: (i, 0)),
                  state_spec,
                  _const_spec(dtab.shape)] + [s[0] for s in specs],
        out_specs=[tok(RET_W), tok(RET_W), tok(RET_W), tok(RET_W), tok(CONV_CH),
                   pl.BlockSpec((1, cpt, HEADS, CHUNK, CHUNK), lambda b, i: (b, i, 0, 0, 0)),
                   state_spec] + [s[1] for s in specs],
        out_shape=[jax.ShapeDtypeStruct((bsz, l, RET_W), BF16),
                   jax.ShapeDtypeStruct((bsz, l, RET_W), BF16),
                   jax.ShapeDtypeStruct((bsz, l, RET_W), BF16),
                   jax.ShapeDtypeStruct((bsz, l, RET_W), BF16),
                   jax.ShapeDtypeStruct((bsz, l, CONV_CH), BF16),
                   jax.ShapeDtypeStruct((bsz, n_chunks, HEADS, CHUNK, CHUNK), BF16),
                   jax.ShapeDtypeStruct((bsz, HEADS, CHUNK, CHUNK), F32)]
        + [jax.ShapeDtypeStruct(stack.shape[1:], BF16) for stack, _ in cast],
        scratch_shapes=[pltpu.VMEM((HEADS, CHUNK, CHUNK), F32)],
        compiler_params=_params(("arbitrary", "arbitrary")),
        name="even_in",
    )(x, mod, nw, w_in, cosf, sinf, s0, dtab, *[stack for stack, _ in cast])
    return (*outs[:7], tuple(outs[7:]))


def _ctx_state_kernel(x_ref, mod_ref, nw_ref, w_ref, cos_ref, sin_ref, dt_ref, sf_ref, sb_ref):
    b = pl.program_id(0)
    h = _rms_mod(x_ref[0], nw_ref[...], mod_ref[b * 6 + 0], mod_ref[b * 6 + 1]).astype(BF16)
    cosf = cos_ref[...]
    sinf = sin_ref[...]
    pk = _dot(h, w_ref[:, 0:RET_W])
    pv = _dot(h, w_ref[:, RET_W:2 * RET_W])
    n_chunks = x_ref.shape[1] // CHUNK
    for hd in range(HEADS):
        sl = slice(hd * HEAD_DIM, (hd + 1) * HEAD_DIM)
        t = pk[:, sl]
        kr = (t * cosf + pltpu.roll(t, HEAD_DIM // 2, 1) * sinf).astype(BF16)
        vh = pv[:, sl]
        s_f = jnp.zeros((CHUNK, CHUNK), F32)
        s_b = jnp.zeros((CHUNK, CHUNK), F32)
        for c in range(n_chunks):
            rows = slice(c * CHUNK, (c + 1) * CHUNK)
            s_f = s_f * dt_ref[5, hd] + _dot_tn(kr[rows], (vh[rows] * dt_ref[1, hd]).astype(BF16))
        for c in reversed(range(n_chunks)):
            rows = slice(c * CHUNK, (c + 1) * CHUNK)
            s_b = s_b * dt_ref[6, hd] + _dot_tn(kr[rows], (vh[rows] * dt_ref[2, hd]).astype(BF16))
        sf_ref[0, hd] = s_f
        sb_ref[0, hd] = s_b


def _ctx_state_call(ctx, mod, nw, w_kv, cosf, sinf, dtab):
    bsz, lc, d = ctx.shape
    state_spec = pl.BlockSpec((1, HEADS, CHUNK, CHUNK), lambda b: (b, 0, 0, 0))
    return pl.pallas_call(
        _ctx_state_kernel,
        grid=(bsz,),
        in_specs=[pl.BlockSpec((1, lc, d), lambda b: (b, 0, 0)),
                  _const_spec(mod.shape),
                  _const_spec((1, d)),
                  _const_spec(w_kv.shape),
                  _const_spec(cosf.shape),
                  _const_spec(sinf.shape),
                  _const_spec(dtab.shape)],
        out_specs=[state_spec, state_spec],
        out_shape=[jax.ShapeDtypeStruct((bsz, HEADS, CHUNK, CHUNK), F32)] * 2,
        compiler_params=_params(("arbitrary",)),
        name="ctx_state",
    )(ctx, mod, nw, w_kv, cosf, sinf, dtab)


def _even_mix_pieces(b, i, n_tiles, x_ref, q_ref, k_ref, v_ref, gs_ref, u_ref, up_ref, un_ref, sprev_ref,
                     dt_ref, mod_ref, cw_ref, lnw_ref, lnb_ref, wo_ref, state_ref, uext_ref, mix_ref, xo_ref, tm):
    def retention(c, hd):
        rows = slice(c * CHUNK, (c + 1) * CHUNK)
        sl = slice(hd * HEAD_DIM, (hd + 1) * HEAD_DIM)
        qc = q_ref[0, rows, sl]
        kc = k_ref[0, rows, sl]
        vc = v_ref[0, rows, sl]
        s_b = state_ref[hd]
        p = (_dot_nt(qc, kc) * dt_ref[0, hd]).astype(BF16)
        y = _dot(p, vc)
        y = y + _dot(qc, sprev_ref[0, c, hd]) * dt_ref[3, hd]
        y = y + _dot(qc, s_b.astype(BF16)) * dt_ref[4, hd]
        y = y * lax.rsqrt(jnp.mean(y * y, axis=-1, keepdims=True) + EPS)
        mix_ref[rows, sl] = gs_ref[0, rows, sl] * y.astype(BF16)
        wv = (vc.astype(F32) * dt_ref[2, hd]).astype(BF16)
        state_ref[hd] = s_b * dt_ref[6, hd] + _dot_tn(kc, wv)

    def halo():
        uext_ref[0:HALO] = jnp.where(i < n_tiles - 1, up_ref[0].astype(F32), 0.0)
        uext_ref[HALO:HALO + tm] = u_ref[0].astype(F32)
        uext_ref[HALO + tm:2 * HALO + tm] = jnp.where(i > 0, un_ref[0].astype(F32), 0.0)

    def conv(r):
        blocks = []
        for cb in range(CONV_CH // LANES):
            cols = slice(cb * LANES, (cb + 1) * LANES)
            acc = None
            for sub in range(SUBLANES):
                part = None
                for t in range(sub, CONV_K, SUBLANES):
                    start = r * CONV_ROWS + t - sub
                    term = uext_ref[start:start + CONV_ROWS + SUBLANES, cols] * cw_ref[t:t + 1, cols]
                    part = term if part is None else part + term
                shifted = part[sub + 1:sub + 1 + CONV_ROWS]
                acc = shifted if acc is None else acc + shifted
            blocks.append(acc)
        cv = _silu(_layer_norm(jnp.concatenate(blocks, axis=-1), lnw_ref[...], lnb_ref[...]))
        mix_ref[r * CONV_ROWS:(r + 1) * CONV_ROWS, RET_W:RET_W + CONV_CH] = cv.astype(BF16)

    def out_proj(n):
        cols = slice(n * OUT_BLOCK, (n + 1) * OUT_BLOCK)
        out = _dot(mix_ref[...], wo_ref[:, cols])
        xo_ref[:, cols] = x_ref[0, :, cols] + mod_ref[b * 6 + 2][:, cols] * out

    pieces = [halo]
    convs = [functools.partial(conv, r) for r in range(tm // CONV_ROWS)]
    rets = [functools.partial(retention, c, hd) for c in reversed(range(tm // CHUNK)) for hd in range(HEADS)]
    per = len(rets) // len(convs)
    for r, cv in enumerate(convs):
        pieces += rets[r * per:(r + 1) * per] + [cv]
    pieces += rets[len(convs) * per:]
    pieces += [functools.partial(out_proj, n) for n in range(D_MODEL // OUT_BLOCK)]
    return pieces


def _ffn_pieces(xs_ref, o_ref, b, mod_ref, nw_ref, wg_ref, wu_ref, wd_ref, fnw_ref, h_ref, act_ref, final_norm):
    def prologue():
        x32 = xs_ref[...]
        o_ref[...] = x32
        h_ref[...] = _rms_mod(x32, nw_ref[...], mod_ref[b * 6 + 3], mod_ref[b * 6 + 4]).astype(BF16)

    def gate_up(j):
        cols = slice(j * FF_BLOCK, (j + 1) * FF_BLOCK)
        h = h_ref[...]
        act_ref[:, cols] = (_silu(_dot(h, wg_ref[:, cols])) * _dot(h, wu_ref[:, cols])).astype(BF16)

    def down(n):
        cols = slice(n * OUT_BLOCK, (n + 1) * OUT_BLOCK)
        o_ref[:, cols] = o_ref[:, cols] + mod_ref[b * 6 + 5][:, cols] * _dot(act_ref[...], wd_ref[:, cols])

    def final():
        y = o_ref[...]
        o_ref[...] = y * lax.rsqrt(jnp.mean(y * y, axis=-1, keepdims=True) + EPS) * fnw_ref[...]

    pieces = [prologue]
    pieces += [functools.partial(gate_up, j) for j in range(D_FF // FF_BLOCK)]
    pieces += [functools.partial(down, n) for n in range(D_MODEL // OUT_BLOCK)]
    if final_norm:
        pieces.append(final)
    return pieces


def _even_mix_kernel(x_ref, q_ref, k_ref, v_ref, gs_ref, u_ref, up_ref, un_ref, sprev_ref, sb0_ref,
                     dt_ref, mod_ref, cw_ref, lnw_ref, lnb_ref, wo_ref,
                     o_ref, sbfin_ref, state_ref, uext_ref, mix_ref, *, tm):
    b = pl.program_id(0)
    i = pl.program_id(1)
    n_tiles = pl.num_programs(1)

    @pl.when(i == 0)
    def _():
        state_ref[...] = sb0_ref[0]

    for piece in _even_mix_pieces(b, i, n_tiles, x_ref, q_ref, k_ref, v_ref, gs_ref, u_ref, up_ref, un_ref,
                                  sprev_ref, dt_ref, mod_ref, cw_ref, lnw_ref, lnb_ref, wo_ref,
                                  state_ref, uext_ref, mix_ref, o_ref.at[0], tm):
        piece()

    @pl.when(i == n_tiles - 1)
    def _():
        sbfin_ref[0] = state_ref[...]


def _even_mix_call(x, q, k, v, gs, u, sprev, sb0, dtab, mod, conv_w, ln_w, ln_b, w_out, tm):
    bsz, l, d = x.shape
    n_tiles = l // tm
    cpt = tm // CHUNK
    hpt = tm // HALO
    n_halo = l // HALO
    rev = lambda i: n_tiles - 1 - i
    tok = lambda w: pl.BlockSpec((1, tm, w), lambda b, i: (b, rev(i), 0))
    state_spec = pl.BlockSpec((1, HEADS, CHUNK, CHUNK), lambda b, i: (b, 0, 0, 0))
    return pl.pallas_call(
        functools.partial(_even_mix_kernel, tm=tm),
        grid=(bsz, n_tiles),
        in_specs=[tok(d), tok(RET_W), tok(RET_W), tok(RET_W), tok(RET_W), tok(CONV_CH),
                  pl.BlockSpec((1, HALO, CONV_CH), lambda b, i: (b, jnp.maximum(rev(i) * hpt - 1, 0), 0)),
                  pl.BlockSpec((1, HALO, CONV_CH),
                               lambda b, i: (b, jnp.minimum((rev(i) + 1) * hpt, n_halo - 1), 0)),
                  pl.BlockSpec((1, cpt, HEADS, CHUNK, CHUNK), lambda b, i: (b, rev(i), 0, 0, 0)),
                  state_spec,
                  _const_spec(dtab.shape),
                  _const_spec(mod.shape),
                  _const_spec(conv_w.shape),
                  _const_spec((1, CONV_CH)),
                  _const_spec((1, CONV_CH)),
                  _const_spec(w_out.shape)],
        out_specs=[tok(d), state_spec],
        out_shape=[jax.ShapeDtypeStruct((bsz, l, d), F32),
                   jax.ShapeDtypeStruct((bsz, HEADS, CHUNK, CHUNK), F32)],
        scratch_shapes=[pltpu.VMEM((HEADS, CHUNK, CHUNK), F32),
                        pltpu.VMEM((tm + 2 * HALO, CONV_CH), F32),
                        pltpu.VMEM((tm, RET_W + CONV_CH), BF16)],
        compiler_params=_params(("arbitrary", "arbitrary")),
        name="even_mix",
    )(x, q, k, v, gs, u, u, u, sprev, sb0, dtab, mod, conv_w, ln_w, ln_b, w_out)


def _odd_kernel(x_ref, xp_ref, xn_ref, mod_ref, nw_ref, wi_ref, pw_ref, ps_ref, lnw_ref, lnb_ref,
                sgw_ref, sgb_ref, wo_ref, o_ref, h_ref, pcx_ref, mix_ref, *, tm, seq_len):
    b = pl.program_id(0)
    i = pl.program_id(1)
    n_tiles = pl.num_programs(1)
    nw = nw_ref[...]
    shift = mod_ref[b * 6 + 0]
    scale = mod_ref[b * 6 + 1]

    h_ref[0:HALO] = _rms_mod(xp_ref[0], nw, shift, scale).astype(BF16)
    h_ref[HALO:HALO + tm] = _rms_mod(x_ref[0], nw, shift, scale).astype(BF16)
    h_ref[HALO + tm:2 * HALO + tm] = _rms_mod(xn_ref[0], nw, shift, scale).astype(BF16)

    pcx_ref[0:tm + 2 * HALO] = _dot(h_ref[...], wi_ref[:, 0:POOL_CH])
    pcx_ref[0:HALO] = jnp.where(i > 0, pcx_ref[0:HALO], 0.0)
    pcx_ref[HALO + tm:2 * HALO + tm] = jnp.where(i < n_tiles - 1, pcx_ref[HALO + tm:2 * HALO + tm], 0.0)
    pcx_ref[tm + 2 * HALO:tm + 2 * HALO + SUBLANES] = jnp.zeros((SUBLANES, POOL_CH), F32)

    sub = min(tm, ODD_ROWS)
    for r0 in range(0, tm, sub):
        rows = slice(r0, r0 + sub)
        pos = i * tm + r0 + lax.broadcasted_iota(jnp.int32, (sub, POOL_GC), 0)
        for gi, w in enumerate(POOL_WINDOWS):
            left = w // 2
            right = w - 1 - left
            sl = slice(gi * POOL_GC, (gi + 1) * POOL_GC)
            run = pcx_ref[r0:r0 + sub + 2 * HALO + SUBLANES, sl]
            span = 1
            while span < w:
                keep = run.shape[0] - SUBLANES
                run = run[0:keep] + run[span:span + keep]
                span *= 2
            win = run[HALO - left:HALO - left + sub]
            cnt = jnp.minimum(pos + right, seq_len - 1) - jnp.maximum(pos - left, 0) + 1
            m = win / cnt.astype(F32) - pcx_ref[HALO + r0:HALO + r0 + sub, sl]
            po = _dot(m.astype(BF16), pw_ref[gi]) * ps_ref[:, sl]
            mix_ref[rows, sl] = po.astype(BF16)

        pd = _dot(h_ref[HALO + r0:HALO + r0 + sub], wi_ref[:, POOL_CH:ODD_IN])
        z = 0.5 * pd * (1.0 + lax.erf(pd * (2.0 ** -0.5)))
        zu = z[:, 0:SG_CH]
        zv = _layer_norm(z[:, SG_CH:2 * SG_CH], lnw_ref[...], lnb_ref[...]).astype(BF16)
        for c in range(sub // CHUNK):
            crows = slice(c * CHUNK, (c + 1) * CHUNK)
            for g in range(SG_GROUPS):
                sl = slice(g * SG_GC, (g + 1) * SG_GC)
                s = _dot(sgw_ref[g], zv[crows, sl]) + sgb_ref[g]
                mix_ref[r0 + c * CHUNK:r0 + (c + 1) * CHUNK, POOL_CH + g * SG_GC:POOL_CH + (g + 1) * SG_GC] = (
                    zu[crows, sl] * s).astype(BF16)

        out = _dot(mix_ref[rows], wo_ref[...])
        o_ref[0, rows] = x_ref[0, rows] + mod_ref[b * 6 + 2] * out


def _odd_call(x, mod, nw, w_in, pool_w, pool_scale, ln_w, ln_b, sg_w, sg_b_full, w_out, tm):
    bsz, l, d = x.shape
    hpt = tm // HALO
    n_halo = l // HALO
    tok = pl.BlockSpec((1, tm, d), lambda b, i: (b, i, 0))
    return pl.pallas_call(
        functools.partial(_odd_kernel, tm=tm, seq_len=l),
        grid=(bsz, l // tm),
        in_specs=[tok,
                  pl.BlockSpec((1, HALO, d), lambda b, i: (b, jnp.maximum(i * hpt - 1, 0), 0)),
                  pl.BlockSpec((1, HALO, d), lambda b, i: (b, jnp.minimum((i + 1) * hpt, n_halo - 1), 0)),
                  _const_spec(mod.shape),
                  _const_spec((1, d)),
                  _const_spec(w_in.shape),
                  _const_spec(pool_w.shape),
                  _const_spec((1, POOL_CH)),
                  _const_spec((1, SG_CH)),
                  _const_spec((1, SG_CH)),
                  _const_spec(sg_w.shape),
                  _const_spec(sg_b_full.shape),
                  _const_spec(w_out.shape)],
        out_specs=tok,
        out_shape=jax.ShapeDtypeStruct((bsz, l, d), F32),
        scratch_shapes=[pltpu.VMEM((tm + 2 * HALO, d), BF16),
                        pltpu.VMEM((tm + 2 * HALO + SUBLANES, POOL_CH), F32),
                        pltpu.VMEM((tm, POOL_CH + SG_CH), BF16)],
        compiler_params=_params(("arbitrary", "arbitrary")),
        name="odd_mix",
    )(x, x, x, mod, nw, w_in, pool_w, pool_scale, ln_w, ln_b, sg_w, sg_b_full, w_out)


def _ffn_kernel(x_ref, mod_ref, nw_ref, wg_ref, wu_ref, wd_ref, fnw_ref, *rest, n_cast, final_norm):
    cast_in = rest[:n_cast]
    o_ref = rest[n_cast]
    cast_out = rest[n_cast + 1:2 * n_cast + 1]
    h_ref, act_ref = rest[2 * n_cast + 1:]
    _cast_bands(cast_in, cast_out)
    tm = x_ref.shape[1]
    subs = []
    for r0 in range(0, tm, min(tm, FFN_ROWS)):
        rows = pl.ds(r0, min(tm, FFN_ROWS))
        subs.append(_ffn_pieces(x_ref.at[0, rows], o_ref.at[0, rows], pl.program_id(0), mod_ref, nw_ref,
                                wg_ref, wu_ref, wd_ref, fnw_ref, h_ref.at[rows], act_ref.at[rows], final_norm))
    order = [subs[0][0]]
    for k, pieces in enumerate(subs):
        order.append(pieces[1])
        if k + 1 < len(subs):
            order.append(subs[k + 1][0])
        order += pieces[2:]
    for piece in order:
        piece()


def _ffn_call(x, mod, nw, wg, wu, wd, fnw, tm, final_norm, cast=()):
    bsz, l, d = x.shape
    n_tiles = l // tm
    tok = pl.BlockSpec((1, tm, d), lambda b, i: (b, i, 0))
    specs = [_cast_band_specs(stack, layer, bsz, n_tiles) for stack, layer in cast]
    outs = pl.pallas_call(
        functools.partial(_ffn_kernel, n_cast=len(cast), final_norm=final_norm),
        grid=(bsz, n_tiles),
        in_specs=[tok,
                  _const_spec(mod.shape),
                  _const_spec((1, d)),
                  _const_spec(wg.shape),
                  _const_spec(wu.shape),
                  _const_spec(wd.shape),
                  _const_spec((1, d))] + [s[0] for s in specs],
        out_specs=[tok] + [s[1] for s in specs],
        out_shape=[jax.ShapeDtypeStruct((bsz, l, d), F32)]
        + [jax.ShapeDtypeStruct(stack.shape[1:], BF16) for stack, _ in cast],
        scratch_shapes=[pltpu.VMEM((tm, d), BF16), pltpu.VMEM((tm, D_FF), BF16)],
        compiler_params=_params(("arbitrary", "arbitrary")),
        name="ffn",
    )(x, mod, nw, wg, wu, wd, fnw, *[stack for stack, _ in cast])
    return outs[0], tuple(outs[1:])


def _rope_tables(p_seq, p_row, p_col):
    parts = []
    for p, n in zip((p_seq, p_row, p_col), ROPE_PAIRS):
        freq = ROPE_BASE ** (-jnp.arange(n, dtype=F32) / n)
        parts.append(p[:, None] * freq[None, :])
    ang = jnp.concatenate(parts, axis=-1)
    cos, sin = jnp.cos(ang), jnp.sin(ang)
    return jnp.concatenate([cos, cos], axis=-1), jnp.concatenate([-sin, sin], axis=-1)


def _tile(l):
    return min(l, 1024)


def kernel(x, c, ctx, c_ctx, ada_w, ada_b, norm_w, even_w_in, even_w_out, ret_decay_logit, conv_dw_w, conv_ln_w, conv_ln_b, odd_w_in, odd_w_out, pool_w, pool_scale, sg_ln_w, sg_ln_b, sg_w, sg_b, ffn_w_gate, ffn_w_up, ffn_w_down, final_norm_w):
    bsz, l, d = x.shape
    lc = ctx.shape[1]
    assert d == D_MODEL and l % 256 == 0 and lc % CHUNK == 0 and bsz + 1 <= MOD_ROWS
    rows = l // GRID_W
    grid_r = jnp.broadcast_to(jnp.arange(rows, dtype=F32)[:, None], (rows, GRID_W)).reshape(-1)
    grid_c = jnp.broadcast_to(jnp.arange(GRID_W, dtype=F32)[None, :], (rows, GRID_W)).reshape(-1)
    cos_x, sin_x = _rope_tables(jnp.full((l,), lc, F32), grid_r, grid_c)
    zeros_c = jnp.zeros((lc,), F32)
    cos_c, sin_c = _rope_tables(jnp.arange(lc, dtype=F32), zeros_c, zeros_c)

    cvec = jnp.concatenate([c, c_ctx[None, :], jnp.zeros((MOD_ROWS - bsz - 1, d), F32)], axis=0)
    mod_all = _mod_call(cvec, ada_w, ada_b)
    dtabs = _decay_call(ret_decay_logit)
    zero_state = jnp.zeros((bsz, HEADS, CHUNK, CHUNK), F32)
    fnw = final_norm_w.reshape(1, d)
    tx, tc = _tile(l), _tile(lc)

    def layer_matrices(i):
        mix_in, mix_out = (even_w_in, even_w_out) if i % 2 == 0 else (odd_w_in, odd_w_out)
        return (mix_in, i // 2), (mix_out, i // 2), (ffn_w_gate, i), (ffn_w_up, i), (ffn_w_down, i)

    first = layer_matrices(0)
    bf16_mats = (first[0][0][first[0][1]].astype(BF16),) + (None,) * (len(first) - 1)
    for i in range(DEPTH):
        j = i // 2
        even = i % 2 == 0
        ctx_after = any(m % 2 == 0 for m in range(i + 1, DEPTH))
        mod_x = mod_all[i, :bsz].reshape(bsz * 6, 1, d)
        mod_c = jnp.broadcast_to(mod_all[i, bsz].reshape(1, 6, 1, d), (bsz, 6, 1, d)).reshape(bsz * 6, 1, d)
        nw1 = norm_w[i, 0].reshape(1, d)
        nw2 = norm_w[i, 1].reshape(1, d)
        w_in, w_out, wg, wu, wd = bf16_mats
        if even:
            conv_w = conv_dw_w[j]
            lnw = conv_ln_w[j].reshape(1, CONV_CH)
            lnb = conv_ln_b[j].reshape(1, CONV_CH)
            dtab = dtabs[j]
            if ctx_after:
                qc, kc, vc, gc, uc, spc, s_f, _ = _even_in_call(ctx, mod_c, nw1, w_in, cos_c, sin_c, zero_state,
                                                                dtab, tc)
            else:
                s_f, s_b = _ctx_state_call(ctx, mod_c, nw1, w_in[:, RET_W:3 * RET_W], cos_c, sin_c, dtab)
            pending = layer_matrices(i)[1:] if w_out is None else ()
            qx, kx, vx, gx, ux, spx, _, cast_now = _even_in_call(x, mod_x, nw1, w_in, cos_x, sin_x, s_f, dtab, tx,
                                                                 pending)
            if pending:
                w_out, wg, wu, wd = cast_now
            if ctx_after:
                ctx, s_b = _even_mix_call(ctx, qc, kc, vc, gc, uc, spc, zero_state, dtab, mod_c,
                                          conv_w, lnw, lnb, w_out, tc)
            x, _ = _even_mix_call(x, qx, kx, vx, gx, ux, spx, s_b, dtab, mod_x, conv_w, lnw, lnb, w_out, tx)
        else:
            args = (nw1, w_in, pool_w[j].astype(BF16), pool_scale[j].reshape(1, POOL_CH),
                    sg_ln_w[j].reshape(1, SG_CH), sg_ln_b[j].reshape(1, SG_CH), sg_w[j].astype(BF16),
                    jnp.broadcast_to(sg_b[j][:, :, None], (SG_GROUPS, CHUNK, SG_GC)), w_out)
            if ctx_after:
                ctx = _odd_call(ctx, mod_c, *args, tc)
            x = _odd_call(x, mod_x, *args, tx)
        if ctx_after:
            ctx, _ = _ffn_call(ctx, mod_c, nw2, wg, wu, wd, fnw, tc, False)
        later = layer_matrices(i + 1) if i + 1 < DEPTH else ()
        x, bf16_mats = _ffn_call(x, mod_x, nw2, wg, wu, wd, fnw, min(l, 2 * FFN_ROWS), i == DEPTH - 1, later)
    return x
```

```python
import functools

import jax
import jax.numpy as jnp
from jax import lax
from jax.experimental import pallas as pl
from jax.experimental.pallas import tpu as pltpu

F32 = jnp.float32
BF16 = jnp.bfloat16

D_MODEL = 1024
DEPTH = 4
GRID_W = 64
EPS = 1e-6
HEADS = 4
HEAD_DIM = 128
RET_W = HEADS * HEAD_DIM
CHUNK = 128
ROPE_BASE = 10000.0
ROPE_PAIRS = (HEAD_DIM // 8, 3 * HEAD_DIM // 16, 3 * HEAD_DIM // 16)
CONV_CH = 512
CONV_K = 31
CONV_PAD = CONV_K // 2
EVEN_IN = 4 * RET_W + 2 * CONV_CH
POOL_CH = 512
POOL_WINDOWS = (2, 4, 8, 16)
POOL_GC = 128
SG_CH = 512
SG_GROUPS = 4
SG_GC = 128
ODD_IN = POOL_CH + 2 * SG_CH
D_FF = 2816
SUBLANES = 8
LANES = 128
FF_BLOCK = 256
OUT_BLOCK = 256
CONV_ROWS = 128
FFN_ROWS = 512
ODD_ROWS = 256
HALO = 16
N_DECAY_TABLES = 7
MOD_ROWS = 8
VMEM_LIMIT = 56 * 1024 * 1024


def _dot(a, b):
    return jnp.dot(a, b, preferred_element_type=F32)


def _dot_nt(a, b):
    return lax.dot_general(a, b, (((1,), (1,)), ((), ())), preferred_element_type=F32)


def _dot_tn(a, b):
    return lax.dot_general(a, b, (((0,), (0,)), ((), ())), preferred_element_type=F32)


def _silu(x):
    return x * jax.nn.sigmoid(x)


def _rms_mod(x32, nw, shift, scale):
    y = x32 * lax.rsqrt(jnp.mean(x32 * x32, axis=-1, keepdims=True) + EPS) * nw
    return y * (1.0 + scale) + shift


def _layer_norm(x32, w, b):
    mu = jnp.mean(x32, axis=-1, keepdims=True)
    xc = x32 - mu
    return xc * lax.rsqrt(jnp.mean(xc * xc, axis=-1, keepdims=True) + EPS) * w + b


def _const_spec(shape):
    return pl.BlockSpec(shape, lambda *_: (0,) * len(shape), pipeline_mode=pl.Buffered(1))


def _params(sem):
    return pltpu.CompilerParams(dimension_semantics=sem, vmem_limit_bytes=VMEM_LIMIT)


def _cast_band_specs(stack, layer, bsz, n_tiles):
    _, rows, cols = stack.shape
    n_steps = bsz * n_tiles
    hold = next(h for h in range(1, n_steps + 1)
                if n_steps % h == 0 and rows % (n_steps // h) == 0 and (rows * h // n_steps) % 16 == 0)
    band = rows * hold // n_steps
    return (pl.BlockSpec((None, band, cols), lambda b, i: (layer, (b * n_tiles + i) // hold, 0)),
            pl.BlockSpec((band, cols), lambda b, i: ((b * n_tiles + i) // hold, 0)))


def _cast_bands(srcs, dsts):
    for src, dst in zip(srcs, dsts):
        dst[...] = src[...].astype(BF16)


def _mod_kernel(c_ref, w_ref, b_ref, o_ref):
    o_ref[...] = _dot(_silu(c_ref[...]).astype(BF16), w_ref[...].astype(BF16)) + b_ref[...]


def _mod_call(c_rows, ada_w, ada_b, layer):
    depth, d, n = ada_w.shape
    tn = 1024
    return pl.pallas_call(
        _mod_kernel,
        grid=(n // tn,),
        in_specs=[_const_spec((MOD_ROWS, d)),
                  pl.BlockSpec((None, d, tn), lambda j: (layer, 0, j)),
                  pl.BlockSpec((None, 1, tn), lambda j: (layer, 0, j))],
        out_specs=pl.BlockSpec((MOD_ROWS, tn), lambda j: (0, j)),
        out_shape=jax.ShapeDtypeStruct((MOD_ROWS, n), F32),
        compiler_params=_params(("arbitrary",)),
        name="adaln_mod",
    )(c_rows, ada_w, ada_b.reshape(depth, 1, n))


def _log_sigmoid(x):
    return -(jnp.maximum(-x, 0.0) + jnp.log1p(jnp.exp(-jnp.abs(x))))


def _decay_kernel(logit_ref, o_ref):
    j = pl.program_id(0)
    row = lax.broadcasted_iota(jnp.int32, (CHUNK, CHUNK), 0).astype(F32)
    col = lax.broadcasted_iota(jnp.int32, (CHUNK, CHUNK), 1).astype(F32)
    diff = row - col
    lower = diff >= 0.0
    for hd in range(HEADS):
        lgf = _log_sigmoid(jnp.full((CHUNK, CHUNK), logit_ref[(j * 2 + 0) * HEADS + hd], F32))
        lgb = _log_sigmoid(jnp.full((CHUNK, CHUNK), logit_ref[(j * 2 + 1) * HEADS + hd], F32))
        o_ref[0, 0, hd] = jnp.where(lower, jnp.exp(lgf * jnp.where(lower, diff, 0.0)),
                                    jnp.exp(lgb * jnp.where(lower, 0.0, -diff - 1.0)))
        o_ref[0, 1, hd] = jnp.exp(lgf * (CHUNK - 1.0 - row))
        o_ref[0, 2, hd] = jnp.exp(lgb * row)
        o_ref[0, 3, hd] = jnp.exp(lgf * (row + 1.0))
        o_ref[0, 4, hd] = jnp.exp(lgb * (CHUNK - 1.0 - row))
        o_ref[0, 5, hd] = jnp.exp(lgf * float(CHUNK))
        o_ref[0, 6, hd] = jnp.exp(lgb * float(CHUNK))


def _decay_call(ret_decay_logit):
    n_even = ret_decay_logit.shape[0]
    return pl.pallas_call(
        _decay_kernel,
        grid=(n_even,),
        in_specs=[pl.BlockSpec(memory_space=pltpu.SMEM)],
        out_specs=pl.BlockSpec((1, N_DECAY_TABLES, HEADS, CHUNK, CHUNK), lambda j: (j, 0, 0, 0, 0)),
        out_shape=jax.ShapeDtypeStruct((n_even, N_DECAY_TABLES, HEADS, CHUNK, CHUNK), F32),
        compiler_params=_params(("arbitrary",)),
        name="ret_decay",
    )(ret_decay_logit.reshape(-1).astype(F32))


def _even_in_kernel(x_ref, mod_ref, nw_ref, w_ref, cos_ref, sin_ref, s0_ref, dt_ref, *rest, n_cast, tm):
    cast_in = rest[:n_cast]
    q_ref, k_ref, v_ref, gs_ref, u_ref, sprev_ref, sfin_ref = rest[n_cast:n_cast + 7]
    cast_out = rest[n_cast + 7:2 * n_cast + 7]
    state_ref = rest[2 * n_cast + 7]
    b = pl.program_id(0)
    i = pl.program_id(1)
    _cast_bands(cast_in, cast_out)

    @pl.when(i == 0)
    def _():
        state_ref[...] = s0_ref[0]

    h = _rms_mod(x_ref[0], nw_ref[...], mod_ref[b * 6 + 0], mod_ref[b * 6 + 1]).astype(BF16)
    cosf = cos_ref[...]
    sinf = sin_ref[...]

    def rope(t):
        return t * cosf + pltpu.roll(t, HEAD_DIM // 2, 1) * sinf

    pq = _dot(h, w_ref[:, 0:RET_W])
    pk = _dot(h, w_ref[:, RET_W:2 * RET_W])
    pv = _dot(h, w_ref[:, 2 * RET_W:3 * RET_W])
    scale = HEAD_DIM ** -0.5
    for hd in range(HEADS):
        sl = slice(hd * HEAD_DIM, (hd + 1) * HEAD_DIM)
        q_ref[0, :, sl] = (rope(pq[:, sl]) * scale).astype(BF16)
        kr = rope(pk[:, sl]).astype(BF16)
        k_ref[0, :, sl] = kr
        vh = pv[:, sl]
        v_ref[0, :, sl] = vh.astype(BF16)
        zeta = dt_ref[1, hd]
        cdec = dt_ref[5, hd]
        for c in range(tm // CHUNK):
            rows = slice(c * CHUNK, (c + 1) * CHUNK)
            s_cur = state_ref[hd]
            sprev_ref[0, c, hd] = s_cur.astype(BF16)
            wv = (vh[rows] * zeta).astype(BF16)
            state_ref[hd] = s_cur * cdec + _dot_tn(kr[rows], wv)

    pg = _dot(h, w_ref[:, 3 * RET_W:4 * RET_W])
    gs_ref[0] = _silu(pg).astype(BF16)
    pa = _dot(h, w_ref[:, 4 * RET_W:4 * RET_W + CONV_CH])
    pb = _dot(h, w_ref[:, 4 * RET_W + CONV_CH:EVEN_IN])
    u_ref[0] = (pa * jax.nn.sigmoid(pb)).astype(BF16)

    @pl.when(i == pl.num_programs(1) - 1)
    def _():
        sfin_ref[0] = state_ref[...]


def _even_in_call(x, mod, nw, w_in, cosf, sinf, s0, dtab, tm, cast=()):
    bsz, l, d = x.shape
    n_chunks = l // CHUNK
    cpt = tm // CHUNK
    tok = lambda w: pl.BlockSpec((1, tm, w), lambda b, i: (b, i, 0))
    state_spec = pl.BlockSpec((1, HEADS, CHUNK, CHUNK), lambda b, i: (b, 0, 0, 0))
    specs = [_cast_band_specs(stack, layer, bsz, l // tm) for stack, layer in cast]
    outs = pl.pallas_call(
        functools.partial(_even_in_kernel, n_cast=len(cast), tm=tm),
        grid=(bsz, l // tm),
        in_specs=[tok(d),
                  _const_spec(mod.shape),
                  _const_spec((1, d)),
                  _const_spec(w_in.shape),
                  pl.BlockSpec((tm, HEAD_DIM), lambda b, i: (i, 0)),
                  pl.BlockSpec((tm, HEAD_DIM), lambda b, i: (i, 0)),
                  state_spec,
                  _const_spec(dtab.shape)] + [s[0] for s in specs],
        out_specs=[tok(RET_W), tok(RET_W), tok(RET_W), tok(RET_W), tok(CONV_CH),
                   pl.BlockSpec((1, cpt, HEADS, CHUNK, CHUNK), lambda b, i: (b, i, 0, 0, 0)),
                   state_spec] + [s[1] for s in specs],
        out_shape=[jax.ShapeDtypeStruct((bsz, l, RET_W), BF16),
                   jax.ShapeDtypeStruct((bsz, l, RET_W), BF16),
                   jax.ShapeDtypeStruct((bsz, l, RET_W), BF16),
                   jax.ShapeDtypeStruct((bsz, l, RET_W), BF16),
                   jax.ShapeDtypeStruct((bsz, l, CONV_CH), BF16),
                   jax.ShapeDtypeStruct((bsz, n_chunks, HEADS, CHUNK, CHUNK), BF16),
                   jax.ShapeDtypeStruct((bsz, HEADS, CHUNK, CHUNK), F32)]
        + [jax.ShapeDtypeStruct(stack.shape[1:], BF16) for stack, _ in cast],
        scratch_shapes=[pltpu.VMEM((HEADS, CHUNK, CHUNK), F32)],
        compiler_params=_params(("arbitrary", "arbitrary")),
        name="even_in",
    )(x, mod, nw, w_in, cosf, sinf, s0, dtab, *[stack for stack, _ in cast])
    return (*outs[:7], tuple(outs[7:]))


def _ctx_state_kernel(x_ref, mod_ref, nw_ref, w_ref, cos_ref, sin_ref, dt_ref, sf_ref, sb_ref):
    b = pl.program_id(0)
    h = _rms_mod(x_ref[0], nw_ref[...], mod_ref[b * 6 + 0], mod_ref[b * 6 + 1]).astype(BF16)
    cosf = cos_ref[...]
    sinf = sin_ref[...]
    pk = _dot(h, w_ref[:, 0:RET_W])
    pv = _dot(h, w_ref[:, RET_W:2 * RET_W])
    n_chunks = x_ref.shape[1] // CHUNK
    for hd in range(HEADS):
        sl = slice(hd * HEAD_DIM, (hd + 1) * HEAD_DIM)
        t = pk[:, sl]
        kr = (t * cosf + pltpu.roll(t, HEAD_DIM // 2, 1) * sinf).astype(BF16)
        vh = pv[:, sl]
        s_f = jnp.zeros((CHUNK, CHUNK), F32)
        s_b = jnp.zeros((CHUNK, CHUNK), F32)
        for c in range(n_chunks):
            rows = slice(c * CHUNK, (c + 1) * CHUNK)
            s_f = s_f * dt_ref[5, hd] + _dot_tn(kr[rows], (vh[rows] * dt_ref[1, hd]).astype(BF16))
        for c in reversed(range(n_chunks)):
            rows = slice(c * CHUNK, (c + 1) * CHUNK)
            s_b = s_b * dt_ref[6, hd] + _dot_tn(kr[rows], (vh[rows] * dt_ref[2, hd]).astype(BF16))
        sf_ref[0, hd] = s_f
        sb_ref[0, hd] = s_b


def _ctx_state_call(ctx, mod, nw, w_kv, cosf, sinf, dtab):
    bsz, lc, d = ctx.shape
    state_spec = pl.BlockSpec((1, HEADS, CHUNK, CHUNK), lambda b: (b, 0, 0, 0))
    return pl.pallas_call(
        _ctx_state_kernel,
        grid=(bsz,),
        in_specs=[pl.BlockSpec((1, lc, d), lambda b: (b, 0, 0)),
                  _const_spec(mod.shape),
                  _const_spec((1, d)),
                  _const_spec(w_kv.shape),
                  _const_spec(cosf.shape),
                  _const_spec(sinf.shape),
                  _const_spec(dtab.shape)],
        out_specs=[state_spec, state_spec],
        out_shape=[jax.ShapeDtypeStruct((bsz, HEADS, CHUNK, CHUNK), F32)] * 2,
        compiler_params=_params(("arbitrary",)),
        name="ctx_state",
    )(ctx, mod, nw, w_kv, cosf, sinf, dtab)


def _even_mix_pieces(b, i, n_tiles, x_ref, q_ref, k_ref, v_ref, gs_ref, u_ref, up_ref, un_ref, sprev_ref,
                     dt_ref, mod_ref, cw_ref, lnw_ref, lnb_ref, wo_ref, state_ref, uext_ref, mix_ref, xo_ref, tm):
    def retention(c, hd):
        rows = slice(c * CHUNK, (c + 1) * CHUNK)
        sl = slice(hd * HEAD_DIM, (hd + 1) * HEAD_DIM)
        qc = q_ref[0, rows, sl]
        kc = k_ref[0, rows, sl]
        vc = v_ref[0, rows, sl]
        s_b = state_ref[hd]
        p = (_dot_nt(qc, kc) * dt_ref[0, hd]).astype(BF16)
        y = _dot(p, vc)
        y = y + _dot(qc, sprev_ref[0, c, hd]) * dt_ref[3, hd]
        y = y + _dot(qc, s_b.astype(BF16)) * dt_ref[4, hd]
        y = y * lax.rsqrt(jnp.mean(y * y, axis=-1, keepdims=True) + EPS)
        mix_ref[rows, sl] = gs_ref[0, rows, sl] * y.astype(BF16)
        wv = (vc.astype(F32) * dt_ref[2, hd]).astype(BF16)
        state_ref[hd] = s_b * dt_ref[6, hd] + _dot_tn(kc, wv)

    def halo():
        uext_ref[0:HALO] = jnp.where(i < n_tiles - 1, up_ref[0].astype(F32), 0.0)
        uext_ref[HALO:HALO + tm] = u_ref[0].astype(F32)
        uext_ref[HALO + tm:2 * HALO + tm] = jnp.where(i > 0, un_ref[0].astype(F32), 0.0)

    def conv(r):
        blocks = []
        for cb in range(CONV_CH // LANES):
            cols = slice(cb * LANES, (cb + 1) * LANES)
            acc = None
            for sub in range(SUBLANES):
                part = None
                for t in range(sub, CONV_K, SUBLANES):
                    start = r * CONV_ROWS + t - sub
                    term = uext_ref[start:start + CONV_ROWS + SUBLANES, cols] * cw_ref[t:t + 1, cols]
                    part = term if part is None else part + term
                shifted = part[sub + 1:sub + 1 + CONV_ROWS]
                acc = shifted if acc is None else acc + shifted
            blocks.append(acc)
        cv = _silu(_layer_norm(jnp.concatenate(blocks, axis=-1), lnw_ref[...], lnb_ref[...]))
        mix_ref[r * CONV_ROWS:(r + 1) * CONV_ROWS, RET_W:RET_W + CONV_CH] = cv.astype(BF16)

    def out_proj(n):
        cols = slice(n * OUT_BLOCK, (n + 1) * OUT_BLOCK)
        out = _dot(mix_ref[...], wo_ref[:, cols])
        xo_ref[:, cols] = x_ref[0, :, cols] + mod_ref[b * 6 + 2][:, cols] * out

    pieces = [halo]
    convs = [functools.partial(conv, r) for r in range(tm // CONV_ROWS)]
    rets = [functools.partial(retention, c, hd) for c in reversed(range(tm // CHUNK)) for hd in range(HEADS)]
    per = len(rets) // len(convs)
    for r, cv in enumerate(convs):
        pieces += rets[r * per:(r + 1) * per] + [cv]
    pieces += rets[len(convs) * per:]
    pieces += [functools.partial(out_proj, n) for n in range(D_MODEL // OUT_BLOCK)]
    return pieces


def _ffn_pieces(xs_ref, o_ref, b, mod_ref, nw_ref, wg_ref, wu_ref, wd_ref, fnw_ref, h_ref, act_ref, final_norm):
    def prologue():
        x32 = xs_ref[...]
        o_ref[...] = x32
        h_ref[...] = _rms_mod(x32, nw_ref[...], mod_ref[b * 6 + 3], mod_ref[b * 6 + 4]).astype(BF16)

    def gate_up(j):
        cols = slice(j * FF_BLOCK, (j + 1) * FF_BLOCK)
        h = h_ref[...]
        act_ref[:, cols] = (_silu(_dot(h, wg_ref[:, cols])) * _dot(h, wu_ref[:, cols])).astype(BF16)

    def down(n):
        cols = slice(n * OUT_BLOCK, (n + 1) * OUT_BLOCK)
        o_ref[:, cols] = o_ref[:, cols] + mod_ref[b * 6 + 5][:, cols] * _dot(act_ref[...], wd_ref[:, cols])

    def final():
        y = o_ref[...]
        o_ref[...] = y * lax.rsqrt(jnp.mean(y * y, axis=-1, keepdims=True) + EPS) * fnw_ref[...]

    pieces = [prologue]
    pieces += [functools.partial(gate_up, j) for j in range(D_FF // FF_BLOCK)]
    pieces += [functools.partial(down, n) for n in range(D_MODEL // OUT_BLOCK)]
    if final_norm:
        pieces.append(final)
    return pieces


def _even_mix_kernel(x_ref, q_ref, k_ref, v_ref, gs_ref, u_ref, up_ref, un_ref, sprev_ref, sb0_ref,
                     dt_ref, mod_ref, cw_ref, lnw_ref, lnb_ref, wo_ref,
                     o_ref, sbfin_ref, state_ref, uext_ref, mix_ref, *, tm):
    b = pl.program_id(0)
    i = pl.program_id(1)
    n_tiles = pl.num_programs(1)

    @pl.when(i == 0)
    def _():
        state_ref[...] = sb0_ref[0]

    for piece in _even_mix_pieces(b, i, n_tiles, x_ref, q_ref, k_ref, v_ref, gs_ref, u_ref, up_ref, un_ref,
                                  sprev_ref, dt_ref, mod_ref, cw_ref, lnw_ref, lnb_ref, wo_ref,
                                  state_ref, uext_ref, mix_ref, o_ref.at[0], tm):
        piece()

    @pl.when(i == n_tiles - 1)
    def _():
        sbfin_ref[0] = state_ref[...]


def _even_mix_call(x, q, k, v, gs, u, sprev, sb0, dtab, mod, conv_w, ln_w, ln_b, w_out, tm):
    bsz, l, d = x.shape
    n_tiles = l // tm
    cpt = tm // CHUNK
    hpt = tm // HALO
    n_halo = l // HALO
    rev = lambda i: n_tiles - 1 - i
    tok = lambda w: pl.BlockSpec((1, tm, w), lambda b, i: (b, rev(i), 0))
    state_spec = pl.BlockSpec((1, HEADS, CHUNK, CHUNK), lambda b, i: (b, 0, 0, 0))
    return pl.pallas_call(
        functools.partial(_even_mix_kernel, tm=tm),
        grid=(bsz, n_tiles),
        in_specs=[tok(d), tok(RET_W), tok(RET_W), tok(RET_W), tok(RET_W), tok(CONV_CH),
                  pl.BlockSpec((1, HALO, CONV_CH), lambda b, i: (b, jnp.maximum(rev(i) * hpt - 1, 0), 0)),
                  pl.BlockSpec((1, HALO, CONV_CH),
                               lambda b, i: (b, jnp.minimum((rev(i) + 1) * hpt, n_halo - 1), 0)),
                  pl.BlockSpec((1, cpt, HEADS, CHUNK, CHUNK), lambda b, i: (b, rev(i), 0, 0, 0)),
                  state_spec,
                  _const_spec(dtab.shape),
                  _const_spec(mod.shape),
                  _const_spec(conv_w.shape),
                  _const_spec((1, CONV_CH)),
                  _const_spec((1, CONV_CH)),
                  _const_spec(w_out.shape)],
        out_specs=[tok(d), state_spec],
        out_shape=[jax.ShapeDtypeStruct((bsz, l, d), F32),
                   jax.ShapeDtypeStruct((bsz, HEADS, CHUNK, CHUNK), F32)],
        scratch_shapes=[pltpu.VMEM((HEADS, CHUNK, CHUNK), F32),
                        pltpu.VMEM((tm + 2 * HALO, CONV_CH), F32),
                        pltpu.VMEM((tm, RET_W + CONV_CH), BF16)],
        compiler_params=_params(("arbitrary", "arbitrary")),
        name="even_mix",
    )(x, q, k, v, gs, u, u, u, sprev, sb0, dtab, mod, conv_w, ln_w, ln_b, w_out)


def _odd_kernel(x_ref, xp_ref, xn_ref, mod_ref, nw_ref, wi_ref, pw_ref, ps_ref, lnw_ref, lnb_ref,
                sgw_ref, sgb_ref, wo_ref, o_ref, h_ref, pcx_ref, mix_ref, *, tm, seq_len):
    b = pl.program_id(0)
    i = pl.program_id(1)
    n_tiles = pl.num_programs(1)
    nw = nw_ref[...]
    shift = mod_ref[b * 6 + 0]
    scale = mod_ref[b * 6 + 1]

    h_ref[0:HALO] = _rms_mod(xp_ref[0], nw, shift, scale).astype(BF16)
    h_ref[HALO:HALO + tm] = _rms_mod(x_ref[0], nw, shift, scale).astype(BF16)
    h_ref[HALO + tm:2 * HALO + tm] = _rms_mod(xn_ref[0], nw, shift, scale).astype(BF16)

    pcx_ref[0:tm + 2 * HALO] = _dot(h_ref[...], wi_ref[:, 0:POOL_CH])
    pcx_ref[0:HALO] = jnp.where(i > 0, pcx_ref[0:HALO], 0.0)
    pcx_ref[HALO + tm:2 * HALO + tm] = jnp.where(i < n_tiles - 1, pcx_ref[HALO + tm:2 * HALO + tm], 0.0)
    pcx_ref[tm + 2 * HALO:tm + 2 * HALO + SUBLANES] = jnp.zeros((SUBLANES, POOL_CH), F32)

    sub = min(tm, ODD_ROWS)
    for r0 in range(0, tm, sub):
        rows = slice(r0, r0 + sub)
        pos = i * tm + r0 + lax.broadcasted_iota(jnp.int32, (sub, POOL_GC), 0)
        for gi, w in enumerate(POOL_WINDOWS):
            left = w // 2
            right = w - 1 - left
            sl = slice(gi * POOL_GC, (gi + 1) * POOL_GC)
            run = pcx_ref[r0:r0 + sub + 2 * HALO + SUBLANES, sl]
            span = 1
            while span < w:
                keep = run.shape[0] - SUBLANES
                run = run[0:keep] + run[span:span + keep]
                span *= 2
            win = run[HALO - left:HALO - left + sub]
            cnt = jnp.minimum(pos + right, seq_len - 1) - jnp.maximum(pos - left, 0) + 1
            m = win / cnt.astype(F32) - pcx_ref[HALO + r0:HALO + r0 + sub, sl]
            po = _dot(m.astype(BF16), pw_ref[gi]) * ps_ref[:, sl]
            mix_ref[rows, sl] = po.astype(BF16)

        pd = _dot(h_ref[HALO + r0:HALO + r0 + sub], wi_ref[:, POOL_CH:ODD_IN])
        z = 0.5 * pd * (1.0 + lax.erf(pd * (2.0 ** -0.5)))
        zu = z[:, 0:SG_CH]
        zv = _layer_norm(z[:, SG_CH:2 * SG_CH], lnw_ref[...], lnb_ref[...]).astype(BF16)
        for c in range(sub // CHUNK):
            crows = slice(c * CHUNK, (c + 1) * CHUNK)
            for g in range(SG_GROUPS):
                sl = slice(g * SG_GC, (g + 1) * SG_GC)
                s = _dot(sgw_ref[g], zv[crows, sl]) + sgb_ref[g]
                mix_ref[r0 + c * CHUNK:r0 + (c + 1) * CHUNK, POOL_CH + g * SG_GC:POOL_CH + (g + 1) * SG_GC] = (
                    zu[crows, sl] * s).astype(BF16)

        out = _dot(mix_ref[rows], wo_ref[...])
        o_ref[0, rows] = x_ref[0, rows] + mod_ref[b * 6 + 2] * out


def _odd_call(x, mod, nw, w_in, pool_w, pool_scale, ln_w, ln_b, sg_w, sg_b_full, w_out, tm):
    bsz, l, d = x.shape
    hpt = tm // HALO
    n_halo = l // HALO
    tok = pl.BlockSpec((1, tm, d), lambda b, i: (b, i, 0))
    return pl.pallas_call(
        functools.partial(_odd_kernel, tm=tm, seq_len=l),
        grid=(bsz, l // tm),
        in_specs=[tok,
                  pl.BlockSpec((1, HALO, d), lambda b, i: (b, jnp.maximum(i * hpt - 1, 0), 0)),
                  pl.BlockSpec((1, HALO, d), lambda b, i: (b, jnp.minimum((i + 1) * hpt, n_halo - 1), 0)),
                  _const_spec(mod.shape),
                  _const_spec((1, d)),
                  _const_spec(w_in.shape),
                  _const_spec(pool_w.shape),
                  _const_spec((1, POOL_CH)),
                  _const_spec((1, SG_CH)),
                  _const_spec((1, SG_CH)),
                  _const_spec(sg_w.shape),
                  _const_spec(sg_b_full.shape),
                  _const_spec(w_out.shape)],
        out_specs=tok,
        out_shape=jax.ShapeDtypeStruct((bsz, l, d), F32),
        scratch_shapes=[pltpu.VMEM((tm + 2 * HALO, d), BF16),
                        pltpu.VMEM((tm + 2 * HALO + SUBLANES, POOL_CH), F32),
                        pltpu.VMEM((tm, POOL_CH + SG_CH), BF16)],
        compiler_params=_params(("arbitrary", "arbitrary")),
        name="odd_mix",
    )(x, x, x, mod, nw, w_in, pool_w, pool_scale, ln_w, ln_b, sg_w, sg_b_full, w_out)


def _ffn_kernel(x_ref, mod_ref, nw_ref, wg_ref, wu_ref, wd_ref, fnw_ref, *rest, n_cast, n_ada, final_norm):
    cast_in, rest = rest[:n_cast], rest[n_cast:]
    ada_in, rest = rest[:3 * n_ada], rest[3 * n_ada:]
    o_ref, rest = rest[0], rest[1:]
    cast_out, rest = rest[:n_cast], rest[n_cast:]
    ada_out, (h_ref, act_ref) = rest[:n_ada], rest[n_ada:]
    _cast_bands(cast_in, cast_out)
    if n_ada:
        c_ref, aw_ref, ab_ref = ada_in
        mo_ref, = ada_out

        @pl.when(jnp.logical_and(pl.program_id(0) == 0, pl.program_id(1) == 0))
        def _():
            mo_ref[...] = jnp.broadcast_to(ab_ref[...], mo_ref.shape)

        mo_ref[...] += _dot(_silu(c_ref[...]).astype(BF16), aw_ref[...].astype(BF16))
    tm = x_ref.shape[1]
    subs = []
    for r0 in range(0, tm, min(tm, FFN_ROWS)):
        rows = pl.ds(r0, min(tm, FFN_ROWS))
        subs.append(_ffn_pieces(x_ref.at[0, rows], o_ref.at[0, rows], pl.program_id(0), mod_ref, nw_ref,
                                wg_ref, wu_ref, wd_ref, fnw_ref, h_ref.at[rows], act_ref.at[rows], final_norm))
    order = [subs[0][0]]
    for k, pieces in enumerate(subs):
        order.append(pieces[1])
        if k + 1 < len(subs):
            order.append(subs[k + 1][0])
        order += pieces[2:]
    for piece in order:
        piece()


def _ffn_call(x, mod, nw, wg, wu, wd, fnw, tm, final_norm, cast=(), ada=None):
    bsz, l, d = x.shape
    n_tiles = l // tm
    n_steps = bsz * n_tiles
    tok = pl.BlockSpec((1, tm, d), lambda b, i: (b, i, 0))
    specs = [_cast_band_specs(stack, layer, bsz, n_tiles) for stack, layer in cast]
    ada_ops, ada_in_specs, ada_out_specs, ada_out_shape = [], [], [], []
    if ada is not None:
        c_rows, ada_w, ada_b, layer = ada
        n_mod = ada_w.shape[2]
        band = d // n_steps
        assert d % n_steps == 0 and band % SUBLANES == 0
        c_bands = c_rows.reshape(MOD_ROWS, n_steps, band).transpose(1, 0, 2)
        ada_ops = [c_bands, ada_w, ada_b.reshape(ada_b.shape[0], 1, n_mod)]
        ada_in_specs = [pl.BlockSpec((None, MOD_ROWS, band), lambda b, i: (b * n_tiles + i, 0, 0)),
                        pl.BlockSpec((None, band, n_mod), lambda b, i: (layer, b * n_tiles + i, 0)),
                        pl.BlockSpec((None, 1, n_mod), lambda b, i: (layer, 0, 0))]
        ada_out_specs = [pl.BlockSpec((MOD_ROWS, n_mod), lambda b, i: (0, 0))]
        ada_out_shape = [jax.ShapeDtypeStruct((MOD_ROWS, n_mod), F32)]
    outs = pl.pallas_call(
        functools.partial(_ffn_kernel, n_cast=len(cast), n_ada=len(ada_out_specs), final_norm=final_norm),
        grid=(bsz, n_tiles),
        in_specs=[tok,
                  _const_spec(mod.shape),
                  _const_spec((1, d)),
                  _const_spec(wg.shape),
                  _const_spec(wu.shape),
                  _const_spec(wd.shape),
                  _const_spec((1, d))] + [s[0] for s in specs] + ada_in_specs,
        out_specs=[tok] + [s[1] for s in specs] + ada_out_specs,
        out_shape=[jax.ShapeDtypeStruct((bsz, l, d), F32)]
        + [jax.ShapeDtypeStruct(stack.shape[1:], BF16) for stack, _ in cast] + ada_out_shape,
        scratch_shapes=[pltpu.VMEM((tm, d), BF16), pltpu.VMEM((tm, D_FF), BF16)],
        compiler_params=_params(("arbitrary", "arbitrary")),
        name="ffn",
    )(x, mod, nw, wg, wu, wd, fnw, *[stack for stack, _ in cast], *ada_ops)
    n_cast = len(cast)
    return outs[0], tuple(outs[1:1 + n_cast]), (outs[1 + n_cast] if ada is not None else None)


def _rope_tables(p_seq, p_row, p_col):
    parts = []
    for p, n in zip((p_seq, p_row, p_col), ROPE_PAIRS):
        freq = ROPE_BASE ** (-jnp.arange(n, dtype=F32) / n)
        parts.append(p[:, None] * freq[None, :])
    ang = jnp.concatenate(parts, axis=-1)
    cos, sin = jnp.cos(ang), jnp.sin(ang)
    return jnp.concatenate([cos, cos], axis=-1), jnp.concatenate([-sin, sin], axis=-1)


def _tile(l):
    return min(l, 1024)


def kernel(x, c, ctx, c_ctx, ada_w, ada_b, norm_w, even_w_in, even_w_out, ret_decay_logit, conv_dw_w, conv_ln_w, conv_ln_b, odd_w_in, odd_w_out, pool_w, pool_scale, sg_ln_w, sg_ln_b, sg_w, sg_b, ffn_w_gate, ffn_w_up, ffn_w_down, final_norm_w):
    bsz, l, d = x.shape
    lc = ctx.shape[1]
    assert d == D_MODEL and l % 256 == 0 and lc % CHUNK == 0 and bsz + 1 <= MOD_ROWS
    rows = l // GRID_W
    grid_r = jnp.broadcast_to(jnp.arange(rows, dtype=F32)[:, None], (rows, GRID_W)).reshape(-1)
    grid_c = jnp.broadcast_to(jnp.arange(GRID_W, dtype=F32)[None, :], (rows, GRID_W)).reshape(-1)
    cos_x, sin_x = _rope_tables(jnp.full((l,), lc, F32), grid_r, grid_c)
    zeros_c = jnp.zeros((lc,), F32)
    cos_c, sin_c = _rope_tables(jnp.arange(lc, dtype=F32), zeros_c, zeros_c)

    cvec = jnp.concatenate([c, c_ctx[None, :], jnp.zeros((MOD_ROWS - bsz - 1, d), F32)], axis=0)
    mod_rows = _mod_call(cvec, ada_w, ada_b, 0)
    dtabs = _decay_call(ret_decay_logit)
    zero_state = jnp.zeros((bsz, HEADS, CHUNK, CHUNK), F32)
    fnw = final_norm_w.reshape(1, d)
    tx, tc = _tile(l), _tile(lc)

    def layer_matrices(i):
        mix_in, mix_out = (even_w_in, even_w_out) if i % 2 == 0 else (odd_w_in, odd_w_out)
        return (mix_in, i // 2), (mix_out, i // 2), (ffn_w_gate, i), (ffn_w_up, i), (ffn_w_down, i)

    first = layer_matrices(0)
    bf16_mats = (first[0][0][first[0][1]].astype(BF16),) + (None,) * (len(first) - 1)
    for i in range(DEPTH):
        j = i // 2
        even = i % 2 == 0
        ctx_after = any(m % 2 == 0 for m in range(i + 1, DEPTH))
        mod_x = mod_rows[:bsz].reshape(bsz * 6, 1, d)
        mod_c = jnp.broadcast_to(mod_rows[bsz].reshape(1, 6, 1, d), (bsz, 6, 1, d)).reshape(bsz * 6, 1, d)
        nw1 = norm_w[i, 0].reshape(1, d)
        nw2 = norm_w[i, 1].reshape(1, d)
        w_in, w_out, wg, wu, wd = bf16_mats
        if even:
            conv_w = conv_dw_w[j]
            lnw = conv_ln_w[j].reshape(1, CONV_CH)
            lnb = conv_ln_b[j].reshape(1, CONV_CH)
            dtab = dtabs[j]
            if ctx_after:
                *mix_c, s_f, _ = _even_in_call(ctx, mod_c, nw1, w_in, cos_c, sin_c, zero_state, dtab, tc)
            else:
                s_f, s_b = _ctx_state_call(ctx, mod_c, nw1, w_in[:, RET_W:3 * RET_W], cos_c, sin_c, dtab)
            pending = layer_matrices(i)[1:] if w_out is None else ()
            *mix_x, _, cast_now = _even_in_call(x, mod_x, nw1, w_in, cos_x, sin_x, s_f, dtab, tx, pending)
            if pending:
                w_out, wg, wu, wd = cast_now
            if ctx_after:
                ctx, s_b = _even_mix_call(ctx, *mix_c, zero_state, dtab, mod_c, conv_w, lnw, lnb, w_out, tc)
            x, _ = _even_mix_call(x, *mix_x, s_b, dtab, mod_x, conv_w, lnw, lnb, w_out, tx)
        else:
            args = (nw1, w_in, pool_w[j].astype(BF16), pool_scale[j].reshape(1, POOL_CH),
                    sg_ln_w[j].reshape(1, SG_CH), sg_ln_b[j].reshape(1, SG_CH), sg_w[j].astype(BF16),
                    jnp.broadcast_to(sg_b[j][:, :, None], (SG_GROUPS, CHUNK, SG_GC)), w_out)
            if ctx_after:
                ctx = _odd_call(ctx, mod_c, *args, tc)
            x = _odd_call(x, mod_x, *args, tx)
        if ctx_after:
            ctx, _, _ = _ffn_call(ctx, mod_c, nw2, wg, wu, wd, fnw, tc, False)
        later = layer_matrices(i + 1) if i + 1 < DEPTH else ()
        ada_next = (cvec, ada_w, ada_b, i + 1) if i + 1 < DEPTH else None
        x, bf16_mats, mod_rows = _ffn_call(x, mod_x, nw2, wg, wu, wd, fnw, min(l, 2 * FFN_ROWS), i == DEPTH - 1,
                                           later, ada_next)
    return x
```

```python
import functools

import jax
import jax.numpy as jnp
from jax import lax
from jax.experimental import pallas as pl
from jax.experimental.pallas import tpu as pltpu

F32 = jnp.float32
BF16 = jnp.bfloat16

D_MODEL = 1024
DEPTH = 4
GRID_W = 64
EPS = 1e-6
HEADS = 4
HEAD_DIM = 128
RET_W = HEADS * HEAD_DIM
CHUNK = 128
ROPE_BASE = 10000.0
ROPE_PAIRS = (HEAD_DIM // 8, 3 * HEAD_DIM // 16, 3 * HEAD_DIM // 16)
CONV_CH = 512
CONV_K = 31
CONV_PAD = CONV_K // 2
EVEN_IN = 4 * RET_W + 2 * CONV_CH
POOL_CH = 512
POOL_WINDOWS = (2, 4, 8, 16)
POOL_GC = 128
SG_CH = 512
SG_GROUPS = 4
SG_GC = 128
ODD_IN = POOL_CH + 2 * SG_CH
D_FF = 2816
SUBLANES = 8
LANES = 128
FF_BLOCK = 256
OUT_BLOCK = 256
CONV_ROWS = 128
FFN_ROWS = 512
ODD_ROWS = 256
HALO = 16
N_DECAY_TABLES = 7
MOD_ROWS = 8
VMEM_LIMIT = 56 * 1024 * 1024


def _dot(a, b):
    return jnp.dot(a, b, preferred_element_type=F32)


def _dot_nt(a, b):
    return lax.dot_general(a, b, (((1,), (1,)), ((), ())), preferred_element_type=F32)


def _dot_tn(a, b):
    return lax.dot_general(a, b, (((0,), (0,)), ((), ())), preferred_element_type=F32)


def _silu(x):
    return x * jax.nn.sigmoid(x)


def _rms_mod(x32, nw, shift, scale):
    y = x32 * lax.rsqrt(jnp.mean(x32 * x32, axis=-1, keepdims=True) + EPS) * nw
    return y * (1.0 + scale) + shift


def _layer_norm(x32, w, b):
    mu = jnp.mean(x32, axis=-1, keepdims=True)
    xc = x32 - mu
    return xc * lax.rsqrt(jnp.mean(xc * xc, axis=-1, keepdims=True) + EPS) * w + b


def _const_spec(shape):
    return pl.BlockSpec(shape, lambda *_: (0,) * len(shape), pipeline_mode=pl.Buffered(1))


def _params(sem):
    return pltpu.CompilerParams(dimension_semantics=sem, vmem_limit_bytes=VMEM_LIMIT)


def _cast_band_specs(stack, layer, bsz, n_tiles):
    _, rows, cols = stack.shape
    n_steps = bsz * n_tiles
    hold = next(h for h in range(1, n_steps + 1)
                if n_steps % h == 0 and rows % (n_steps // h) == 0 and (rows * h // n_steps) % 16 == 0)
    band = rows * hold // n_steps
    return (pl.BlockSpec((None, band, cols), lambda b, i: (layer, (b * n_tiles + i) // hold, 0)),
            pl.BlockSpec((band, cols), lambda b, i: ((b * n_tiles + i) // hold, 0)))


def _cast_bands(srcs, dsts):
    for src, dst in zip(srcs, dsts):
        dst[...] = src[...].astype(BF16)


def _mod_kernel(c_ref, w_ref, b_ref, o_ref):
    o_ref[...] = _dot(_silu(c_ref[...]).astype(BF16), w_ref[...].astype(BF16)) + b_ref[...]


def _mod_call(c_rows, ada_w, ada_b, layer):
    depth, d, n = ada_w.shape
    tn = 1024
    return pl.pallas_call(
        _mod_kernel,
        grid=(n // tn,),
        in_specs=[_const_spec((MOD_ROWS, d)),
                  pl.BlockSpec((None, d, tn), lambda j: (layer, 0, j)),
                  pl.BlockSpec((None, 1, tn), lambda j: (layer, 0, j))],
        out_specs=pl.BlockSpec((MOD_ROWS, tn), lambda j: (0, j)),
        out_shape=jax.ShapeDtypeStruct((MOD_ROWS, n), F32),
        compiler_params=_params(("arbitrary",)),
        name="adaln_mod",
    )(c_rows, ada_w, ada_b.reshape(depth, 1, n))


def _log_sigmoid(x):
    return -(jnp.maximum(-x, 0.0) + jnp.log1p(jnp.exp(-jnp.abs(x))))


def _decay_kernel(logit_ref, o_ref):
    j = pl.program_id(0)
    row = lax.broadcasted_iota(jnp.int32, (CHUNK, CHUNK), 0).astype(F32)
    col = lax.broadcasted_iota(jnp.int32, (CHUNK, CHUNK), 1).astype(F32)
    diff = row - col
    lower = diff >= 0.0
    for hd in range(HEADS):
        lgf = _log_sigmoid(jnp.full((CHUNK, CHUNK), logit_ref[(j * 2 + 0) * HEADS + hd], F32))
        lgb = _log_sigmoid(jnp.full((CHUNK, CHUNK), logit_ref[(j * 2 + 1) * HEADS + hd], F32))
        o_ref[0, 0, hd] = jnp.where(lower, jnp.exp(lgf * jnp.where(lower, diff, 0.0)),
                                    jnp.exp(lgb * jnp.where(lower, 0.0, -diff - 1.0)))
        o_ref[0, 1, hd] = jnp.exp(lgf * (CHUNK - 1.0 - row))
        o_ref[0, 2, hd] = jnp.exp(lgb * row)
        o_ref[0, 3, hd] = jnp.exp(lgf * (row + 1.0))
        o_ref[0, 4, hd] = jnp.exp(lgb * (CHUNK - 1.0 - row))
        o_ref[0, 5, hd] = jnp.exp(lgf * float(CHUNK))
        o_ref[0, 6, hd] = jnp.exp(lgb * float(CHUNK))


def _decay_call(ret_decay_logit):
    n_even = ret_decay_logit.shape[0]
    return pl.pallas_call(
        _decay_kernel,
        grid=(n_even,),
        in_specs=[pl.BlockSpec(memory_space=pltpu.SMEM)],
        out_specs=pl.BlockSpec((1, N_DECAY_TABLES, HEADS, CHUNK, CHUNK), lambda j: (j, 0, 0, 0, 0)),
        out_shape=jax.ShapeDtypeStruct((n_even, N_DECAY_TABLES, HEADS, CHUNK, CHUNK), F32),
        compiler_params=_params(("arbitrary",)),
        name="ret_decay",
    )(ret_decay_logit.reshape(-1).astype(F32))


def _even_in_kernel(x_ref, mod_ref, nw_ref, w_ref, cos_ref, sin_ref, s0_ref, dt_ref, *rest, n_cast, tm):
    cast_in = rest[:n_cast]
    q_ref, k_ref, v_ref, gs_ref, u_ref, sprev_ref, sfin_ref = rest[n_cast:n_cast + 7]
    cast_out = rest[n_cast + 7:2 * n_cast + 7]
    state_ref = rest[2 * n_cast + 7]
    b = pl.program_id(0)
    i = pl.program_id(1)
    _cast_bands(cast_in, cast_out)

    @pl.when(i == 0)
    def _():
        state_ref[...] = s0_ref[0]

    h = _rms_mod(x_ref[0], nw_ref[...], mod_ref[b * 6 + 0], mod_ref[b * 6 + 1]).astype(BF16)
    cosf = cos_ref[...]
    sinf = sin_ref[...]

    def rope(t):
        return t * cosf + pltpu.roll(t, HEAD_DIM // 2, 1) * sinf

    pq = _dot(h, w_ref[:, 0:RET_W])
    pk = _dot(h, w_ref[:, RET_W:2 * RET_W])
    pv = _dot(h, w_ref[:, 2 * RET_W:3 * RET_W])
    scale = HEAD_DIM ** -0.5
    for hd in range(HEADS):
        sl = slice(hd * HEAD_DIM, (hd + 1) * HEAD_DIM)
        q_ref[0, :, sl] = (rope(pq[:, sl]) * scale).astype(BF16)
        kr = rope(pk[:, sl]).astype(BF16)
        k_ref[0, :, sl] = kr
        vh = pv[:, sl]
        v_ref[0, :, sl] = vh.astype(BF16)
        zeta = dt_ref[1, hd]
        cdec = dt_ref[5, hd]
        for c in range(tm // CHUNK):
            rows = slice(c * CHUNK, (c + 1) * CHUNK)
            s_cur = state_ref[hd]
            sprev_ref[0, c, hd] = s_cur.astype(BF16)
            wv = (vh[rows] * zeta).astype(BF16)
            state_ref[hd] = s_cur * cdec + _dot_tn(kr[rows], wv)

    pg = _dot(h, w_ref[:, 3 * RET_W:4 * RET_W])
    gs_ref[0] = _silu(pg).astype(BF16)
    pa = _dot(h, w_ref[:, 4 * RET_W:4 * RET_W + CONV_CH])
    pb = _dot(h, w_ref[:, 4 * RET_W + CONV_CH:EVEN_IN])
    u_ref[0] = (pa * jax.nn.sigmoid(pb)).astype(BF16)

    @pl.when(i == pl.num_programs(1) - 1)
    def _():
        sfin_ref[0] = state_ref[...]


def _even_in_call(x, mod, nw, w_in, cosf, sinf, s0, dtab, tm, cast=()):
    bsz, l, d = x.shape
    n_chunks = l // CHUNK
    cpt = tm // CHUNK
    tok = lambda w: pl.BlockSpec((1, tm, w), lambda b, i: (b, i, 0))
    state_spec = pl.BlockSpec((1, HEADS, CHUNK, CHUNK), lambda b, i: (b, 0, 0, 0))
    specs = [_cast_band_specs(stack, layer, bsz, l // tm) for stack, layer in cast]
    outs = pl.pallas_call(
        functools.partial(_even_in_kernel, n_cast=len(cast), tm=tm),
        grid=(bsz, l // tm),
        in_specs=[tok(d),
                  _const_spec(mod.shape),
                  _const_spec((1, d)),
                  _const_spec(w_in.shape),
                  pl.BlockSpec((tm, HEAD_DIM), lambda b, i: (i, 0)),
                  pl.BlockSpec((tm, HEAD_DIM), lambda b, i: (i, 0)),
                  state_spec,
                  _const_spec(dtab.shape)] + [s[0] for s in specs],
        out_specs=[tok(RET_W), tok(RET_W), tok(RET_W), tok(RET_W), tok(CONV_CH),
                   pl.BlockSpec((1, cpt, HEADS, CHUNK, CHUNK), lambda b, i: (b, i, 0, 0, 0)),
                   state_spec] + [s[1] for s in specs],
        out_shape=[jax.ShapeDtypeStruct((bsz, l, RET_W), BF16),
                   jax.ShapeDtypeStruct((bsz, l, RET_W), BF16),
                   jax.ShapeDtypeStruct((bsz, l, RET_W), BF16),
                   jax.ShapeDtypeStruct((bsz, l, RET_W), BF16),
                   jax.ShapeDtypeStruct((bsz, l, CONV_CH), BF16),
                   jax.ShapeDtypeStruct((bsz, n_chunks, HEADS, CHUNK, CHUNK), BF16),
                   jax.ShapeDtypeStruct((bsz, HEADS, CHUNK, CHUNK), F32)]
        + [jax.ShapeDtypeStruct(stack.shape[1:], BF16) for stack, _ in cast],
        scratch_shapes=[pltpu.VMEM((HEADS, CHUNK, CHUNK), F32)],
        compiler_params=_params(("arbitrary", "arbitrary")),
        name="even_in",
    )(x, mod, nw, w_in, cosf, sinf, s0, dtab, *[stack for stack, _ in cast])
    return (*outs[:7], tuple(outs[7:]))


def _ctx_state_kernel(x_ref, mod_ref, nw_ref, w_ref, cos_ref, sin_ref, dt_ref, sf_ref, sb_ref):
    b = pl.program_id(0)
    h = _rms_mod(x_ref[0], nw_ref[...], mod_ref[b * 6 + 0], mod_ref[b * 6 + 1]).astype(BF16)
    cosf = cos_ref[...]
    sinf = sin_ref[...]
    pk = _dot(h, w_ref[:, 0:RET_W])
    pv = _dot(h, w_ref[:, RET_W:2 * RET_W])
    n_chunks = x_ref.shape[1] // CHUNK
    for hd in range(HEADS):
        sl = slice(hd * HEAD_DIM, (hd + 1) * HEAD_DIM)
        t = pk[:, sl]
        kr = (t * cosf + pltpu.roll(t, HEAD_DIM // 2, 1) * sinf).astype(BF16)
        vh = pv[:, sl]
        s_f = jnp.zeros((CHUNK, CHUNK), F32)
        s_b = jnp.zeros((CHUNK, CHUNK), F32)
        for c in range(n_chunks):
            rows = slice(c * CHUNK, (c + 1) * CHUNK)
            s_f = s_f * dt_ref[5, hd] + _dot_tn(kr[rows], (vh[rows] * dt_ref[1, hd]).astype(BF16))
        for c in reversed(range(n_chunks)):
            rows = slice(c * CHUNK, (c + 1) * CHUNK)
            s_b = s_b * dt_ref[6, hd] + _dot_tn(kr[rows], (vh[rows] * dt_ref[2, hd]).astype(BF16))
        sf_ref[0, hd] = s_f
        sb_ref[0, hd] = s_b


def _ctx_state_call(ctx, mod, nw, w_kv, cosf, sinf, dtab):
    bsz, lc, d = ctx.shape
    state_spec = pl.BlockSpec((1, HEADS, CHUNK, CHUNK), lambda b: (b, 0, 0, 0))
    return pl.pallas_call(
        _ctx_state_kernel,
        grid=(bsz,),
        in_specs=[pl.BlockSpec((1, lc, d), lambda b: (b, 0, 0)),
                  _const_spec(mod.shape),
                  _const_spec((1, d)),
                  _const_spec(w_kv.shape),
                  _const_spec(cosf.shape),
                  _const_spec(sinf.shape),
                  _const_spec(dtab.shape)],
        out_specs=[state_spec, state_spec],
        out_shape=[jax.ShapeDtypeStruct((bsz, HEADS, CHUNK, CHUNK), F32)] * 2,
        compiler_params=_params(("arbitrary",)),
        name="ctx_state",
    )(ctx, mod, nw, w_kv, cosf, sinf, dtab)


def _even_mix_pieces(b, i, n_tiles, x_ref, q_ref, k_ref, v_ref, gs_ref, u_ref, up_ref, un_ref, sprev_ref,
                     dt_ref, mod_ref, cw_ref, lnw_ref, lnb_ref, wo_ref, state_ref, uext_ref, mix_ref, xo_ref, tm):
    def retention(c, hd):
        rows = slice(c * CHUNK, (c + 1) * CHUNK)
        sl = slice(hd * HEAD_DIM, (hd + 1) * HEAD_DIM)
        qc = q_ref[0, rows, sl]
        kc = k_ref[0, rows, sl]
        vc = v_ref[0, rows, sl]
        s_b = state_ref[hd]
        p = (_dot_nt(qc, kc) * dt_ref[0, hd]).astype(BF16)
        y = _dot(p, vc)
        y = y + _dot(qc, sprev_ref[0, c, hd]) * dt_ref[3, hd]
        y = y + _dot(qc, s_b.astype(BF16)) * dt_ref[4, hd]
        y = y * lax.rsqrt(jnp.mean(y * y, axis=-1, keepdims=True) + EPS)
        mix_ref[rows, sl] = gs_ref[0, rows, sl] * y.astype(BF16)
        wv = (vc.astype(F32) * dt_ref[2, hd]).astype(BF16)
        state_ref[hd] = s_b * dt_ref[6, hd] + _dot_tn(kc, wv)

    def halo():
        uext_ref[0:HALO] = jnp.where(i < n_tiles - 1, up_ref[0].astype(F32), 0.0)
        uext_ref[HALO:HALO + tm] = u_ref[0].astype(F32)
        uext_ref[HALO + tm:2 * HALO + tm] = jnp.where(i > 0, un_ref[0].astype(F32), 0.0)

    def conv(r):
        blocks = []
        for cb in range(CONV_CH // LANES):
            cols = slice(cb * LANES, (cb + 1) * LANES)
            acc = None
            for sub in range(SUBLANES):
                part = None
                for t in range(sub, CONV_K, SUBLANES):
                    start = r * CONV_ROWS + t - sub
                    term = uext_ref[start:start + CONV_ROWS + SUBLANES, cols] * cw_ref[t:t + 1, cols]
                    part = term if part is None else part + term
                shifted = pltpu.roll(part, CONV_ROWS + SUBLANES - (sub + 1), 0)[0:CONV_ROWS]
                acc = shifted if acc is None else acc + shifted
            blocks.append(acc)
        cv = _silu(_layer_norm(jnp.concatenate(blocks, axis=-1), lnw_ref[...], lnb_ref[...]))
        mix_ref[r * CONV_ROWS:(r + 1) * CONV_ROWS, RET_W:RET_W + CONV_CH] = cv.astype(BF16)

    def out_proj(n):
        cols = slice(n * OUT_BLOCK, (n + 1) * OUT_BLOCK)
        out = _dot(mix_ref[...], wo_ref[:, cols])
        xo_ref[:, cols] = x_ref[0, :, cols] + mod_ref[b * 6 + 2][:, cols] * out

    pieces = [halo]
    convs = [functools.partial(conv, r) for r in range(tm // CONV_ROWS)]
    rets = [functools.partial(retention, c, hd) for c in reversed(range(tm // CHUNK)) for hd in range(HEADS)]
    per = len(rets) // len(convs)
    for r, cv in enumerate(convs):
        pieces += rets[r * per:(r + 1) * per] + [cv]
    pieces += rets[len(convs) * per:]
    pieces += [functools.partial(out_proj, n) for n in range(D_MODEL // OUT_BLOCK)]
    return pieces


def _ffn_pieces(xs_ref, o_ref, b, mod_ref, nw_ref, wg_ref, wu_ref, wd_ref, fnw_ref, h_ref, act_ref, final_norm):
    def prologue():
        x32 = xs_ref[...]
        o_ref[...] = x32
        h_ref[...] = _rms_mod(x32, nw_ref[...], mod_ref[b * 6 + 3], mod_ref[b * 6 + 4]).astype(BF16)

    def gate_up(j):
        cols = slice(j * FF_BLOCK, (j + 1) * FF_BLOCK)
        h = h_ref[...]
        act_ref[:, cols] = (_silu(_dot(h, wg_ref[:, cols])) * _dot(h, wu_ref[:, cols])).astype(BF16)

    def down(n):
        cols = slice(n * OUT_BLOCK, (n + 1) * OUT_BLOCK)
        o_ref[:, cols] = o_ref[:, cols] + mod_ref[b * 6 + 5][:, cols] * _dot(act_ref[...], wd_ref[:, cols])

    def final():
        y = o_ref[...]
        o_ref[...] = y * lax.rsqrt(jnp.mean(y * y, axis=-1, keepdims=True) + EPS) * fnw_ref[...]

    pieces = [prologue]
    pieces += [functools.partial(gate_up, j) for j in range(D_FF // FF_BLOCK)]
    pieces += [functools.partial(down, n) for n in range(D_MODEL // OUT_BLOCK)]
    if final_norm:
        pieces.append(final)
    return pieces


def _even_mix_kernel(x_ref, q_ref, k_ref, v_ref, gs_ref, u_ref, up_ref, un_ref, sprev_ref, sb0_ref,
                     dt_ref, mod_ref, cw_ref, lnw_ref, lnb_ref, wo_ref,
                     o_ref, sbfin_ref, state_ref, uext_ref, mix_ref, *, tm):
    b = pl.program_id(0)
    i = pl.program_id(1)
    n_tiles = pl.num_programs(1)

    @pl.when(i == 0)
    def _():
        state_ref[...] = sb0_ref[0]

    for piece in _even_mix_pieces(b, i, n_tiles, x_ref, q_ref, k_ref, v_ref, gs_ref, u_ref, up_ref, un_ref,
                                  sprev_ref, dt_ref, mod_ref, cw_ref, lnw_ref, lnb_ref, wo_ref,
                                  state_ref, uext_ref, mix_ref, o_ref.at[0], tm):
        piece()

    @pl.when(i == n_tiles - 1)
    def _():
        sbfin_ref[0] = state_ref[...]


def _even_mix_call(x, q, k, v, gs, u, sprev, sb0, dtab, mod, conv_w, ln_w, ln_b, w_out, tm):
    bsz, l, d = x.shape
    n_tiles = l // tm
    cpt = tm // CHUNK
    hpt = tm // HALO
    n_halo = l // HALO
    rev = lambda i: n_tiles - 1 - i
    tok = lambda w: pl.BlockSpec((1, tm, w), lambda b, i: (b, rev(i), 0))
    state_spec = pl.BlockSpec((1, HEADS, CHUNK, CHUNK), lambda b, i: (b, 0, 0, 0))
    return pl.pallas_call(
        functools.partial(_even_mix_kernel, tm=tm),
        grid=(bsz, n_tiles),
        in_specs=[tok(d), tok(RET_W), tok(RET_W), tok(RET_W), tok(RET_W), tok(CONV_CH),
                  pl.BlockSpec((1, HALO, CONV_CH), lambda b, i: (b, jnp.maximum(rev(i) * hpt - 1, 0), 0)),
                  pl.BlockSpec((1, HALO, CONV_CH),
                               lambda b, i: (b, jnp.minimum((rev(i) + 1) * hpt, n_halo - 1), 0)),
                  pl.BlockSpec((1, cpt, HEADS, CHUNK, CHUNK), lambda b, i: (b, rev(i), 0, 0, 0)),
                  state_spec,
                  _const_spec(dtab.shape),
                  _const_spec(mod.shape),
                  _const_spec(conv_w.shape),
                  _const_spec((1, CONV_CH)),
                  _const_spec((1, CONV_CH)),
                  _const_spec(w_out.shape)],
        out_specs=[tok(d), state_spec],
        out_shape=[jax.ShapeDtypeStruct((bsz, l, d), F32),
                   jax.ShapeDtypeStruct((bsz, HEADS, CHUNK, CHUNK), F32)],
        scratch_shapes=[pltpu.VMEM((HEADS, CHUNK, CHUNK), F32),
                        pltpu.VMEM((tm + 2 * HALO, CONV_CH), F32),
                        pltpu.VMEM((tm, RET_W + CONV_CH), BF16)],
        compiler_params=_params(("arbitrary", "arbitrary")),
        name="even_mix",
    )(x, q, k, v, gs, u, u, u, sprev, sb0, dtab, mod, conv_w, ln_w, ln_b, w_out)


def _odd_kernel(x_ref, xp_ref, xn_ref, mod_ref, nw_ref, wi_ref, pw_ref, ps_ref, lnw_ref, lnb_ref,
                sgw_ref, sgb_ref, wo_ref, o_ref, h_ref, pcx_ref, mix_ref, *, tm, seq_len):
    b = pl.program_id(0)
    i = pl.program_id(1)
    n_tiles = pl.num_programs(1)
    nw = nw_ref[...]
    shift = mod_ref[b * 6 + 0]
    scale = mod_ref[b * 6 + 1]

    h_ref[0:HALO] = _rms_mod(xp_ref[0], nw, shift, scale).astype(BF16)
    h_ref[HALO:HALO + tm] = _rms_mod(x_ref[0], nw, shift, scale).astype(BF16)
    h_ref[HALO + tm:2 * HALO + tm] = _rms_mod(xn_ref[0], nw, shift, scale).astype(BF16)

    pcx_ref[0:tm + 2 * HALO] = _dot(h_ref[...], wi_ref[:, 0:POOL_CH])
    pcx_ref[0:HALO] = jnp.where(i > 0, pcx_ref[0:HALO], 0.0)
    pcx_ref[HALO + tm:2 * HALO + tm] = jnp.where(i < n_tiles - 1, pcx_ref[HALO + tm:2 * HALO + tm], 0.0)
    pcx_ref[tm + 2 * HALO:tm + 2 * HALO + SUBLANES] = jnp.zeros((SUBLANES, POOL_CH), F32)

    sub = min(tm, ODD_ROWS)
    for r0 in range(0, tm, sub):
        rows = slice(r0, r0 + sub)
        pos = i * tm + r0 + lax.broadcasted_iota(jnp.int32, (sub, POOL_GC), 0)
        for gi, w in enumerate(POOL_WINDOWS):
            left = w // 2
            right = w - 1 - left
            sl = slice(gi * POOL_GC, (gi + 1) * POOL_GC)
            run = pcx_ref[r0:r0 + sub + 2 * HALO + SUBLANES, sl]
            span = 1
            while span < w:
                n_run = run.shape[0]
                run = (run + pltpu.roll(run, n_run - span, 0))[0:n_run - SUBLANES]
                span *= 2
            win = pltpu.roll(run, run.shape[0] - (HALO - left), 0)[0:sub]
            cnt = jnp.minimum(pos + right, seq_len - 1) - jnp.maximum(pos - left, 0) + 1
            m = win / cnt.astype(F32) - pcx_ref[HALO + r0:HALO + r0 + sub, sl]
            po = _dot(m.astype(BF16), pw_ref[gi]) * ps_ref[:, sl]
            mix_ref[rows, sl] = po.astype(BF16)

        pd = _dot(h_ref[HALO + r0:HALO + r0 + sub], wi_ref[:, POOL_CH:ODD_IN])
        z = 0.5 * pd * (1.0 + lax.erf(pd * (2.0 ** -0.5)))
        zu = z[:, 0:SG_CH]
        zv = _layer_norm(z[:, SG_CH:2 * SG_CH], lnw_ref[...], lnb_ref[...]).astype(BF16)
        for c in range(sub // CHUNK):
            crows = slice(c * CHUNK, (c + 1) * CHUNK)
            for g in range(SG_GROUPS):
                sl = slice(g * SG_GC, (g + 1) * SG_GC)
                s = _dot(sgw_ref[g], zv[crows, sl]) + sgb_ref[g]
                mix_ref[r0 + c * CHUNK:r0 + (c + 1) * CHUNK, POOL_CH + g * SG_GC:POOL_CH + (g + 1) * SG_GC] = (
                    zu[crows, sl] * s).astype(BF16)

        out = _dot(mix_ref[rows], wo_ref[...])
        o_ref[0, rows] = x_ref[0, rows] + mod_ref[b * 6 + 2] * out


def _odd_call(x, mod, nw, w_in, pool_w, pool_scale, ln_w, ln_b, sg_w, sg_b_full, w_out, tm):
    bsz, l, d = x.shape
    hpt = tm // HALO
    n_halo = l // HALO
    tok = pl.BlockSpec((1, tm, d), lambda b, i: (b, i, 0))
    return pl.pallas_call(
        functools.partial(_odd_kernel, tm=tm, seq_len=l),
        grid=(bsz, l // tm),
        in_specs=[tok,
                  pl.BlockSpec((1, HALO, d), lambda b, i: (b, jnp.maximum(i * hpt - 1, 0), 0)),
                  pl.BlockSpec((1, HALO, d), lambda b, i: (b, jnp.minimum((i + 1) * hpt, n_halo - 1), 0)),
                  _const_spec(mod.shape),
                  _const_spec((1, d)),
                  _const_spec(w_in.shape),
                  _const_spec(pool_w.shape),
                  _const_spec((1, POOL_CH)),
                  _const_spec((1, SG_CH)),
                  _const_spec((1, SG_CH)),
                  _const_spec(sg_w.shape),
                  _const_spec(sg_b_full.shape),
                  _const_spec(w_out.shape)],
        out_specs=tok,
        out_shape=jax.ShapeDtypeStruct((bsz, l, d), F32),
        scratch_shapes=[pltpu.VMEM((tm + 2 * HALO, d), BF16),
                        pltpu.VMEM((tm + 2 * HALO + SUBLANES, POOL_CH), F32),
                        pltpu.VMEM((tm, POOL_CH + SG_CH), BF16)],
        compiler_params=_params(("arbitrary", "arbitrary")),
        name="odd_mix",
    )(x, x, x, mod, nw, w_in, pool_w, pool_scale, ln_w, ln_b, sg_w, sg_b_full, w_out)


def _ffn_kernel(x_ref, mod_ref, nw_ref, wg_ref, wu_ref, wd_ref, fnw_ref, *rest, n_cast, n_ada, final_norm):
    cast_in, rest = rest[:n_cast], rest[n_cast:]
    ada_in, rest = rest[:3 * n_ada], rest[3 * n_ada:]
    o_ref, rest = rest[0], rest[1:]
    cast_out, rest = rest[:n_cast], rest[n_cast:]
    ada_out, (h_ref, act_ref) = rest[:n_ada], rest[n_ada:]
    _cast_bands(cast_in, cast_out)
    if n_ada:
        c_ref, aw_ref, ab_ref = ada_in
        mo_ref, = ada_out

        @pl.when(jnp.logical_and(pl.program_id(0) == 0, pl.program_id(1) == 0))
        def _():
            mo_ref[...] = jnp.broadcast_to(ab_ref[...], mo_ref.shape)

        mo_ref[...] += _dot(_silu(c_ref[...]).astype(BF16), aw_ref[...].astype(BF16))
    tm = x_ref.shape[1]
    subs = []
    for r0 in range(0, tm, min(tm, FFN_ROWS)):
        rows = pl.ds(r0, min(tm, FFN_ROWS))
        subs.append(_ffn_pieces(x_ref.at[0, rows], o_ref.at[0, rows], pl.program_id(0), mod_ref, nw_ref,
                                wg_ref, wu_ref, wd_ref, fnw_ref, h_ref.at[rows], act_ref.at[rows], final_norm))
    order = [subs[0][0]]
    for k, pieces in enumerate(subs):
        order.append(pieces[1])
        if k + 1 < len(subs):
            order.append(subs[k + 1][0])
        order += pieces[2:]
    for piece in order:
        piece()


def _ffn_call(x, mod, nw, wg, wu, wd, fnw, tm, final_norm, cast=(), ada=None):
    bsz, l, d = x.shape
    n_tiles = l // tm
    n_steps = bsz * n_tiles
    tok = pl.BlockSpec((1, tm, d), lambda b, i: (b, i, 0))
    specs = [_cast_band_specs(stack, layer, bsz, n_tiles) for stack, layer in cast]
    ada_ops, ada_in_specs, ada_out_specs, ada_out_shape = [], [], [], []
    if ada is not None:
        c_rows, ada_w, ada_b, layer = ada
        n_mod = ada_w.shape[2]
        band = d // n_steps
        assert d % n_steps == 0 and band % SUBLANES == 0
        c_bands = c_rows.reshape(MOD_ROWS, n_steps, band).transpose(1, 0, 2)
        ada_ops = [c_bands, ada_w, ada_b.reshape(ada_b.shape[0], 1, n_mod)]
        ada_in_specs = [pl.BlockSpec((None, MOD_ROWS, band), lambda b, i: (b * n_tiles + i, 0, 0)),
                        pl.BlockSpec((None, band, n_mod), lambda b, i: (layer, b * n_tiles + i, 0)),
                        pl.BlockSpec((None, 1, n_mod), lambda b, i: (layer, 0, 0))]
        ada_out_specs = [pl.BlockSpec((MOD_ROWS, n_mod), lambda b, i: (0, 0))]
        ada_out_shape = [jax.ShapeDtypeStruct((MOD_ROWS, n_mod), F32)]
    outs = pl.pallas_call(
        functools.partial(_ffn_kernel, n_cast=len(cast), n_ada=len(ada_out_specs), final_norm=final_norm),
        grid=(bsz, n_tiles),
        in_specs=[tok,
                  _const_spec(mod.shape),
                  _const_spec((1, d)),
                  _const_spec(wg.shape),
                  _const_spec(wu.shape),
                  _const_spec(wd.shape),
                  _const_spec((1, d))] + [s[0] for s in specs] + ada_in_specs,
        out_specs=[tok] + [s[1] for s in specs] + ada_out_specs,
        out_shape=[jax.ShapeDtypeStruct((bsz, l, d), F32)]
        + [jax.ShapeDtypeStruct(stack.shape[1:], BF16) for stack, _ in cast] + ada_out_shape,
        scratch_shapes=[pltpu.VMEM((tm, d), BF16), pltpu.VMEM((tm, D_FF), BF16)],
        compiler_params=_params(("arbitrary", "arbitrary")),
        name="ffn",
    )(x, mod, nw, wg, wu, wd, fnw, *[stack for stack, _ in cast], *ada_ops)
    n_cast = len(cast)
    return outs[0], tuple(outs[1:1 + n_cast]), (outs[1 + n_cast] if ada is not None else None)


def _rope_tables(p_seq, p_row, p_col):
    parts = []
    for p, n in zip((p_seq, p_row, p_col), ROPE_PAIRS):
        freq = ROPE_BASE ** (-jnp.arange(n, dtype=F32) / n)
        parts.append(p[:, None] * freq[None, :])
    ang = jnp.concatenate(parts, axis=-1)
    cos, sin = jnp.cos(ang), jnp.sin(ang)
    return jnp.concatenate([cos, cos], axis=-1), jnp.concatenate([-sin, sin], axis=-1)


def _tile(l):
    return min(l, 1024)


def kernel(x, c, ctx, c_ctx, ada_w, ada_b, norm_w, even_w_in, even_w_out, ret_decay_logit, conv_dw_w, conv_ln_w, conv_ln_b, odd_w_in, odd_w_out, pool_w, pool_scale, sg_ln_w, sg_ln_b, sg_w, sg_b, ffn_w_gate, ffn_w_up, ffn_w_down, final_norm_w):
    bsz, l, d = x.shape
    lc = ctx.shape[1]
    assert d == D_MODEL and l % 256 == 0 and lc % CHUNK == 0 and bsz + 1 <= MOD_ROWS
    rows = l // GRID_W
    grid_r = jnp.broadcast_to(jnp.arange(rows, dtype=F32)[:, None], (rows, GRID_W)).reshape(-1)
    grid_c = jnp.broadcast_to(jnp.arange(GRID_W, dtype=F32)[None, :], (rows, GRID_W)).reshape(-1)
    cos_x, sin_x = _rope_tables(jnp.full((l,), lc, F32), grid_r, grid_c)
    zeros_c = jnp.zeros((lc,), F32)
    cos_c, sin_c = _rope_tables(jnp.arange(lc, dtype=F32), zeros_c, zeros_c)

    cvec = jnp.concatenate([c, c_ctx[None, :], jnp.zeros((MOD_ROWS - bsz - 1, d), F32)], axis=0)
    mod_rows = _mod_call(cvec, ada_w, ada_b, 0)
    dtabs = _decay_call(ret_decay_logit)
    zero_state = jnp.zeros((bsz, HEADS, CHUNK, CHUNK), F32)
    fnw = final_norm_w.reshape(1, d)
    tx, tc = _tile(l), _tile(lc)

    def layer_matrices(i):
        mix_in, mix_out = (even_w_in, even_w_out) if i % 2 == 0 else (odd_w_in, odd_w_out)
        return (mix_in, i // 2), (mix_out, i // 2), (ffn_w_gate, i), (ffn_w_up, i), (ffn_w_down, i)

    first = layer_matrices(0)
    bf16_mats = (first[0][0][first[0][1]].astype(BF16),) + (None,) * (len(first) - 1)
    for i in range(DEPTH):
        j = i // 2
        even = i % 2 == 0
        ctx_after = any(m % 2 == 0 for m in range(i + 1, DEPTH))
        mod_x = mod_rows[:bsz].reshape(bsz * 6, 1, d)
        mod_c = jnp.broadcast_to(mod_rows[bsz].reshape(1, 6, 1, d), (bsz, 6, 1, d)).reshape(bsz * 6, 1, d)
        nw1 = norm_w[i, 0].reshape(1, d)
        nw2 = norm_w[i, 1].reshape(1, d)
        w_in, w_out, wg, wu, wd = bf16_mats
        if even:
            conv_w = conv_dw_w[j]
            lnw = conv_ln_w[j].reshape(1, CONV_CH)
            lnb = conv_ln_b[j].reshape(1, CONV_CH)
            dtab = dtabs[j]
            if ctx_after:
                *mix_c, s_f, _ = _even_in_call(ctx, mod_c, nw1, w_in, cos_c, sin_c, zero_state, dtab, tc)
            else:
                s_f, s_b = _ctx_state_call(ctx, mod_c, nw1, w_in[:, RET_W:3 * RET_W], cos_c, sin_c, dtab)
            pending = layer_matrices(i)[1:] if w_out is None else ()
            *mix_x, _, cast_now = _even_in_call(x, mod_x, nw1, w_in, cos_x, sin_x, s_f, dtab, tx, pending)
            if pending:
                w_out, wg, wu, wd = cast_now
            if ctx_after:
                ctx, s_b = _even_mix_call(ctx, *mix_c, zero_state, dtab, mod_c, conv_w, lnw, lnb, w_out, tc)
            x, _ = _even_mix_call(x, *mix_x, s_b, dtab, mod_x, conv_w, lnw, lnb, w_out, tx)
        else:
            args = (nw1, w_in, pool_w[j].astype(BF16), pool_scale[j].reshape(1, POOL_CH),
                    sg_ln_w[j].reshape(1, SG_CH), sg_ln_b[j].reshape(1, SG_CH), sg_w[j].astype(BF16),
                    jnp.broadcast_to(sg_b[j][:, :, None], (SG_GROUPS, CHUNK, SG_GC)), w_out)
            if ctx_after:
                ctx = _odd_call(ctx, mod_c, *args, tc)
            x = _odd_call(x, mod_x, *args, tx)
        if ctx_after:
            ctx, _, _ = _ffn_call(ctx, mod_c, nw2, wg, wu, wd, fnw, tc, False)
        later = layer_matrices(i + 1) if i + 1 < DEPTH else ()
        ada_next = (cvec, ada_w, ada_b, i + 1) if i + 1 < DEPTH else None
        x, bf16_mats, mod_rows = _ffn_call(x, mod_x, nw2, wg, wu, wd, fnw, min(l, 2 * FFN_ROWS), i == DEPTH - 1,
                                           later, ada_next)
    return x
```

```python
import functools

import jax
import jax.numpy as jnp
from jax import lax
from jax.experimental import pallas as pl
from jax.experimental.pallas import tpu as pltpu

F32 = jnp.float32
BF16 = jnp.bfloat16

D_MODEL = 1024
DEPTH = 4
GRID_W = 64
EPS = 1e-6
HEADS = 4
HEAD_DIM = 128
RET_W = HEADS * HEAD_DIM
CHUNK = 128
ROPE_BASE = 10000.0
ROPE_PAIRS = (HEAD_DIM // 8, 3 * HEAD_DIM // 16, 3 * HEAD_DIM // 16)
CONV_CH = 512
CONV_K = 31
CONV_PAD = CONV_K // 2
EVEN_IN = 4 * RET_W + 2 * CONV_CH
POOL_CH = 512
POOL_WINDOWS = (2, 4, 8, 16)
POOL_GC = 128
SG_CH = 512
SG_GROUPS = 4
SG_GC = 128
ODD_IN = POOL_CH + 2 * SG_CH
D_FF = 2816
SUBLANES = 8
BF16_SUBLANES = 16
MOD_COLS = 1024
LANES = 128
FF_BLOCK = 256
OUT_BLOCK = 256
CONV_ROWS = 256
FFN_ROWS = 512
ODD_ROWS = 256
HALO = 16
N_DECAY_TABLES = 7
MOD_ROWS = 8
VMEM_LIMIT = 56 * 1024 * 1024


def _dot(a, b):
    return jnp.dot(a, b, preferred_element_type=F32)


def _dot_nt(a, b):
    return lax.dot_general(a, b, (((1,), (1,)), ((), ())), preferred_element_type=F32)


def _dot_tn(a, b):
    return lax.dot_general(a, b, (((0,), (0,)), ((), ())), preferred_element_type=F32)


def _silu(x):
    return x * jax.nn.sigmoid(x)


def _rms_mod(x32, nw, shift, scale):
    y = x32 * lax.rsqrt(jnp.mean(x32 * x32, axis=-1, keepdims=True) + EPS) * nw
    return y * (1.0 + scale) + shift


def _layer_norm(x32, w, b):
    mu = jnp.mean(x32, axis=-1, keepdims=True)
    xc = x32 - mu
    return xc * lax.rsqrt(jnp.mean(xc * xc, axis=-1, keepdims=True) + EPS) * w + b


def _const_spec(shape):
    return pl.BlockSpec(shape, lambda *_: (0,) * len(shape), pipeline_mode=pl.Buffered(1))


def _params(sem):
    return pltpu.CompilerParams(dimension_semantics=sem, vmem_limit_bytes=VMEM_LIMIT)


def _cast_band_specs(stack, layer, bsz, n_tiles):
    _, rows, cols = stack.shape
    n_steps = bsz * n_tiles
    hold = next(h for h in range(1, n_steps + 1)
                if n_steps % h == 0 and rows % (n_steps // h) == 0 and (rows * h // n_steps) % BF16_SUBLANES == 0)
    band = rows * hold // n_steps
    return (pl.BlockSpec((None, band, cols), lambda b, i: (layer, (b * n_tiles + i) // hold, 0)),
            pl.BlockSpec((band, cols), lambda b, i: ((b * n_tiles + i) // hold, 0)))


def _cast_bands(srcs, dsts):
    for src, dst in zip(srcs, dsts):
        dst[...] = src[...].astype(BF16)


def _mod_kernel(c_ref, w_ref, b_ref, o_ref):
    o_ref[...] = _dot(_silu(c_ref[...]).astype(BF16), w_ref[...].astype(BF16)) + b_ref[...]


def _mod_call(c_rows, ada_w, ada_b, layer):
    depth, d, n = ada_w.shape
    tn = MOD_COLS
    return pl.pallas_call(
        _mod_kernel,
        grid=(n // tn,),
        in_specs=[_const_spec((MOD_ROWS, d)),
                  pl.BlockSpec((None, d, tn), lambda j: (layer, 0, j)),
                  pl.BlockSpec((None, 1, tn), lambda j: (layer, 0, j))],
        out_specs=pl.BlockSpec((MOD_ROWS, tn), lambda j: (0, j)),
        out_shape=jax.ShapeDtypeStruct((MOD_ROWS, n), F32),
        compiler_params=_params(("arbitrary",)),
        name="adaln_mod",
    )(c_rows, ada_w, ada_b.reshape(depth, 1, n))


def _log_sigmoid(x):
    return -(jnp.maximum(-x, 0.0) + jnp.log1p(jnp.exp(-jnp.abs(x))))


def _decay_kernel(logit_ref, o_ref):
    j = pl.program_id(0)
    row = lax.broadcasted_iota(jnp.int32, (CHUNK, CHUNK), 0).astype(F32)
    col = lax.broadcasted_iota(jnp.int32, (CHUNK, CHUNK), 1).astype(F32)
    diff = row - col
    lower = diff >= 0.0
    for hd in range(HEADS):
        lgf = _log_sigmoid(jnp.full((CHUNK, CHUNK), logit_ref[(j * 2 + 0) * HEADS + hd], F32))
        lgb = _log_sigmoid(jnp.full((CHUNK, CHUNK), logit_ref[(j * 2 + 1) * HEADS + hd], F32))
        o_ref[0, 0, hd] = jnp.where(lower, jnp.exp(lgf * jnp.where(lower, diff, 0.0)),
                                    jnp.exp(lgb * jnp.where(lower, 0.0, -diff - 1.0)))
        o_ref[0, 1, hd] = jnp.exp(lgf * (CHUNK - 1.0 - row))
        o_ref[0, 2, hd] = jnp.exp(lgb * row)
        o_ref[0, 3, hd] = jnp.exp(lgf * (row + 1.0))
        o_ref[0, 4, hd] = jnp.exp(lgb * (CHUNK - 1.0 - row))
        o_ref[0, 5, hd] = jnp.exp(lgf * float(CHUNK))
        o_ref[0, 6, hd] = jnp.exp(lgb * float(CHUNK))


def _decay_call(ret_decay_logit):
    n_even = ret_decay_logit.shape[0]
    return pl.pallas_call(
        _decay_kernel,
        grid=(n_even,),
        in_specs=[pl.BlockSpec(memory_space=pltpu.SMEM)],
        out_specs=pl.BlockSpec((1, N_DECAY_TABLES, HEADS, CHUNK, CHUNK), lambda j: (j, 0, 0, 0, 0)),
        out_shape=jax.ShapeDtypeStruct((n_even, N_DECAY_TABLES, HEADS, CHUNK, CHUNK), F32),
        compiler_params=_params(("arbitrary",)),
        name="ret_decay",
    )(ret_decay_logit.reshape(-1).astype(F32))


def _even_in_kernel(x_ref, mod_ref, nw_ref, w_ref, cos_ref, sin_ref, s0_ref, dt_ref, *rest, n_cast, tm):
    cast_in = rest[:n_cast]
    q_ref, k_ref, v_ref, gs_ref, u_ref, sprev_ref, sfin_ref = rest[n_cast:n_cast + 7]
    cast_out = rest[n_cast + 7:2 * n_cast + 7]
    state_ref = rest[2 * n_cast + 7]
    b = pl.program_id(0)
    i = pl.program_id(1)
    _cast_bands(cast_in, cast_out)

    @pl.when(i == 0)
    def _():
        state_ref[...] = s0_ref[0]

    h = _rms_mod(x_ref[0], nw_ref[...], mod_ref[b * 6 + 0], mod_ref[b * 6 + 1]).astype(BF16)
    cosf = cos_ref[...]
    sinf = sin_ref[...]

    def rope(t):
        return t * cosf + pltpu.roll(t, HEAD_DIM // 2, 1) * sinf

    pq = _dot(h, w_ref[:, 0:RET_W])
    pk = _dot(h, w_ref[:, RET_W:2 * RET_W])
    pv = _dot(h, w_ref[:, 2 * RET_W:3 * RET_W])
    scale = HEAD_DIM ** -0.5
    for hd in range(HEADS):
        sl = slice(hd * HEAD_DIM, (hd + 1) * HEAD_DIM)
        q_ref[0, :, sl] = (rope(pq[:, sl]) * scale).astype(BF16)
        kr = rope(pk[:, sl]).astype(BF16)
        k_ref[0, :, sl] = kr
        vh = pv[:, sl]
        v_ref[0, :, sl] = vh.astype(BF16)
        zeta = dt_ref[1, hd]
        cdec = dt_ref[5, hd]
        s_cur = state_ref[hd]
        for c in range(tm // CHUNK):
            rows = slice(c * CHUNK, (c + 1) * CHUNK)
            sprev_ref[0, c, hd] = s_cur.astype(BF16)
            wv = (vh[rows] * zeta).astype(BF16)
            s_cur = s_cur * cdec + _dot_tn(kr[rows], wv)
        state_ref[hd] = s_cur

    pg = _dot(h, w_ref[:, 3 * RET_W:4 * RET_W])
    gs_ref[0] = _silu(pg).astype(BF16)
    pa = _dot(h, w_ref[:, 4 * RET_W:4 * RET_W + CONV_CH])
    pb = _dot(h, w_ref[:, 4 * RET_W + CONV_CH:EVEN_IN])
    u_ref[0] = (pa * jax.nn.sigmoid(pb)).astype(BF16)

    @pl.when(i == pl.num_programs(1) - 1)
    def _():
        sfin_ref[0] = state_ref[...]


def _even_in_call(x, mod, nw, w_in, cosf, sinf, s0, dtab, tm, cast=()):
    bsz, l, d = x.shape
    n_chunks = l // CHUNK
    cpt = tm // CHUNK
    tok = lambda w: pl.BlockSpec((1, tm, w), lambda b, i: (b, i, 0))
    state_spec = pl.BlockSpec((1, HEADS, CHUNK, CHUNK), lambda b, i: (b, 0, 0, 0))
    specs = [_cast_band_specs(stack, layer, bsz, l // tm) for stack, layer in cast]
    outs = pl.pallas_call(
        functools.partial(_even_in_kernel, n_cast=len(cast), tm=tm),
        grid=(bsz, l // tm),
        in_specs=[tok(d),
                  _const_spec(mod.shape),
                  _const_spec((1, d)),
                  _const_spec(w_in.shape),
                  pl.BlockSpec((tm, HEAD_DIM), lambda b, i: (i, 0)),
                  pl.BlockSpec((tm, HEAD_DIM), lambda b, i: (i, 0)),
                  state_spec,
                  _const_spec(dtab.shape)] + [s[0] for s in specs],
        out_specs=[tok(RET_W), tok(RET_W), tok(RET_W), tok(RET_W), tok(CONV_CH),
                   pl.BlockSpec((1, cpt, HEADS, CHUNK, CHUNK), lambda b, i: (b, i, 0, 0, 0)),
                   state_spec] + [s[1] for s in specs],
        out_shape=[jax.ShapeDtypeStruct((bsz, l, RET_W), BF16),
                   jax.ShapeDtypeStruct((bsz, l, RET_W), BF16),
                   jax.ShapeDtypeStruct((bsz, l, RET_W), BF16),
                   jax.ShapeDtypeStruct((bsz, l, RET_W), BF16),
                   jax.ShapeDtypeStruct((bsz, l, CONV_CH), BF16),
                   jax.ShapeDtypeStruct((bsz, n_chunks, HEADS, CHUNK, CHUNK), BF16),
                   jax.ShapeDtypeStruct((bsz, HEADS, CHUNK, CHUNK), F32)]
        + [jax.ShapeDtypeStruct(stack.shape[1:], BF16) for stack, _ in cast],
        scratch_shapes=[pltpu.VMEM((HEADS, CHUNK, CHUNK), F32)],
        compiler_params=_params(("arbitrary", "arbitrary")),
        name="even_in",
    )(x, mod, nw, w_in, cosf, sinf, s0, dtab, *[stack for stack, _ in cast])
    return (*outs[:7], tuple(outs[7:]))


def _ctx_state_kernel(x_ref, mod_ref, nw_ref, w_ref, cos_ref, sin_ref, dt_ref, sf_ref, sb_ref):
    b = pl.program_id(0)
    h = _rms_mod(x_ref[0], nw_ref[...], mod_ref[b * 6 + 0], mod_ref[b * 6 + 1]).astype(BF16)
    cosf = cos_ref[...]
    sinf = sin_ref[...]
    pk = _dot(h, w_ref[:, 0:RET_W])
    pv = _dot(h, w_ref[:, RET_W:2 * RET_W])
    n_chunks = x_ref.shape[1] // CHUNK
    for hd in range(HEADS):
        sl = slice(hd * HEAD_DIM, (hd + 1) * HEAD_DIM)
        t = pk[:, sl]
        kr = (t * cosf + pltpu.roll(t, HEAD_DIM // 2, 1) * sinf).astype(BF16)
        vh = pv[:, sl]
        s_f = jnp.zeros((CHUNK, CHUNK), F32)
        s_b = jnp.zeros((CHUNK, CHUNK), F32)
        for c in range(n_chunks):
            rows = slice(c * CHUNK, (c + 1) * CHUNK)
            s_f = s_f * dt_ref[5, hd] + _dot_tn(kr[rows], (vh[rows] * dt_ref[1, hd]).astype(BF16))
        for c in reversed(range(n_chunks)):
            rows = slice(c * CHUNK, (c + 1) * CHUNK)
            s_b = s_b * dt_ref[6, hd] + _dot_tn(kr[rows], (vh[rows] * dt_ref[2, hd]).astype(BF16))
        sf_ref[0, hd] = s_f
        sb_ref[0, hd] = s_b


def _ctx_state_call(ctx, mod, nw, w_kv, cosf, sinf, dtab):
    bsz, lc, d = ctx.shape
    state_spec = pl.BlockSpec((1, HEADS, CHUNK, CHUNK), lambda b: (b, 0, 0, 0))
    return pl.pallas_call(
        _ctx_state_kernel,
        grid=(bsz,),
        in_specs=[pl.BlockSpec((1, lc, d), lambda b: (b, 0, 0)),
                  _const_spec(mod.shape),
                  _const_spec((1, d)),
                  _const_spec(w_kv.shape),
                  _const_spec(cosf.shape),
                  _const_spec(sinf.shape),
                  _const_spec(dtab.shape)],
        out_specs=[state_spec, state_spec],
        out_shape=[jax.ShapeDtypeStruct((bsz, HEADS, CHUNK, CHUNK), F32)] * 2,
        compiler_params=_params(("arbitrary",)),
        name="ctx_state",
    )(ctx, mod, nw, w_kv, cosf, sinf, dtab)


def _even_mix_pieces(b, i, n_tiles, x_ref, q_ref, k_ref, v_ref, gs_ref, u_ref, up_ref, un_ref, sprev_ref,
                     dt_ref, mod_ref, cw_ref, lnw_ref, lnb_ref, wo_ref, state_ref, uext_ref, mix_ref, xo_ref, tm):
    def retention(c, hd):
        rows = slice(c * CHUNK, (c + 1) * CHUNK)
        sl = slice(hd * HEAD_DIM, (hd + 1) * HEAD_DIM)
        qc = q_ref[0, rows, sl]
        kc = k_ref[0, rows, sl]
        vc = v_ref[0, rows, sl]
        s_b = state_ref[hd]
        p = (_dot_nt(qc, kc) * dt_ref[0, hd]).astype(BF16)
        y = _dot(p, vc)
        y = y + _dot(qc, sprev_ref[0, c, hd]) * dt_ref[3, hd]
        y = y + _dot(qc, s_b.astype(BF16)) * dt_ref[4, hd]
        y = y * lax.rsqrt(jnp.mean(y * y, axis=-1, keepdims=True) + EPS)
        mix_ref[rows, sl] = gs_ref[0, rows, sl] * y.astype(BF16)
        wv = (vc.astype(F32) * dt_ref[2, hd]).astype(BF16)
        state_ref[hd] = s_b * dt_ref[6, hd] + _dot_tn(kc, wv)

    def halo():
        uext_ref[0:HALO] = jnp.where(i < n_tiles - 1, up_ref[0].astype(F32), 0.0)
        uext_ref[HALO:HALO + tm] = u_ref[0].astype(F32)
        uext_ref[HALO + tm:2 * HALO + tm] = jnp.where(i > 0, un_ref[0].astype(F32), 0.0)

    def conv(r):
        blocks = []
        for cb in range(CONV_CH // LANES):
            cols = slice(cb * LANES, (cb + 1) * LANES)
            acc = None
            for sub in range(SUBLANES):
                part = None
                for t in range(sub, CONV_K, SUBLANES):
                    start = r * CONV_ROWS + t - sub
                    term = uext_ref[start:start + CONV_ROWS + SUBLANES, cols] * cw_ref[t:t + 1, cols]
                    part = term if part is None else part + term
                shifted = pltpu.roll(part, CONV_ROWS + SUBLANES - (sub + 1), 0)[0:CONV_ROWS]
                acc = shifted if acc is None else acc + shifted
            blocks.append(acc)
        cv = _silu(_layer_norm(jnp.concatenate(blocks, axis=-1), lnw_ref[...], lnb_ref[...]))
        mix_ref[r * CONV_ROWS:(r + 1) * CONV_ROWS, RET_W:RET_W + CONV_CH] = cv.astype(BF16)

    def out_proj(n):
        cols = slice(n * OUT_BLOCK, (n + 1) * OUT_BLOCK)
        out = _dot(mix_ref[...], wo_ref[:, cols])
        xo_ref[:, cols] = x_ref[0, :, cols] + mod_ref[b * 6 + 2][:, cols] * out

    pieces = [halo]
    convs = [functools.partial(conv, r) for r in range(tm // CONV_ROWS)]
    rets = [functools.partial(retention, c, hd) for c in reversed(range(tm // CHUNK)) for hd in range(HEADS)]
    per = len(rets) // len(convs)
    for r, cv in enumerate(convs):
        pieces += rets[r * per:(r + 1) * per] + [cv]
    pieces += rets[len(convs) * per:]
    pieces += [functools.partial(out_proj, n) for n in range(D_MODEL // OUT_BLOCK)]
    return pieces


def _ffn_pieces(xs_ref, o_ref, b, mod_ref, nw_ref, wg_ref, wu_ref, wd_ref, fnw_ref, h_ref, act_ref, final_norm):
    def prologue():
        x32 = xs_ref[...]
        o_ref[...] = x32
        h_ref[...] = _rms_mod(x32, nw_ref[...], mod_ref[b * 6 + 3], mod_ref[b * 6 + 4]).astype(BF16)

    def gate_up(j):
        cols = slice(j * FF_BLOCK, (j + 1) * FF_BLOCK)
        h = h_ref[...]
        act_ref[:, cols] = (_silu(_dot(h, wg_ref[:, cols])) * _dot(h, wu_ref[:, cols])).astype(BF16)

    def down(n):
        cols = slice(n * OUT_BLOCK, (n + 1) * OUT_BLOCK)
        o_ref[:, cols] = o_ref[:, cols] + mod_ref[b * 6 + 5][:, cols] * _dot(act_ref[...], wd_ref[:, cols])

    def final():
        y = o_ref[...]
        o_ref[...] = y * lax.rsqrt(jnp.mean(y * y, axis=-1, keepdims=True) + EPS) * fnw_ref[...]

    pieces = [prologue]
    pieces += [functools.partial(gate_up, j) for j in range(D_FF // FF_BLOCK)]
    pieces += [functools.partial(down, n) for n in range(D_MODEL // OUT_BLOCK)]
    if final_norm:
        pieces.append(final)
    return pieces


def _even_mix_kernel(x_ref, q_ref, k_ref, v_ref, gs_ref, u_ref, up_ref, un_ref, sprev_ref, sb0_ref,
                     dt_ref, mod_ref, cw_ref, lnw_ref, lnb_ref, wo_ref,
                     o_ref, sbfin_ref, state_ref, uext_ref, mix_ref, *, tm):
    b = pl.program_id(0)
    i = pl.program_id(1)
    n_tiles = pl.num_programs(1)

    @pl.when(i == 0)
    def _():
        state_ref[...] = sb0_ref[0]

    for piece in _even_mix_pieces(b, i, n_tiles, x_ref, q_ref, k_ref, v_ref, gs_ref, u_ref, up_ref, un_ref,
                                  sprev_ref, dt_ref, mod_ref, cw_ref, lnw_ref, lnb_ref, wo_ref,
                                  state_ref, uext_ref, mix_ref, o_ref.at[0], tm):
        piece()

    @pl.when(i == n_tiles - 1)
    def _():
        sbfin_ref[0] = state_ref[...]


def _even_mix_call(x, q, k, v, gs, u, sprev, sb0, dtab, mod, conv_w, ln_w, ln_b, w_out, tm):
    bsz, l, d = x.shape
    n_tiles = l // tm
    cpt = tm // CHUNK
    hpt = tm // HALO
    n_halo = l // HALO
    rev = lambda i: n_tiles - 1 - i
    tok = lambda w: pl.BlockSpec((1, tm, w), lambda b, i: (b, rev(i), 0))
    state_spec = pl.BlockSpec((1, HEADS, CHUNK, CHUNK), lambda b, i: (b, 0, 0, 0))
    return pl.pallas_call(
        functools.partial(_even_mix_kernel, tm=tm),
        grid=(bsz, n_tiles),
        in_specs=[tok(d), tok(RET_W), tok(RET_W), tok(RET_W), tok(RET_W), tok(CONV_CH),
                  pl.BlockSpec((1, HALO, CONV_CH), lambda b, i: (b, jnp.maximum(rev(i) * hpt - 1, 0), 0)),
                  pl.BlockSpec((1, HALO, CONV_CH),
                               lambda b, i: (b, jnp.minimum((rev(i) + 1) * hpt, n_halo - 1), 0)),
                  pl.BlockSpec((1, cpt, HEADS, CHUNK, CHUNK), lambda b, i: (b, rev(i), 0, 0, 0)),
                  state_spec,
                  _const_spec(dtab.shape),
                  _const_spec(mod.shape),
                  _const_spec(conv_w.shape),
                  _const_spec((1, CONV_CH)),
                  _const_spec((1, CONV_CH)),
                  _const_spec(w_out.shape)],
        out_specs=[tok(d), state_spec],
        out_shape=[jax.ShapeDtypeStruct((bsz, l, d), F32),
                   jax.ShapeDtypeStruct((bsz, HEADS, CHUNK, CHUNK), F32)],
        scratch_shapes=[pltpu.VMEM((HEADS, CHUNK, CHUNK), F32),
                        pltpu.VMEM((tm + 2 * HALO, CONV_CH), F32),
                        pltpu.VMEM((tm, RET_W + CONV_CH), BF16)],
        compiler_params=_params(("arbitrary", "arbitrary")),
        name="even_mix",
    )(x, q, k, v, gs, u, u, u, sprev, sb0, dtab, mod, conv_w, ln_w, ln_b, w_out)


def _odd_kernel(x_ref, xp_ref, xn_ref, mod_ref, nw_ref, wi_ref, pw_ref, ps_ref, lnw_ref, lnb_ref,
                sgw_ref, sgb_ref, wo_ref, o_ref, h_ref, pcx_ref, mix_ref, *, tm, seq_len):
    b = pl.program_id(0)
    i = pl.program_id(1)
    n_tiles = pl.num_programs(1)
    nw = nw_ref[...]
    shift = mod_ref[b * 6 + 0]
    scale = mod_ref[b * 6 + 1]

    h_ref[0:HALO] = _rms_mod(xp_ref[0], nw, shift, scale).astype(BF16)
    h_ref[HALO:HALO + tm] = _rms_mod(x_ref[0], nw, shift, scale).astype(BF16)
    h_ref[HALO + tm:2 * HALO + tm] = _rms_mod(xn_ref[0], nw, shift, scale).astype(BF16)

    pcx_ref[0:tm + 2 * HALO] = _dot(h_ref[...], wi_ref[:, 0:POOL_CH])
    pcx_ref[0:HALO] = jnp.where(i > 0, pcx_ref[0:HALO], 0.0)
    pcx_ref[HALO + tm:2 * HALO + tm] = jnp.where(i < n_tiles - 1, pcx_ref[HALO + tm:2 * HALO + tm], 0.0)
    pcx_ref[tm + 2 * HALO:tm + 2 * HALO + SUBLANES] = jnp.zeros((SUBLANES, POOL_CH), F32)

    sub = min(tm, ODD_ROWS)
    for r0 in range(0, tm, sub):
        rows = slice(r0, r0 + sub)
        pos = i * tm + r0 + lax.broadcasted_iota(jnp.int32, (sub, POOL_GC), 0)
        for gi, w in enumerate(POOL_WINDOWS):
            left = w // 2
            right = w - 1 - left
            sl = slice(gi * POOL_GC, (gi + 1) * POOL_GC)
            run = pcx_ref[r0:r0 + sub + 2 * HALO + SUBLANES, sl]
            span = 1
            while span < w:
                n_run = run.shape[0]
                run = (run + pltpu.roll(run, n_run - span, 0))[0:n_run - SUBLANES]
                span *= 2
            win = pltpu.roll(run, run.shape[0] - (HALO - left), 0)[0:sub]
            cnt = jnp.minimum(pos + right, seq_len - 1) - jnp.maximum(pos - left, 0) + 1
            m = win / cnt.astype(F32) - pcx_ref[HALO + r0:HALO + r0 + sub, sl]
            po = _dot(m.astype(BF16), pw_ref[gi]) * ps_ref[:, sl]
            mix_ref[rows, sl] = po.astype(BF16)

        pd = _dot(h_ref[HALO + r0:HALO + r0 + sub], wi_ref[:, POOL_CH:ODD_IN])
        z = 0.5 * pd * (1.0 + lax.erf(pd * (2.0 ** -0.5)))
        zu = z[:, 0:SG_CH]
        zv = _layer_norm(z[:, SG_CH:2 * SG_CH], lnw_ref[...], lnb_ref[...]).astype(BF16)
        for c in range(sub // CHUNK):
            crows = slice(c * CHUNK, (c + 1) * CHUNK)
            for g in range(SG_GROUPS):
                sl = slice(g * SG_GC, (g + 1) * SG_GC)
                s = _dot(sgw_ref[g], zv[crows, sl]) + sgb_ref[g]
                mix_ref[r0 + c * CHUNK:r0 + (c + 1) * CHUNK, POOL_CH + g * SG_GC:POOL_CH + (g + 1) * SG_GC] = (
                    zu[crows, sl] * s).astype(BF16)

        out = _dot(mix_ref[rows], wo_ref[...])
        o_ref[0, rows] = x_ref[0, rows] + mod_ref[b * 6 + 2] * out


def _odd_call(x, mod, nw, w_in, pool_w, pool_scale, ln_w, ln_b, sg_w, sg_b_full, w_out, tm):
    bsz, l, d = x.shape
    hpt = tm // HALO
    n_halo = l // HALO
    tok = pl.BlockSpec((1, tm, d), lambda b, i: (b, i, 0))
    return pl.pallas_call(
        functools.partial(_odd_kernel, tm=tm, seq_len=l),
        grid=(bsz, l // tm),
        in_specs=[tok,
                  pl.BlockSpec((1, HALO, d), lambda b, i: (b, jnp.maximum(i * hpt - 1, 0), 0)),
                  pl.BlockSpec((1, HALO, d), lambda b, i: (b, jnp.minimum((i + 1) * hpt, n_halo - 1), 0)),
                  _const_spec(mod.shape),
                  _const_spec((1, d)),
                  _const_spec(w_in.shape),
                  _const_spec(pool_w.shape),
                  _const_spec((1, POOL_CH)),
                  _const_spec((1, SG_CH)),
                  _const_spec((1, SG_CH)),
                  _const_spec(sg_w.shape),
                  _const_spec(sg_b_full.shape),
                  _const_spec(w_out.shape)],
        out_specs=tok,
        out_shape=jax.ShapeDtypeStruct((bsz, l, d), F32),
        scratch_shapes=[pltpu.VMEM((tm + 2 * HALO, d), BF16),
                        pltpu.VMEM((tm + 2 * HALO + SUBLANES, POOL_CH), F32),
                        pltpu.VMEM((tm, POOL_CH + SG_CH), BF16)],
        compiler_params=_params(("arbitrary", "arbitrary")),
        name="odd_mix",
    )(x, x, x, mod, nw, w_in, pool_w, pool_scale, ln_w, ln_b, sg_w, sg_b_full, w_out)


def _ffn_kernel(x_ref, mod_ref, nw_ref, wg_ref, wu_ref, wd_ref, fnw_ref, *rest, n_cast, n_ada, final_norm):
    cast_in, rest = rest[:n_cast], rest[n_cast:]
    ada_in, rest = rest[:3 * n_ada], rest[3 * n_ada:]
    o_ref, rest = rest[0], rest[1:]
    cast_out, rest = rest[:n_cast], rest[n_cast:]
    ada_out, (h_ref, act_ref) = rest[:n_ada], rest[n_ada:]
    _cast_bands(cast_in, cast_out)
    if n_ada:
        c_ref, aw_ref, ab_ref = ada_in
        mo_ref, = ada_out

        @pl.when(jnp.logical_and(pl.program_id(0) == 0, pl.program_id(1) == 0))
        def _():
            mo_ref[...] = jnp.broadcast_to(ab_ref[...], mo_ref.shape)

        mo_ref[...] += _dot(_silu(c_ref[...]).astype(BF16), aw_ref[...].astype(BF16))
    tm = x_ref.shape[1]
    subs = []
    for r0 in range(0, tm, min(tm, FFN_ROWS)):
        rows = pl.ds(r0, min(tm, FFN_ROWS))
        subs.append(_ffn_pieces(x_ref.at[0, rows], o_ref.at[0, rows], pl.program_id(0), mod_ref, nw_ref,
                                wg_ref, wu_ref, wd_ref, fnw_ref, h_ref.at[rows], act_ref.at[rows], final_norm))
    order = [subs[0][0]]
    for k, pieces in enumerate(subs):
        order.append(pieces[1])
        if k + 1 < len(subs):
            order.append(subs[k + 1][0])
        order += pieces[2:]
    for piece in order:
        piece()


def _ffn_call(x, mod, nw, wg, wu, wd, fnw, tm, final_norm, cast=(), ada=None):
    bsz, l, d = x.shape
    n_tiles = l // tm
    n_steps = bsz * n_tiles
    tok = pl.BlockSpec((1, tm, d), lambda b, i: (b, i, 0))
    specs = [_cast_band_specs(stack, layer, bsz, n_tiles) for stack, layer in cast]
    ada_ops, ada_in_specs, ada_out_specs, ada_out_shape = [], [], [], []
    if ada is not None:
        c_rows, ada_w, ada_b, layer = ada
        n_mod = ada_w.shape[2]
        band = d // n_steps
        assert d % n_steps == 0 and band % SUBLANES == 0
        c_bands = c_rows.reshape(MOD_ROWS, n_steps, band).transpose(1, 0, 2)
        ada_ops = [c_bands, ada_w, ada_b.reshape(ada_b.shape[0], 1, n_mod)]
        ada_in_specs = [pl.BlockSpec((None, MOD_ROWS, band), lambda b, i: (b * n_tiles + i, 0, 0)),
                        pl.BlockSpec((None, band, n_mod), lambda b, i: (layer, b * n_tiles + i, 0)),
                        pl.BlockSpec((None, 1, n_mod), lambda b, i: (layer, 0, 0))]
        ada_out_specs = [pl.BlockSpec((MOD_ROWS, n_mod), lambda b, i: (0, 0))]
        ada_out_shape = [jax.ShapeDtypeStruct((MOD_ROWS, n_mod), F32)]
    outs = pl.pallas_call(
        functools.partial(_ffn_kernel, n_cast=len(cast), n_ada=len(ada_out_specs), final_norm=final_norm),
        grid=(bsz, n_tiles),
        in_specs=[tok,
                  _const_spec(mod.shape),
                  _const_spec((1, d)),
                  _const_spec(wg.shape),
                  _const_spec(wu.shape),
                  _const_spec(wd.shape),
                  _const_spec((1, d))] + [s[0] for s in specs] + ada_in_specs,
        out_specs=[tok] + [s[1] for s in specs] + ada_out_specs,
        out_shape=[jax.ShapeDtypeStruct((bsz, l, d), F32)]
        + [jax.ShapeDtypeStruct(stack.shape[1:], BF16) for stack, _ in cast] + ada_out_shape,
        scratch_shapes=[pltpu.VMEM((tm, d), BF16), pltpu.VMEM((tm, D_FF), BF16)],
        compiler_params=_params(("arbitrary", "arbitrary")),
        name="ffn",
    )(x, mod, nw, wg, wu, wd, fnw, *[stack for stack, _ in cast], *ada_ops)
    n_cast = len(cast)
    return outs[0], tuple(outs[1:1 + n_cast]), (outs[1 + n_cast] if ada is not None else None)


def _rope_tables(p_seq, p_row, p_col):
    parts = []
    for p, n in zip((p_seq, p_row, p_col), ROPE_PAIRS):
        freq = ROPE_BASE ** (-jnp.arange(n, dtype=F32) / n)
        parts.append(p[:, None] * freq[None, :])
    ang = jnp.concatenate(parts, axis=-1)
    cos, sin = jnp.cos(ang), jnp.sin(ang)
    return jnp.concatenate([cos, cos], axis=-1), jnp.concatenate([-sin, sin], axis=-1)


def _tile(l):
    return min(l, 1024)


def kernel(x, c, ctx, c_ctx, ada_w, ada_b, norm_w, even_w_in, even_w_out, ret_decay_logit, conv_dw_w, conv_ln_w, conv_ln_b, odd_w_in, odd_w_out, pool_w, pool_scale, sg_ln_w, sg_ln_b, sg_w, sg_b, ffn_w_gate, ffn_w_up, ffn_w_down, final_norm_w):
    bsz, l, d = x.shape
    lc = ctx.shape[1]
    assert d == D_MODEL and l % _tile(l) == 0 and l % CHUNK == 0 and lc % CHUNK == 0 and bsz + 1 <= MOD_ROWS
    rows = l // GRID_W
    grid_r = jnp.broadcast_to(jnp.arange(rows, dtype=F32)[:, None], (rows, GRID_W)).reshape(-1)
    grid_c = jnp.broadcast_to(jnp.arange(GRID_W, dtype=F32)[None, :], (rows, GRID_W)).reshape(-1)
    cos_x, sin_x = _rope_tables(jnp.full((l,), lc, F32), grid_r, grid_c)
    zeros_c = jnp.zeros((lc,), F32)
    cos_c, sin_c = _rope_tables(jnp.arange(lc, dtype=F32), zeros_c, zeros_c)

    cvec = jnp.concatenate([c, c_ctx[None, :], jnp.zeros((MOD_ROWS - bsz - 1, d), F32)], axis=0)
    mod_rows = _mod_call(cvec, ada_w, ada_b, 0)
    dtabs = _decay_call(ret_decay_logit)
    zero_state = jnp.zeros((bsz, HEADS, CHUNK, CHUNK), F32)
    fnw = final_norm_w.reshape(1, d)
    tx, tc = _tile(l), _tile(lc)

    def layer_matrices(i):
        mix_in, mix_out = (even_w_in, even_w_out) if i % 2 == 0 else (odd_w_in, odd_w_out)
        return (mix_in, i // 2), (mix_out, i // 2), (ffn_w_gate, i), (ffn_w_up, i), (ffn_w_down, i)

    first = layer_matrices(0)
    bf16_mats = (first[0][0][first[0][1]].astype(BF16),) + (None,) * (len(first) - 1)
    for i in range(DEPTH):
        j = i // 2
        even = i % 2 == 0
        ctx_after = any(m % 2 == 0 for m in range(i + 1, DEPTH))
        mod_x = mod_rows[:bsz].reshape(bsz * 6, 1, d)
        mod_c = jnp.broadcast_to(mod_rows[bsz].reshape(1, 6, 1, d), (bsz, 6, 1, d)).reshape(bsz * 6, 1, d)
        nw1 = norm_w[i, 0].reshape(1, d)
        nw2 = norm_w[i, 1].reshape(1, d)
        w_in, w_out, wg, wu, wd = bf16_mats
        if even:
            conv_w = conv_dw_w[j]
            lnw = conv_ln_w[j].reshape(1, CONV_CH)
            lnb = conv_ln_b[j].reshape(1, CONV_CH)
            dtab = dtabs[j]
            if ctx_after:
                *mix_c, s_f, _ = _even_in_call(ctx, mod_c, nw1, w_in, cos_c, sin_c, zero_state, dtab, tc)
            else:
                s_f, s_b = _ctx_state_call(ctx, mod_c, nw1, w_in[:, RET_W:3 * RET_W], cos_c, sin_c, dtab)
            pending = layer_matrices(i)[1:] if w_out is None else ()
            *mix_x, _, cast_now = _even_in_call(x, mod_x, nw1, w_in, cos_x, sin_x, s_f, dtab, tx, pending)
            if pending:
                w_out, wg, wu, wd = cast_now
            if ctx_after:
                ctx, s_b = _even_mix_call(ctx, *mix_c, zero_state, dtab, mod_c, conv_w, lnw, lnb, w_out, tc)
            x, _ = _even_mix_call(x, *mix_x, s_b, dtab, mod_x, conv_w, lnw, lnb, w_out, tx)
        else:
            args = (nw1, w_in, pool_w[j].astype(BF16), pool_scale[j].reshape(1, POOL_CH),
                    sg_ln_w[j].reshape(1, SG_CH), sg_ln_b[j].reshape(1, SG_CH), sg_w[j].astype(BF16),
                    jnp.broadcast_to(sg_b[j][:, :, None], (SG_GROUPS, CHUNK, SG_GC)), w_out)
            if ctx_after:
                ctx = _odd_call(ctx, mod_c, *args, tc)
            x = _odd_call(x, mod_x, *args, tx)
        if ctx_after:
            ctx, _, _ = _ffn_call(ctx, mod_c, nw2, wg, wu, wd, fnw, tc, False)
        later = layer_matrices(i + 1) if i + 1 < DEPTH else ()
        ada_next = (cvec, ada_w, ada_b, i + 1) if i + 1 < DEPTH else None
        x, bf16_mats, mod_rows = _ffn_call(x, mod_x, nw2, wg, wu, wd, fnw, min(l, 2 * FFN_ROWS), i == DEPTH - 1,
                                           later, ada_next)
    return x
```

```python
import functools

import jax
import jax.numpy as jnp
from jax import lax
from jax.experimental import pallas as pl
from jax.experimental.pallas import tpu as pltpu

F32 = jnp.float32
BF16 = jnp.bfloat16

D_MODEL = 1024
DEPTH = 4
GRID_W = 64
EPS = 1e-6
HEADS = 4
HEAD_DIM = 128
RET_W = HEADS * HEAD_DIM
CHUNK = 128
ROPE_BASE = 10000.0
ROPE_PAIRS = (HEAD_DIM // 8, 3 * HEAD_DIM // 16, 3 * HEAD_DIM // 16)
CONV_CH = 512
CONV_K = 31
CONV_PAD = CONV_K // 2
EVEN_IN = 4 * RET_W + 2 * CONV_CH
POOL_CH = 512
POOL_WINDOWS = (2, 4, 8, 16)
POOL_GC = 128
SG_CH = 512
SG_GROUPS = 4
SG_GC = 128
ODD_IN = POOL_CH + 2 * SG_CH
D_FF = 2816
SUBLANES = 8
BF16_SUBLANES = 16
MOD_COLS = 1024
LANES = 128
FF_BLOCK = 256
OUT_BLOCK = 256
CONV_ROWS = 256
FFN_ROWS = 512
ODD_ROWS = 512
HALO = 16
N_DECAY_TABLES = 7
MOD_ROWS = 8
VMEM_LIMIT = 56 * 1024 * 1024


def _dot(a, b):
    return jnp.dot(a, b, preferred_element_type=F32)


def _dot_nt(a, b):
    return lax.dot_general(a, b, (((1,), (1,)), ((), ())), preferred_element_type=F32)


def _dot_tn(a, b):
    return lax.dot_general(a, b, (((0,), (0,)), ((), ())), preferred_element_type=F32)


def _silu(x):
    return x * jax.nn.sigmoid(x)


def _rms_mod(x32, nw, shift, scale):
    y = x32 * lax.rsqrt(jnp.mean(x32 * x32, axis=-1, keepdims=True) + EPS) * nw
    return y * (1.0 + scale) + shift


def _layer_norm(x32, w, b):
    mu = jnp.mean(x32, axis=-1, keepdims=True)
    xc = x32 - mu
    return xc * lax.rsqrt(jnp.mean(xc * xc, axis=-1, keepdims=True) + EPS) * w + b


def _const_spec(shape):
    return pl.BlockSpec(shape, lambda *_: (0,) * len(shape), pipeline_mode=pl.Buffered(1))


def _params(sem):
    return pltpu.CompilerParams(dimension_semantics=sem, vmem_limit_bytes=VMEM_LIMIT)


def _cast_band_specs(stack, layer, bsz, n_tiles):
    _, rows, cols = stack.shape
    n_steps = bsz * n_tiles
    hold = next(h for h in range(1, n_steps + 1)
                if n_steps % h == 0 and rows % (n_steps // h) == 0 and (rows * h // n_steps) % BF16_SUBLANES == 0)
    band = rows * hold // n_steps
    return (pl.BlockSpec((None, band, cols), lambda b, i: (layer, (b * n_tiles + i) // hold, 0)),
            pl.BlockSpec((band, cols), lambda b, i: ((b * n_tiles + i) // hold, 0)))


def _cast_bands(srcs, dsts):
    for src, dst in zip(srcs, dsts):
        dst[...] = src[...].astype(BF16)


def _mod_kernel(c_ref, w_ref, b_ref, o_ref):
    o_ref[...] = _dot(_silu(c_ref[...]).astype(BF16), w_ref[...].astype(BF16)) + b_ref[...]


def _mod_call(c_rows, ada_w, ada_b, layer):
    depth, d, n = ada_w.shape
    tn = MOD_COLS
    return pl.pallas_call(
        _mod_kernel,
        grid=(n // tn,),
        in_specs=[_const_spec((MOD_ROWS, d)),
                  pl.BlockSpec((None, d, tn), lambda j: (layer, 0, j)),
                  pl.BlockSpec((None, 1, tn), lambda j: (layer, 0, j))],
        out_specs=pl.BlockSpec((MOD_ROWS, tn), lambda j: (0, j)),
        out_shape=jax.ShapeDtypeStruct((MOD_ROWS, n), F32),
        compiler_params=_params(("arbitrary",)),
        name="adaln_mod",
    )(c_rows, ada_w, ada_b.reshape(depth, 1, n))


def _log_sigmoid(x):
    return -(jnp.maximum(-x, 0.0) + jnp.log1p(jnp.exp(-jnp.abs(x))))


def _decay_kernel(logit_ref, o_ref):
    j = pl.program_id(0)
    row = lax.broadcasted_iota(jnp.int32, (CHUNK, CHUNK), 0).astype(F32)
    col = lax.broadcasted_iota(jnp.int32, (CHUNK, CHUNK), 1).astype(F32)
    diff = row - col
    lower = diff >= 0.0
    for hd in range(HEADS):
        lgf = _log_sigmoid(jnp.full((CHUNK, CHUNK), logit_ref[(j * 2 + 0) * HEADS + hd], F32))
        lgb = _log_sigmoid(jnp.full((CHUNK, CHUNK), logit_ref[(j * 2 + 1) * HEADS + hd], F32))
        o_ref[0, 0, hd] = jnp.where(lower, jnp.exp(lgf * jnp.where(lower, diff, 0.0)),
                                    jnp.exp(lgb * jnp.where(lower, 0.0, -diff - 1.0)))
        o_ref[0, 1, hd] = jnp.exp(lgf * (CHUNK - 1.0 - row))
        o_ref[0, 2, hd] = jnp.exp(lgb * row)
        o_ref[0, 3, hd] = jnp.exp(lgf * (row + 1.0))
        o_ref[0, 4, hd] = jnp.exp(lgb * (CHUNK - 1.0 - row))
        o_ref[0, 5, hd] = jnp.exp(lgf * float(CHUNK))
        o_ref[0, 6, hd] = jnp.exp(lgb * float(CHUNK))


def _decay_call(ret_decay_logit):
    n_even = ret_decay_logit.shape[0]
    return pl.pallas_call(
        _decay_kernel,
        grid=(n_even,),
        in_specs=[pl.BlockSpec(memory_space=pltpu.SMEM)],
        out_specs=pl.BlockSpec((1, N_DECAY_TABLES, HEADS, CHUNK, CHUNK), lambda j: (j, 0, 0, 0, 0)),
        out_shape=jax.ShapeDtypeStruct((n_even, N_DECAY_TABLES, HEADS, CHUNK, CHUNK), F32),
        compiler_params=_params(("arbitrary",)),
        name="ret_decay",
    )(ret_decay_logit.reshape(-1).astype(F32))


def _even_in_kernel(x_ref, mod_ref, nw_ref, w_ref, cos_ref, sin_ref, s0_ref, dt_ref, *rest, n_cast, tm):
    cast_in = rest[:n_cast]
    q_ref, k_ref, v_ref, gs_ref, u_ref, sprev_ref, sfin_ref = rest[n_cast:n_cast + 7]
    cast_out = rest[n_cast + 7:2 * n_cast + 7]
    state_ref = rest[2 * n_cast + 7]
    b = pl.program_id(0)
    i = pl.program_id(1)
    _cast_bands(cast_in, cast_out)

    @pl.when(i == 0)
    def _():
        state_ref[...] = s0_ref[0]

    h = _rms_mod(x_ref[0], nw_ref[...], mod_ref[b * 6 + 0], mod_ref[b * 6 + 1]).astype(BF16)
    cosf = cos_ref[...]
    sinf = sin_ref[...]

    def rope(t):
        return t * cosf + pltpu.roll(t, HEAD_DIM // 2, 1) * sinf

    pq = _dot(h, w_ref[:, 0:RET_W])
    pk = _dot(h, w_ref[:, RET_W:2 * RET_W])
    pv = _dot(h, w_ref[:, 2 * RET_W:3 * RET_W])
    scale = HEAD_DIM ** -0.5
    for hd in range(HEADS):
        sl = slice(hd * HEAD_DIM, (hd + 1) * HEAD_DIM)
        q_ref[0, :, sl] = (rope(pq[:, sl]) * scale).astype(BF16)
        kr = rope(pk[:, sl]).astype(BF16)
        k_ref[0, :, sl] = kr
        vh = pv[:, sl]
        v_ref[0, :, sl] = vh.astype(BF16)
        zeta = dt_ref[1, hd]
        cdec = dt_ref[5, hd]
        s_cur = state_ref[hd]
        for c in range(tm // CHUNK):
            rows = slice(c * CHUNK, (c + 1) * CHUNK)
            sprev_ref[0, c, hd] = s_cur.astype(BF16)
            wv = (vh[rows] * zeta).astype(BF16)
            s_cur = s_cur * cdec + _dot_tn(kr[rows], wv)
        state_ref[hd] = s_cur

    pg = _dot(h, w_ref[:, 3 * RET_W:4 * RET_W])
    gs_ref[0] = _silu(pg).astype(BF16)
    pa = _dot(h, w_ref[:, 4 * RET_W:4 * RET_W + CONV_CH])
    pb = _dot(h, w_ref[:, 4 * RET_W + CONV_CH:EVEN_IN])
    u_ref[0] = (pa * jax.nn.sigmoid(pb)).astype(BF16)

    @pl.when(i == pl.num_programs(1) - 1)
    def _():
        sfin_ref[0] = state_ref[...]


def _even_in_call(x, mod, nw, w_in, cosf, sinf, s0, dtab, tm, cast=()):
    bsz, l, d = x.shape
    n_chunks = l // CHUNK
    cpt = tm // CHUNK
    tok = lambda w: pl.BlockSpec((1, tm, w), lambda b, i: (b, i, 0))
    state_spec = pl.BlockSpec((1, HEADS, CHUNK, CHUNK), lambda b, i: (b, 0, 0, 0))
    specs = [_cast_band_specs(stack, layer, bsz, l // tm) for stack, layer in cast]
    outs = pl.pallas_call(
        functools.partial(_even_in_kernel, n_cast=len(cast), tm=tm),
        grid=(bsz, l // tm),
        in_specs=[tok(d),
                  _const_spec(mod.shape),
                  _const_spec((1, d)),
                  _const_spec(w_in.shape),
                  pl.BlockSpec((tm, HEAD_DIM), lambda b, i: (i, 0)),
                  pl.BlockSpec((tm, HEAD_DIM), lambda b, i: (i, 0)),
                  state_spec,
                  _const_spec(dtab.shape)] + [s[0] for s in specs],
        out_specs=[tok(RET_W), tok(RET_W), tok(RET_W), tok(RET_W), tok(CONV_CH),
                   pl.BlockSpec((1, cpt, HEADS, CHUNK, CHUNK), lambda b, i: (b, i, 0, 0, 0)),
                   state_spec] + [s[1] for s in specs],
        out_shape=[jax.ShapeDtypeStruct((bsz, l, RET_W), BF16),
                   jax.ShapeDtypeStruct((bsz, l, RET_W), BF16),
                   jax.ShapeDtypeStruct((bsz, l, RET_W), BF16),
                   jax.ShapeDtypeStruct((bsz, l, RET_W), BF16),
                   jax.ShapeDtypeStruct((bsz, l, CONV_CH), BF16),
                   jax.ShapeDtypeStruct((bsz, n_chunks, HEADS, CHUNK, CHUNK), BF16),
                   jax.ShapeDtypeStruct((bsz, HEADS, CHUNK, CHUNK), F32)]
        + [jax.ShapeDtypeStruct(stack.shape[1:], BF16) for stack, _ in cast],
        scratch_shapes=[pltpu.VMEM((HEADS, CHUNK, CHUNK), F32)],
        compiler_params=_params(("arbitrary", "arbitrary")),
        name="even_in",
    )(x, mod, nw, w_in, cosf, sinf, s0, dtab, *[stack for stack, _ in cast])
    return (*outs[:7], tuple(outs[7:]))


def _ctx_state_kernel(x_ref, mod_ref, nw_ref, w_ref, cos_ref, sin_ref, dt_ref, sf_ref, sb_ref):
    b = pl.program_id(0)
    h = _rms_mod(x_ref[0], nw_ref[...], mod_ref[b * 6 + 0], mod_ref[b * 6 + 1]).astype(BF16)
    cosf = cos_ref[...]
    sinf = sin_ref[...]
    pk = _dot(h, w_ref[:, 0:RET_W])
    pv = _dot(h, w_ref[:, RET_W:2 * RET_W])
    n_chunks = x_ref.shape[1] // CHUNK
    for hd in range(HEADS):
        sl = slice(hd * HEAD_DIM, (hd + 1) * HEAD_DIM)
        t = pk[:, sl]
        kr = (t * cosf + pltpu.roll(t, HEAD_DIM // 2, 1) * sinf).astype(BF16)
        vh = pv[:, sl]
        s_f = jnp.zeros((CHUNK, CHUNK), F32)
        s_b = jnp.zeros((CHUNK, CHUNK), F32)
        for c in range(n_chunks):
            rows = slice(c * CHUNK, (c + 1) * CHUNK)
            s_f = s_f * dt_ref[5, hd] + _dot_tn(kr[rows], (vh[rows] * dt_ref[1, hd]).astype(BF16))
        for c in reversed(range(n_chunks)):
            rows = slice(c * CHUNK, (c + 1) * CHUNK)
            s_b = s_b * dt_ref[6, hd] + _dot_tn(kr[rows], (vh[rows] * dt_ref[2, hd]).astype(BF16))
        sf_ref[0, hd] = s_f
        sb_ref[0, hd] = s_b


def _ctx_state_call(ctx, mod, nw, w_kv, cosf, sinf, dtab):
    bsz, lc, d = ctx.shape
    state_spec = pl.BlockSpec((1, HEADS, CHUNK, CHUNK), lambda b: (b, 0, 0, 0))
    return pl.pallas_call(
        _ctx_state_kernel,
        grid=(bsz,),
        in_specs=[pl.BlockSpec((1, lc, d), lambda b: (b, 0, 0)),
                  _const_spec(mod.shape),
                  _const_spec((1, d)),
                  _const_spec(w_kv.shape),
                  _const_spec(cosf.shape),
                  _const_spec(sinf.shape),
                  _const_spec(dtab.shape)],
        out_specs=[state_spec, state_spec],
        out_shape=[jax.ShapeDtypeStruct((bsz, HEADS, CHUNK, CHUNK), F32)] * 2,
        compiler_params=_params(("arbitrary",)),
        name="ctx_state",
    )(ctx, mod, nw, w_kv, cosf, sinf, dtab)


def _even_mix_pieces(b, i, n_tiles, x_ref, q_ref, k_ref, v_ref, gs_ref, u_ref, up_ref, un_ref, sprev_ref,
                     dt_ref, mod_ref, cw_ref, lnw_ref, lnb_ref, wo_ref, state_ref, uext_ref, mix_ref, xo_ref, tm):
    def retention(c, hd):
        rows = slice(c * CHUNK, (c + 1) * CHUNK)
        sl = slice(hd * HEAD_DIM, (hd + 1) * HEAD_DIM)
        qc = q_ref[0, rows, sl]
        kc = k_ref[0, rows, sl]
        vc = v_ref[0, rows, sl]
        s_b = state_ref[hd]
        p = (_dot_nt(qc, kc) * dt_ref[0, hd]).astype(BF16)
        y = _dot(p, vc)
        y = y + _dot(qc, sprev_ref[0, c, hd]) * dt_ref[3, hd]
        y = y + _dot(qc, s_b.astype(BF16)) * dt_ref[4, hd]
        y = y * lax.rsqrt(jnp.mean(y * y, axis=-1, keepdims=True) + EPS)
        mix_ref[rows, sl] = gs_ref[0, rows, sl] * y.astype(BF16)
        wv = (vc.astype(F32) * dt_ref[2, hd]).astype(BF16)
        state_ref[hd] = s_b * dt_ref[6, hd] + _dot_tn(kc, wv)

    def halo():
        uext_ref[0:HALO] = jnp.where(i < n_tiles - 1, up_ref[0].astype(F32), 0.0)
        uext_ref[HALO:HALO + tm] = u_ref[0].astype(F32)
        uext_ref[HALO + tm:2 * HALO + tm] = jnp.where(i > 0, un_ref[0].astype(F32), 0.0)

    def conv(r):
        blocks = []
        for cb in range(CONV_CH // LANES):
            cols = slice(cb * LANES, (cb + 1) * LANES)
            acc = None
            for sub in range(SUBLANES):
                part = None
                for t in range(sub, CONV_K, SUBLANES):
                    start = r * CONV_ROWS + t - sub
                    term = uext_ref[start:start + CONV_ROWS + SUBLANES, cols] * cw_ref[t:t + 1, cols]
                    part = term if part is None else part + term
                shifted = pltpu.roll(part, CONV_ROWS + SUBLANES - (sub + 1), 0)[0:CONV_ROWS]
                acc = shifted if acc is None else acc + shifted
            blocks.append(acc)
        cv = _silu(_layer_norm(jnp.concatenate(blocks, axis=-1), lnw_ref[...], lnb_ref[...]))
        mix_ref[r * CONV_ROWS:(r + 1) * CONV_ROWS, RET_W:RET_W + CONV_CH] = cv.astype(BF16)

    def out_proj(n):
        cols = slice(n * OUT_BLOCK, (n + 1) * OUT_BLOCK)
        out = _dot(mix_ref[...], wo_ref[:, cols])
        xo_ref[:, cols] = x_ref[0, :, cols] + mod_ref[b * 6 + 2][:, cols] * out

    pieces = [halo]
    convs = [functools.partial(conv, r) for r in range(tm // CONV_ROWS)]
    rets = [functools.partial(retention, c, hd) for c in reversed(range(tm // CHUNK)) for hd in range(HEADS)]
    per = len(rets) // len(convs)
    for r, cv in enumerate(convs):
        pieces += rets[r * per:(r + 1) * per] + [cv]
    pieces += rets[len(convs) * per:]
    pieces += [functools.partial(out_proj, n) for n in range(D_MODEL // OUT_BLOCK)]
    return pieces


def _ffn_pieces(xs_ref, o_ref, b, mod_ref, nw_ref, wg_ref, wu_ref, wd_ref, fnw_ref, h_ref, act_ref, final_norm):
    def prologue():
        x32 = xs_ref[...]
        o_ref[...] = x32
        h_ref[...] = _rms_mod(x32, nw_ref[...], mod_ref[b * 6 + 3], mod_ref[b * 6 + 4]).astype(BF16)

    def gate_up(j):
        cols = slice(j * FF_BLOCK, (j + 1) * FF_BLOCK)
        h = h_ref[...]
        act_ref[:, cols] = (_silu(_dot(h, wg_ref[:, cols])) * _dot(h, wu_ref[:, cols])).astype(BF16)

    def down(n):
        cols = slice(n * OUT_BLOCK, (n + 1) * OUT_BLOCK)
        o_ref[:, cols] = o_ref[:, cols] + mod_ref[b * 6 + 5][:, cols] * _dot(act_ref[...], wd_ref[:, cols])

    def final():
        y = o_ref[...]
        o_ref[...] = y * lax.rsqrt(jnp.mean(y * y, axis=-1, keepdims=True) + EPS) * fnw_ref[...]

    pieces = [prologue]
    pieces += [functools.partial(gate_up, j) for j in range(D_FF // FF_BLOCK)]
    pieces += [functools.partial(down, n) for n in range(D_MODEL // OUT_BLOCK)]
    if final_norm:
        pieces.append(final)
    return pieces


def _even_mix_kernel(x_ref, q_ref, k_ref, v_ref, gs_ref, u_ref, up_ref, un_ref, sprev_ref, sb0_ref,
                     dt_ref, mod_ref, cw_ref, lnw_ref, lnb_ref, wo_ref,
                     o_ref, sbfin_ref, state_ref, uext_ref, mix_ref, *, tm):
    b = pl.program_id(0)
    i = pl.program_id(1)
    n_tiles = pl.num_programs(1)

    @pl.when(i == 0)
    def _():
        state_ref[...] = sb0_ref[0]

    for piece in _even_mix_pieces(b, i, n_tiles, x_ref, q_ref, k_ref, v_ref, gs_ref, u_ref, up_ref, un_ref,
                                  sprev_ref, dt_ref, mod_ref, cw_ref, lnw_ref, lnb_ref, wo_ref,
                                  state_ref, uext_ref, mix_ref, o_ref.at[0], tm):
        piece()

    @pl.when(i == n_tiles - 1)
    def _():
        sbfin_ref[0] = state_ref[...]


def _even_mix_call(x, q, k, v, gs, u, sprev, sb0, dtab, mod, conv_w, ln_w, ln_b, w_out, tm):
    bsz, l, d = x.shape
    n_tiles = l // tm
    cpt = tm // CHUNK
    hpt = tm // HALO
    n_halo = l // HALO
    rev = lambda i: n_tiles - 1 - i
    tok = lambda w: pl.BlockSpec((1, tm, w), lambda b, i: (b, rev(i), 0))
    state_spec = pl.BlockSpec((1, HEADS, CHUNK, CHUNK), lambda b, i: (b, 0, 0, 0))
    return pl.pallas_call(
        functools.partial(_even_mix_kernel, tm=tm),
        grid=(bsz, n_tiles),
        in_specs=[tok(d), tok(RET_W), tok(RET_W), tok(RET_W), tok(RET_W), tok(CONV_CH),
                  pl.BlockSpec((1, HALO, CONV_CH), lambda b, i: (b, jnp.maximum(rev(i) * hpt - 1, 0), 0)),
                  pl.BlockSpec((1, HALO, CONV_CH),
                               lambda b, i: (b, jnp.minimum((rev(i) + 1) * hpt, n_halo - 1), 0)),
                  pl.BlockSpec((1, cpt, HEADS, CHUNK, CHUNK), lambda b, i: (b, rev(i), 0, 0, 0)),
                  state_spec,
                  _const_spec(dtab.shape),
                  _const_spec(mod.shape),
                  _const_spec(conv_w.shape),
                  _const_spec((1, CONV_CH)),
                  _const_spec((1, CONV_CH)),
                  _const_spec(w_out.shape)],
        out_specs=[tok(d), state_spec],
        out_shape=[jax.ShapeDtypeStruct((bsz, l, d), F32),
                   jax.ShapeDtypeStruct((bsz, HEADS, CHUNK, CHUNK), F32)],
        scratch_shapes=[pltpu.VMEM((HEADS, CHUNK, CHUNK), F32),
                        pltpu.VMEM((tm + 2 * HALO, CONV_CH), F32),
                        pltpu.VMEM((tm, RET_W + CONV_CH), BF16)],
        compiler_params=_params(("arbitrary", "arbitrary")),
        name="even_mix",
    )(x, q, k, v, gs, u, u, u, sprev, sb0, dtab, mod, conv_w, ln_w, ln_b, w_out)


def _odd_kernel(x_ref, xp_ref, xn_ref, mod_ref, nw_ref, wi_ref, pw_ref, ps_ref, lnw_ref, lnb_ref,
                sgw_ref, sgb_ref, wo_ref, o_ref, h_ref, pcx_ref, mix_ref, *, tm, seq_len):
    b = pl.program_id(0)
    i = pl.program_id(1)
    n_tiles = pl.num_programs(1)
    nw = nw_ref[...]
    shift = mod_ref[b * 6 + 0]
    scale = mod_ref[b * 6 + 1]

    h_ref[0:HALO] = _rms_mod(xp_ref[0], nw, shift, scale).astype(BF16)
    h_ref[HALO:HALO + tm] = _rms_mod(x_ref[0], nw, shift, scale).astype(BF16)
    h_ref[HALO + tm:2 * HALO + tm] = _rms_mod(xn_ref[0], nw, shift, scale).astype(BF16)

    pcx_ref[0:tm + 2 * HALO] = _dot(h_ref[...], wi_ref[:, 0:POOL_CH])
    pcx_ref[0:HALO] = jnp.where(i > 0, pcx_ref[0:HALO], 0.0)
    pcx_ref[HALO + tm:2 * HALO + tm] = jnp.where(i < n_tiles - 1, pcx_ref[HALO + tm:2 * HALO + tm], 0.0)
    pcx_ref[tm + 2 * HALO:tm + 2 * HALO + SUBLANES] = jnp.zeros((SUBLANES, POOL_CH), F32)

    sub = min(tm, ODD_ROWS)
    for r0 in range(0, tm, sub):
        rows = slice(r0, r0 + sub)
        pos = i * tm + r0 + lax.broadcasted_iota(jnp.int32, (sub, POOL_GC), 0)
        for gi, w in enumerate(POOL_WINDOWS):
            left = w // 2
            right = w - 1 - left
            sl = slice(gi * POOL_GC, (gi + 1) * POOL_GC)
            run = pcx_ref[r0:r0 + sub + 2 * HALO + SUBLANES, sl]
            span = 1
            while span < w:
                n_run = run.shape[0]
                run = (run + pltpu.roll(run, n_run - span, 0))[0:n_run - SUBLANES]
                span *= 2
            win = pltpu.roll(run, run.shape[0] - (HALO - left), 0)[0:sub]
            cnt = jnp.minimum(pos + right, seq_len - 1) - jnp.maximum(pos - left, 0) + 1
            m = win / cnt.astype(F32) - pcx_ref[HALO + r0:HALO + r0 + sub, sl]
            po = _dot(m.astype(BF16), pw_ref[gi]) * ps_ref[:, sl]
            mix_ref[rows, sl] = po.astype(BF16)

        pd = _dot(h_ref[HALO + r0:HALO + r0 + sub], wi_ref[:, POOL_CH:ODD_IN])
        z = 0.5 * pd * (1.0 + lax.erf(pd * (2.0 ** -0.5)))
        zu = z[:, 0:SG_CH]
        zv = _layer_norm(z[:, SG_CH:2 * SG_CH], lnw_ref[...], lnb_ref[...]).astype(BF16)
        for c in range(sub // CHUNK):
            crows = slice(c * CHUNK, (c + 1) * CHUNK)
            for g in range(SG_GROUPS):
                sl = slice(g * SG_GC, (g + 1) * SG_GC)
                s = _dot(sgw_ref[g], zv[crows, sl]) + sgb_ref[g]
                mix_ref[r0 + c * CHUNK:r0 + (c + 1) * CHUNK, POOL_CH + g * SG_GC:POOL_CH + (g + 1) * SG_GC] = (
                    zu[crows, sl] * s).astype(BF16)

        out = _dot(mix_ref[rows], wo_ref[...])
        o_ref[0, rows] = x_ref[0, rows] + mod_ref[b * 6 + 2] * out


def _odd_call(x, mod, nw, w_in, pool_w, pool_scale, ln_w, ln_b, sg_w, sg_b_full, w_out, tm):
    bsz, l, d = x.shape
    hpt = tm // HALO
    n_halo = l // HALO
    tok = pl.BlockSpec((1, tm, d), lambda b, i: (b, i, 0))
    return pl.pallas_call(
        functools.partial(_odd_kernel, tm=tm, seq_len=l),
        grid=(bsz, l // tm),
        in_specs=[tok,
                  pl.BlockSpec((1, HALO, d), lambda b, i: (b, jnp.maximum(i * hpt - 1, 0), 0)),
                  pl.BlockSpec((1, HALO, d), lambda b, i: (b, jnp.minimum((i + 1) * hpt, n_halo - 1), 0)),
                  _const_spec(mod.shape),
                  _const_spec((1, d)),
                  _const_spec(w_in.shape),
                  _const_spec(pool_w.shape),
                  _const_spec((1, POOL_CH)),
                  _const_spec((1, SG_CH)),
                  _const_spec((1, SG_CH)),
                  _const_spec(sg_w.shape),
                  _const_spec(sg_b_full.shape),
                  _const_spec(w_out.shape)],
        out_specs=tok,
        out_shape=jax.ShapeDtypeStruct((bsz, l, d), F32),
        scratch_shapes=[pltpu.VMEM((tm + 2 * HALO, d), BF16),
                        pltpu.VMEM((tm + 2 * HALO + SUBLANES, POOL_CH), F32),
                        pltpu.VMEM((tm, POOL_CH + SG_CH), BF16)],
        compiler_params=_params(("arbitrary", "arbitrary")),
        name="odd_mix",
    )(x, x, x, mod, nw, w_in, pool_w, pool_scale, ln_w, ln_b, sg_w, sg_b_full, w_out)


def _ffn_kernel(x_ref, mod_ref, nw_ref, wg_ref, wu_ref, wd_ref, fnw_ref, *rest, n_cast, n_ada, final_norm):
    cast_in, rest = rest[:n_cast], rest[n_cast:]
    ada_in, rest = rest[:3 * n_ada], rest[3 * n_ada:]
    o_ref, rest = rest[0], rest[1:]
    cast_out, rest = rest[:n_cast], rest[n_cast:]
    ada_out, (h_ref, act_ref) = rest[:n_ada], rest[n_ada:]
    _cast_bands(cast_in, cast_out)
    if n_ada:
        c_ref, aw_ref, ab_ref = ada_in
        mo_ref, = ada_out

        @pl.when(jnp.logical_and(pl.program_id(0) == 0, pl.program_id(1) == 0))
        def _():
            mo_ref[...] = jnp.broadcast_to(ab_ref[...], mo_ref.shape)

        mo_ref[...] += _dot(_silu(c_ref[...]).astype(BF16), aw_ref[...].astype(BF16))
    tm = x_ref.shape[1]
    subs = []
    for r0 in range(0, tm, min(tm, FFN_ROWS)):
        rows = pl.ds(r0, min(tm, FFN_ROWS))
        subs.append(_ffn_pieces(x_ref.at[0, rows], o_ref.at[0, rows], pl.program_id(0), mod_ref, nw_ref,
                                wg_ref, wu_ref, wd_ref, fnw_ref, h_ref.at[rows], act_ref.at[rows], final_norm))
    order = [subs[0][0]]
    for k, pieces in enumerate(subs):
        order.append(pieces[1])
        if k + 1 < len(subs):
            order.append(subs[k + 1][0])
        order += pieces[2:]
    for piece in order:
        piece()


def _ffn_call(x, mod, nw, wg, wu, wd, fnw, tm, final_norm, cast=(), ada=None):
    bsz, l, d = x.shape
    n_tiles = l // tm
    n_steps = bsz * n_tiles
    tok = pl.BlockSpec((1, tm, d), lambda b, i: (b, i, 0))
    specs = [_cast_band_specs(stack, layer, bsz, n_tiles) for stack, layer in cast]
    ada_ops, ada_in_specs, ada_out_specs, ada_out_shape = [], [], [], []
    if ada is not None:
        c_rows, ada_w, ada_b, layer = ada
        n_mod = ada_w.shape[2]
        band = d // n_steps
        assert d % n_steps == 0 and band % SUBLANES == 0
        c_bands = c_rows.reshape(MOD_ROWS, n_steps, band).transpose(1, 0, 2)
        ada_ops = [c_bands, ada_w, ada_b.reshape(ada_b.shape[0], 1, n_mod)]
        ada_in_specs = [pl.BlockSpec((None, MOD_ROWS, band), lambda b, i: (b * n_tiles + i, 0, 0)),
                        pl.BlockSpec((None, band, n_mod), lambda b, i: (layer, b * n_tiles + i, 0)),
                        pl.BlockSpec((None, 1, n_mod), lambda b, i: (layer, 0, 0))]
        ada_out_specs = [pl.BlockSpec((MOD_ROWS, n_mod), lambda b, i: (0, 0))]
        ada_out_shape = [jax.ShapeDtypeStruct((MOD_ROWS, n_mod), F32)]
    outs = pl.pallas_call(
        functools.partial(_ffn_kernel, n_cast=len(cast), n_ada=len(ada_out_specs), final_norm=final_norm),
        grid=(bsz, n_tiles),
        in_specs=[tok,
                  _const_spec(mod.shape),
                  _const_spec((1, d)),
                  _const_spec(wg.shape),
                  _const_spec(wu.shape),
                  _const_spec(wd.shape),
                  _const_spec((1, d))] + [s[0] for s in specs] + ada_in_specs,
        out_specs=[tok] + [s[1] for s in specs] + ada_out_specs,
        out_shape=[jax.ShapeDtypeStruct((bsz, l, d), F32)]
        + [jax.ShapeDtypeStruct(stack.shape[1:], BF16) for stack, _ in cast] + ada_out_shape,
        scratch_shapes=[pltpu.VMEM((tm, d), BF16), pltpu.VMEM((tm, D_FF), BF16)],
        compiler_params=_params(("arbitrary", "arbitrary")),
        name="ffn",
    )(x, mod, nw, wg, wu, wd, fnw, *[stack for stack, _ in cast], *ada_ops)
    n_cast = len(cast)
    return outs[0], tuple(outs[1:1 + n_cast]), (outs[1 + n_cast] if ada is not None else None)


def _rope_tables(p_seq, p_row, p_col):
    parts = []
    for p, n in zip((p_seq, p_row, p_col), ROPE_PAIRS):
        freq = ROPE_BASE ** (-jnp.arange(n, dtype=F32) / n)
        parts.append(p[:, None] * freq[None, :])
    ang = jnp.concatenate(parts, axis=-1)
    cos, sin = jnp.cos(ang), jnp.sin(ang)
    return jnp.concatenate([cos, cos], axis=-1), jnp.concatenate([-sin, sin], axis=-1)


def _tile(l):
    return min(l, 1024)


def kernel(x, c, ctx, c_ctx, ada_w, ada_b, norm_w, even_w_in, even_w_out, ret_decay_logit, conv_dw_w, conv_ln_w, conv_ln_b, odd_w_in, odd_w_out, pool_w, pool_scale, sg_ln_w, sg_ln_b, sg_w, sg_b, ffn_w_gate, ffn_w_up, ffn_w_down, final_norm_w):
    bsz, l, d = x.shape
    lc = ctx.shape[1]
    assert d == D_MODEL and l % _tile(l) == 0 and l % CHUNK == 0 and lc % CHUNK == 0 and bsz + 1 <= MOD_ROWS
    rows = l // GRID_W
    grid_r = jnp.broadcast_to(jnp.arange(rows, dtype=F32)[:, None], (rows, GRID_W)).reshape(-1)
    grid_c = jnp.broadcast_to(jnp.arange(GRID_W, dtype=F32)[None, :], (rows, GRID_W)).reshape(-1)
    cos_x, sin_x = _rope_tables(jnp.full((l,), lc, F32), grid_r, grid_c)
    zeros_c = jnp.zeros((lc,), F32)
    cos_c, sin_c = _rope_tables(jnp.arange(lc, dtype=F32), zeros_c, zeros_c)

    cvec = jnp.concatenate([c, c_ctx[None, :], jnp.zeros((MOD_ROWS - bsz - 1, d), F32)], axis=0)
    mod_rows = _mod_call(cvec, ada_w, ada_b, 0)
    dtabs = _decay_call(ret_decay_logit)
    zero_state = jnp.zeros((bsz, HEADS, CHUNK, CHUNK), F32)
    fnw = final_norm_w.reshape(1, d)
    tx, tc = _tile(l), _tile(lc)

    def layer_matrices(i):
        mix_in, mix_out = (even_w_in, even_w_out) if i % 2 == 0 else (odd_w_in, odd_w_out)
        return (mix_in, i // 2), (mix_out, i // 2), (ffn_w_gate, i), (ffn_w_up, i), (ffn_w_down, i)

    first = layer_matrices(0)
    bf16_mats = (first[0][0][first[0][1]].astype(BF16),) + (None,) * (len(first) - 1)
    for i in range(DEPTH):
        j = i // 2
        even = i % 2 == 0
        ctx_after = any(m % 2 == 0 for m in range(i + 1, DEPTH))
        mod_x = mod_rows[:bsz].reshape(bsz * 6, 1, d)
        mod_c = jnp.broadcast_to(mod_rows[bsz].reshape(1, 6, 1, d), (bsz, 6, 1, d)).reshape(bsz * 6, 1, d)
        nw1 = norm_w[i, 0].reshape(1, d)
        nw2 = norm_w[i, 1].reshape(1, d)
        w_in, w_out, wg, wu, wd = bf16_mats
        if even:
            conv_w = conv_dw_w[j]
            lnw = conv_ln_w[j].reshape(1, CONV_CH)
            lnb = conv_ln_b[j].reshape(1, CONV_CH)
            dtab = dtabs[j]
            if ctx_after:
                *mix_c, s_f, _ = _even_in_call(ctx, mod_c, nw1, w_in, cos_c, sin_c, zero_state, dtab, tc)
            else:
                s_f, s_b = _ctx_state_call(ctx, mod_c, nw1, w_in[:, RET_W:3 * RET_W], cos_c, sin_c, dtab)
            pending = layer_matrices(i)[1:] if w_out is None else ()
            *mix_x, _, cast_now = _even_in_call(x, mod_x, nw1, w_in, cos_x, sin_x, s_f, dtab, tx, pending)
            if pending:
                w_out, wg, wu, wd = cast_now
            if ctx_after:
                ctx, s_b = _even_mix_call(ctx, *mix_c, zero_state, dtab, mod_c, conv_w, lnw, lnb, w_out, tc)
            x, _ = _even_mix_call(x, *mix_x, s_b, dtab, mod_x, conv_w, lnw, lnb, w_out, tx)
        else:
            args = (nw1, w_in, pool_w[j].astype(BF16), pool_scale[j].reshape(1, POOL_CH),
                    sg_ln_w[j].reshape(1, SG_CH), sg_ln_b[j].reshape(1, SG_CH), sg_w[j].astype(BF16),
                    jnp.broadcast_to(sg_b[j][:, :, None], (SG_GROUPS, CHUNK, SG_GC)), w_out)
            if ctx_after:
                ctx = _odd_call(ctx, mod_c, *args, tc)
            x = _odd_call(x, mod_x, *args, tx)
        if ctx_after:
            ctx, _, _ = _ffn_call(ctx, mod_c, nw2, wg, wu, wd, fnw, tc, False)
        later = layer_matrices(i + 1) if i + 1 < DEPTH else ()
        ada_next = (cvec, ada_w, ada_b, i + 1) if i + 1 < DEPTH else None
        x, bf16_mats, mod_rows = _ffn_call(x, mod_x, nw2, wg, wu, wd, fnw, min(l, 2 * FFN_ROWS), i == DEPTH - 1,
                                           later, ada_next)
    return x
```

```python
import functools

import jax
import jax.numpy as jnp
from jax import lax
from jax.experimental import pallas as pl
from jax.experimental.pallas import tpu as pltpu

F32 = jnp.float32
BF16 = jnp.bfloat16

D_MODEL = 1024
DEPTH = 4
GRID_W = 64
EPS = 1e-6
HEADS = 4
HEAD_DIM = 128
RET_W = HEADS * HEAD_DIM
CHUNK = 128
ROPE_BASE = 10000.0
ROPE_PAIRS = (HEAD_DIM // 8, 3 * HEAD_DIM // 16, 3 * HEAD_DIM // 16)
CONV_CH = 512
CONV_K = 31
CONV_PAD = CONV_K // 2
EVEN_IN = 4 * RET_W + 2 * CONV_CH
POOL_CH = 512
POOL_WINDOWS = (2, 4, 8, 16)
POOL_GC = 128
SG_CH = 512
SG_GROUPS = 4
SG_GC = 128
ODD_IN = POOL_CH + 2 * SG_CH
D_FF = 2816
SUBLANES = 8
BF16_SUBLANES = 16
MOD_COLS = 1024
LANES = 128
FF_BLOCK = 256
OUT_BLOCK = 256
CONV_ROWS = 256
FFN_ROWS = 512
EVEN_MIX_ROWS = 512
ODD_ROWS = 512
HALO = 16
N_DECAY_TABLES = 7
MOD_ROWS = 8
VMEM_LIMIT = 56 * 1024 * 1024


def _dot(a, b):
    return jnp.dot(a, b, preferred_element_type=F32)


def _dot_nt(a, b):
    return lax.dot_general(a, b, (((1,), (1,)), ((), ())), preferred_element_type=F32)


def _dot_tn(a, b):
    return lax.dot_general(a, b, (((0,), (0,)), ((), ())), preferred_element_type=F32)


def _silu(x):
    return x * jax.nn.sigmoid(x)


def _rms_mod(x32, nw, shift, scale):
    y = x32 * lax.rsqrt(jnp.mean(x32 * x32, axis=-1, keepdims=True) + EPS) * nw
    return y * (1.0 + scale) + shift


def _layer_norm(x32, w, b):
    mu = jnp.mean(x32, axis=-1, keepdims=True)
    xc = x32 - mu
    return xc * lax.rsqrt(jnp.mean(xc * xc, axis=-1, keepdims=True) + EPS) * w + b


def _const_spec(shape):
    return pl.BlockSpec(shape, lambda *_: (0,) * len(shape), pipeline_mode=pl.Buffered(1))


def _params(sem):
    return pltpu.CompilerParams(dimension_semantics=sem, vmem_limit_bytes=VMEM_LIMIT)


def _cast_band_specs(stack, layer, bsz, n_tiles):
    _, rows, cols = stack.shape
    n_steps = bsz * n_tiles
    hold = next(h for h in range(1, n_steps + 1)
                if n_steps % h == 0 and rows % (n_steps // h) == 0 and (rows * h // n_steps) % BF16_SUBLANES == 0)
    band = rows * hold // n_steps
    return (pl.BlockSpec((None, band, cols), lambda b, i: (layer, (b * n_tiles + i) // hold, 0)),
            pl.BlockSpec((band, cols), lambda b, i: ((b * n_tiles + i) // hold, 0)))


def _cast_bands(srcs, dsts):
    for src, dst in zip(srcs, dsts):
        dst[...] = src[...].astype(BF16)


def _mod_kernel(c_ref, w_ref, b_ref, o_ref):
    o_ref[...] = _dot(_silu(c_ref[...]).astype(BF16), w_ref[...].astype(BF16)) + b_ref[...]


def _mod_call(c_rows, ada_w, ada_b, layer):
    depth, d, n = ada_w.shape
    tn = MOD_COLS
    return pl.pallas_call(
        _mod_kernel,
        grid=(n // tn,),
        in_specs=[_const_spec((MOD_ROWS, d)),
                  pl.BlockSpec((None, d, tn), lambda j: (layer, 0, j)),
                  pl.BlockSpec((None, 1, tn), lambda j: (layer, 0, j))],
        out_specs=pl.BlockSpec((MOD_ROWS, tn), lambda j: (0, j)),
        out_shape=jax.ShapeDtypeStruct((MOD_ROWS, n), F32),
        compiler_params=_params(("arbitrary",)),
        name="adaln_mod",
    )(c_rows, ada_w, ada_b.reshape(depth, 1, n))


def _log_sigmoid(x):
    return -(jnp.maximum(-x, 0.0) + jnp.log1p(jnp.exp(-jnp.abs(x))))


def _decay_kernel(logit_ref, o_ref):
    j = pl.program_id(0)
    row = lax.broadcasted_iota(jnp.int32, (CHUNK, CHUNK), 0).astype(F32)
    col = lax.broadcasted_iota(jnp.int32, (CHUNK, CHUNK), 1).astype(F32)
    diff = row - col
    lower = diff >= 0.0
    for hd in range(HEADS):
        lgf = _log_sigmoid(jnp.full((CHUNK, CHUNK), logit_ref[(j * 2 + 0) * HEADS + hd], F32))
        lgb = _log_sigmoid(jnp.full((CHUNK, CHUNK), logit_ref[(j * 2 + 1) * HEADS + hd], F32))
        o_ref[0, 0, hd] = jnp.where(lower, jnp.exp(lgf * jnp.where(lower, diff, 0.0)),
                                    jnp.exp(lgb * jnp.where(lower, 0.0, -diff - 1.0)))
        o_ref[0, 1, hd] = jnp.exp(lgf * (CHUNK - 1.0 - row))
        o_ref[0, 2, hd] = jnp.exp(lgb * row)
        o_ref[0, 3, hd] = jnp.exp(lgf * (row + 1.0))
        o_ref[0, 4, hd] = jnp.exp(lgb * (CHUNK - 1.0 - row))
        o_ref[0, 5, hd] = jnp.exp(lgf * float(CHUNK))
        o_ref[0, 6, hd] = jnp.exp(lgb * float(CHUNK))


def _decay_call(ret_decay_logit):
    n_even = ret_decay_logit.shape[0]
    return pl.pallas_call(
        _decay_kernel,
        grid=(n_even,),
        in_specs=[pl.BlockSpec(memory_space=pltpu.SMEM)],
        out_specs=pl.BlockSpec((1, N_DECAY_TABLES, HEADS, CHUNK, CHUNK), lambda j: (j, 0, 0, 0, 0)),
        out_shape=jax.ShapeDtypeStruct((n_even, N_DECAY_TABLES, HEADS, CHUNK, CHUNK), F32),
        compiler_params=_params(("arbitrary",)),
        name="ret_decay",
    )(ret_decay_logit.reshape(-1).astype(F32))


def _even_in_kernel(x_ref, mod_ref, nw_ref, w_ref, cos_ref, sin_ref, s0_ref, dt_ref, *rest, n_cast, tm):
    cast_in = rest[:n_cast]
    q_ref, k_ref, v_ref, gs_ref, u_ref, sprev_ref, sfin_ref = rest[n_cast:n_cast + 7]
    cast_out = rest[n_cast + 7:2 * n_cast + 7]
    state_ref = rest[2 * n_cast + 7]
    b = pl.program_id(0)
    i = pl.program_id(1)
    _cast_bands(cast_in, cast_out)

    @pl.when(i == 0)
    def _():
        state_ref[...] = s0_ref[0]

    h = _rms_mod(x_ref[0], nw_ref[...], mod_ref[b * 6 + 0], mod_ref[b * 6 + 1]).astype(BF16)
    cosf = cos_ref[...]
    sinf = sin_ref[...]

    def rope(t):
        return t * cosf + pltpu.roll(t, HEAD_DIM // 2, 1) * sinf

    pq = _dot(h, w_ref[:, 0:RET_W])
    pk = _dot(h, w_ref[:, RET_W:2 * RET_W])
    pv = _dot(h, w_ref[:, 2 * RET_W:3 * RET_W])
    scale = HEAD_DIM ** -0.5
    for hd in range(HEADS):
        sl = slice(hd * HEAD_DIM, (hd + 1) * HEAD_DIM)
        q_ref[0, :, sl] = (rope(pq[:, sl]) * scale).astype(BF16)
        kr = rope(pk[:, sl]).astype(BF16)
        k_ref[0, :, sl] = kr
        vh = pv[:, sl]
        v_ref[0, :, sl] = vh.astype(BF16)
        zeta = dt_ref[1, hd]
        cdec = dt_ref[5, hd]
        s_cur = state_ref[hd]
        for c in range(tm // CHUNK):
            rows = slice(c * CHUNK, (c + 1) * CHUNK)
            sprev_ref[0, c, hd] = s_cur.astype(BF16)
            wv = (vh[rows] * zeta).astype(BF16)
            s_cur = s_cur * cdec + _dot_tn(kr[rows], wv)
        state_ref[hd] = s_cur

    pg = _dot(h, w_ref[:, 3 * RET_W:4 * RET_W])
    gs_ref[0] = _silu(pg).astype(BF16)
    pa = _dot(h, w_ref[:, 4 * RET_W:4 * RET_W + CONV_CH])
    pb = _dot(h, w_ref[:, 4 * RET_W + CONV_CH:EVEN_IN])
    u_ref[0] = (pa * jax.nn.sigmoid(pb)).astype(BF16)

    @pl.when(i == pl.num_programs(1) - 1)
    def _():
        sfin_ref[0] = state_ref[...]


def _even_in_call(x, mod, nw, w_in, cosf, sinf, s0, dtab, tm, cast=()):
    bsz, l, d = x.shape
    n_chunks = l // CHUNK
    cpt = tm // CHUNK
    tok = lambda w: pl.BlockSpec((1, tm, w), lambda b, i: (b, i, 0))
    state_spec = pl.BlockSpec((1, HEADS, CHUNK, CHUNK), lambda b, i: (b, 0, 0, 0))
    specs = [_cast_band_specs(stack, layer, bsz, l // tm) for stack, layer in cast]
    outs = pl.pallas_call(
        functools.partial(_even_in_kernel, n_cast=len(cast), tm=tm),
        grid=(bsz, l // tm),
        in_specs=[tok(d),
                  _const_spec(mod.shape),
                  _const_spec((1, d)),
                  _const_spec(w_in.shape),
                  pl.BlockSpec((tm, HEAD_DIM), lambda b, i: (i, 0)),
                  pl.BlockSpec((tm, HEAD_DIM), lambda b, i: (i, 0)),
                  state_spec,
                  _const_spec(dtab.shape)] + [s[0] for s in specs],
        out_specs=[tok(RET_W), tok(RET_W), tok(RET_W), tok(RET_W), tok(CONV_CH),
                   pl.BlockSpec((1, cpt, HEADS, CHUNK, CHUNK), lambda b, i: (b, i, 0, 0, 0)),
                   state_spec] + [s[1] for s in specs],
        out_shape=[jax.ShapeDtypeStruct((bsz, l, RET_W), BF16),
                   jax.ShapeDtypeStruct((bsz, l, RET_W), BF16),
                   jax.ShapeDtypeStruct((bsz, l, RET_W), BF16),
                   jax.ShapeDtypeStruct((bsz, l, RET_W), BF16),
                   jax.ShapeDtypeStruct((bsz, l, CONV_CH), BF16),
                   jax.ShapeDtypeStruct((bsz, n_chunks, HEADS, CHUNK, CHUNK), BF16),
                   jax.ShapeDtypeStruct((bsz, HEADS, CHUNK, CHUNK), F32)]
        + [jax.ShapeDtypeStruct(stack.shape[1:], BF16) for stack, _ in cast],
        scratch_shapes=[pltpu.VMEM((HEADS, CHUNK, CHUNK), F32)],
        compiler_params=_params(("arbitrary", "arbitrary")),
        name="even_in",
    )(x, mod, nw, w_in, cosf, sinf, s0, dtab, *[stack for stack, _ in cast])
    return (*outs[:7], tuple(outs[7:]))


def _ctx_state_kernel(x_ref, mod_ref, nw_ref, w_ref, cos_ref, sin_ref, dt_ref, sf_ref, sb_ref):
    b = pl.program_id(0)
    h = _rms_mod(x_ref[0], nw_ref[...], mod_ref[b * 6 + 0], mod_ref[b * 6 + 1]).astype(BF16)
    cosf = cos_ref[...]
    sinf = sin_ref[...]
    pk = _dot(h, w_ref[:, 0:RET_W])
    pv = _dot(h, w_ref[:, RET_W:2 * RET_W])
    n_chunks = x_ref.shape[1] // CHUNK
    for hd in range(HEADS):
        sl = slice(hd * HEAD_DIM, (hd + 1) * HEAD_DIM)
        t = pk[:, sl]
        kr = (t * cosf + pltpu.roll(t, HEAD_DIM // 2, 1) * sinf).astype(BF16)
        vh = pv[:, sl]
        s_f = jnp.zeros((CHUNK, CHUNK), F32)
        s_b = jnp.zeros((CHUNK, CHUNK), F32)
        for c in range(n_chunks):
            rows = slice(c * CHUNK, (c + 1) * CHUNK)
            s_f = s_f * dt_ref[5, hd] + _dot_tn(kr[rows], (vh[rows] * dt_ref[1, hd]).astype(BF16))
        for c in reversed(range(n_chunks)):
            rows = slice(c * CHUNK, (c + 1) * CHUNK)
            s_b = s_b * dt_ref[6, hd] + _dot_tn(kr[rows], (vh[rows] * dt_ref[2, hd]).astype(BF16))
        sf_ref[0, hd] = s_f
        sb_ref[0, hd] = s_b


def _ctx_state_call(ctx, mod, nw, w_kv, cosf, sinf, dtab):
    bsz, lc, d = ctx.shape
    state_spec = pl.BlockSpec((1, HEADS, CHUNK, CHUNK), lambda b: (b, 0, 0, 0))
    return pl.pallas_call(
        _ctx_state_kernel,
        grid=(bsz,),
        in_specs=[pl.BlockSpec((1, lc, d), lambda b: (b, 0, 0)),
                  _const_spec(mod.shape),
                  _const_spec((1, d)),
                  _const_spec(w_kv.shape),
                  _const_spec(cosf.shape),
                  _const_spec(sinf.shape),
                  _const_spec(dtab.shape)],
        out_specs=[state_spec, state_spec],
        out_shape=[jax.ShapeDtypeStruct((bsz, HEADS, CHUNK, CHUNK), F32)] * 2,
        compiler_params=_params(("arbitrary",)),
        name="ctx_state",
    )(ctx, mod, nw, w_kv, cosf, sinf, dtab)


def _even_mix_pieces(b, i, n_tiles, x_ref, q_ref, k_ref, v_ref, gs_ref, u_ref, up_ref, un_ref, sprev_ref,
                     dt_ref, mod_ref, cw_ref, lnw_ref, lnb_ref, wo_ref, state_ref, uext_ref, mix_ref, xo_ref, tm):
    def retention(c, hd):
        rows = slice(c * CHUNK, (c + 1) * CHUNK)
        sl = slice(hd * HEAD_DIM, (hd + 1) * HEAD_DIM)
        qc = q_ref[0, rows, sl]
        kc = k_ref[0, rows, sl]
        vc = v_ref[0, rows, sl]
        s_b = state_ref[hd]
        p = (_dot_nt(qc, kc) * dt_ref[0, hd]).astype(BF16)
        y = _dot(p, vc)
        y = y + _dot(qc, sprev_ref[0, c, hd]) * dt_ref[3, hd]
        y = y + _dot(qc, s_b.astype(BF16)) * dt_ref[4, hd]
        y = y * lax.rsqrt(jnp.mean(y * y, axis=-1, keepdims=True) + EPS)
        mix_ref[rows, sl] = gs_ref[0, rows, sl] * y.astype(BF16)
        wv = (vc.astype(F32) * dt_ref[2, hd]).astype(BF16)
        state_ref[hd] = s_b * dt_ref[6, hd] + _dot_tn(kc, wv)

    def halo():
        uext_ref[0:HALO] = jnp.where(i < n_tiles - 1, up_ref[0].astype(F32), 0.0)
        uext_ref[HALO:HALO + tm] = u_ref[0].astype(F32)
        uext_ref[HALO + tm:2 * HALO + tm] = jnp.where(i > 0, un_ref[0].astype(F32), 0.0)

    def conv(r):
        blocks = []
        for cb in range(CONV_CH // LANES):
            cols = slice(cb * LANES, (cb + 1) * LANES)
            acc = None
            for sub in range(SUBLANES):
                part = None
                for t in range(sub, CONV_K, SUBLANES):
                    start = r * CONV_ROWS + t - sub
                    term = uext_ref[start:start + CONV_ROWS + SUBLANES, cols] * cw_ref[t:t + 1, cols]
                    part = term if part is None else part + term
                shifted = pltpu.roll(part, CONV_ROWS + SUBLANES - (sub + 1), 0)[0:CONV_ROWS]
                acc = shifted if acc is None else acc + shifted
            blocks.append(acc)
        cv = _silu(_layer_norm(jnp.concatenate(blocks, axis=-1), lnw_ref[...], lnb_ref[...]))
        mix_ref[r * CONV_ROWS:(r + 1) * CONV_ROWS, RET_W:RET_W + CONV_CH] = cv.astype(BF16)

    def out_proj(r0, n_rows, n):
        rows = slice(r0, r0 + n_rows)
        cols = slice(n * OUT_BLOCK, (n + 1) * OUT_BLOCK)
        out = _dot(mix_ref[rows], wo_ref[:, cols])
        xo_ref[rows, cols] = x_ref[0, rows, cols] + mod_ref[b * 6 + 2][:, cols] * out

    sub = min(tm, EVEN_MIX_ROWS)
    pieces = [halo]
    for r0 in reversed(range(0, tm, sub)):
        convs = [functools.partial(conv, r) for r in reversed(range(r0 // CONV_ROWS, (r0 + sub) // CONV_ROWS))]
        rets = [functools.partial(retention, c, hd)
                for c in reversed(range(r0 // CHUNK, (r0 + sub) // CHUNK)) for hd in range(HEADS)]
        per = len(rets) // len(convs)
        for r, cv in enumerate(convs):
            pieces += rets[r * per:(r + 1) * per] + [cv]
        pieces += rets[len(convs) * per:]
        pieces += [functools.partial(out_proj, r0, sub, n) for n in range(D_MODEL // OUT_BLOCK)]
    return pieces


def _ffn_pieces(xs_ref, o_ref, b, mod_ref, nw_ref, wg_ref, wu_ref, wd_ref, fnw_ref, h_ref, act_ref, final_norm):
    def prologue():
        x32 = xs_ref[...]
        o_ref[...] = x32
        h_ref[...] = _rms_mod(x32, nw_ref[...], mod_ref[b * 6 + 3], mod_ref[b * 6 + 4]).astype(BF16)

    def gate_up(j):
        cols = slice(j * FF_BLOCK, (j + 1) * FF_BLOCK)
        h = h_ref[...]
        act_ref[:, cols] = (_silu(_dot(h, wg_ref[:, cols])) * _dot(h, wu_ref[:, cols])).astype(BF16)

    def down(n):
        cols = slice(n * OUT_BLOCK, (n + 1) * OUT_BLOCK)
        o_ref[:, cols] = o_ref[:, cols] + mod_ref[b * 6 + 5][:, cols] * _dot(act_ref[...], wd_ref[:, cols])

    def final():
        y = o_ref[...]
        o_ref[...] = y * lax.rsqrt(jnp.mean(y * y, axis=-1, keepdims=True) + EPS) * fnw_ref[...]

    pieces = [prologue]
    pieces += [functools.partial(gate_up, j) for j in range(D_FF // FF_BLOCK)]
    pieces += [functools.partial(down, n) for n in range(D_MODEL // OUT_BLOCK)]
    if final_norm:
        pieces.append(final)
    return pieces


def _even_mix_kernel(x_ref, q_ref, k_ref, v_ref, gs_ref, u_ref, up_ref, un_ref, sprev_ref, sb0_ref,
                     dt_ref, mod_ref, cw_ref, lnw_ref, lnb_ref, wo_ref,
                     o_ref, sbfin_ref, state_ref, uext_ref, mix_ref, *, tm):
    b = pl.program_id(0)
    i = pl.program_id(1)
    n_tiles = pl.num_programs(1)

    @pl.when(i == 0)
    def _():
        state_ref[...] = sb0_ref[0]

    for piece in _even_mix_pieces(b, i, n_tiles, x_ref, q_ref, k_ref, v_ref, gs_ref, u_ref, up_ref, un_ref,
                                  sprev_ref, dt_ref, mod_ref, cw_ref, lnw_ref, lnb_ref, wo_ref,
                                  state_ref, uext_ref, mix_ref, o_ref.at[0], tm):
        piece()

    @pl.when(i == n_tiles - 1)
    def _():
        sbfin_ref[0] = state_ref[...]


def _even_mix_call(x, q, k, v, gs, u, sprev, sb0, dtab, mod, conv_w, ln_w, ln_b, w_out, tm):
    bsz, l, d = x.shape
    n_tiles = l // tm
    cpt = tm // CHUNK
    hpt = tm // HALO
    n_halo = l // HALO
    rev = lambda i: n_tiles - 1 - i
    tok = lambda w: pl.BlockSpec((1, tm, w), lambda b, i: (b, rev(i), 0))
    state_spec = pl.BlockSpec((1, HEADS, CHUNK, CHUNK), lambda b, i: (b, 0, 0, 0))
    return pl.pallas_call(
        functools.partial(_even_mix_kernel, tm=tm),
        grid=(bsz, n_tiles),
        in_specs=[tok(d), tok(RET_W), tok(RET_W), tok(RET_W), tok(RET_W), tok(CONV_CH),
                  pl.BlockSpec((1, HALO, CONV_CH), lambda b, i: (b, jnp.maximum(rev(i) * hpt - 1, 0), 0)),
                  pl.BlockSpec((1, HALO, CONV_CH),
                               lambda b, i: (b, jnp.minimum((rev(i) + 1) * hpt, n_halo - 1), 0)),
                  pl.BlockSpec((1, cpt, HEADS, CHUNK, CHUNK), lambda b, i: (b, rev(i), 0, 0, 0)),
                  state_spec,
                  _const_spec(dtab.shape),
                  _const_spec(mod.shape),
                  _const_spec(conv_w.shape),
                  _const_spec((1, CONV_CH)),
                  _const_spec((1, CONV_CH)),
                  _const_spec(w_out.shape)],
        out_specs=[tok(d), state_spec],
        out_shape=[jax.ShapeDtypeStruct((bsz, l, d), F32),
                   jax.ShapeDtypeStruct((bsz, HEADS, CHUNK, CHUNK), F32)],
        scratch_shapes=[pltpu.VMEM((HEADS, CHUNK, CHUNK), F32),
                        pltpu.VMEM((tm + 2 * HALO, CONV_CH), F32),
                        pltpu.VMEM((tm, RET_W + CONV_CH), BF16)],
        compiler_params=_params(("arbitrary", "arbitrary")),
        name="even_mix",
    )(x, q, k, v, gs, u, u, u, sprev, sb0, dtab, mod, conv_w, ln_w, ln_b, w_out)


def _odd_kernel(x_ref, xp_ref, xn_ref, mod_ref, nw_ref, wi_ref, pw_ref, ps_ref, lnw_ref, lnb_ref,
                sgw_ref, sgb_ref, wo_ref, o_ref, h_ref, pcx_ref, mix_ref, *, tm, seq_len):
    b = pl.program_id(0)
    i = pl.program_id(1)
    n_tiles = pl.num_programs(1)
    nw = nw_ref[...]
    shift = mod_ref[b * 6 + 0]
    scale = mod_ref[b * 6 + 1]

    h_ref[0:HALO] = _rms_mod(xp_ref[0], nw, shift, scale).astype(BF16)
    h_ref[HALO:HALO + tm] = _rms_mod(x_ref[0], nw, shift, scale).astype(BF16)
    h_ref[HALO + tm:2 * HALO + tm] = _rms_mod(xn_ref[0], nw, shift, scale).astype(BF16)

    pcx_ref[0:tm + 2 * HALO] = _dot(h_ref[...], wi_ref[:, 0:POOL_CH])
    pcx_ref[0:HALO] = jnp.where(i > 0, pcx_ref[0:HALO], 0.0)
    pcx_ref[HALO + tm:2 * HALO + tm] = jnp.where(i < n_tiles - 1, pcx_ref[HALO + tm:2 * HALO + tm], 0.0)
    pcx_ref[tm + 2 * HALO:tm + 2 * HALO + SUBLANES] = jnp.zeros((SUBLANES, POOL_CH), F32)

    sub = min(tm, ODD_ROWS)
    for r0 in range(0, tm, sub):
        rows = slice(r0, r0 + sub)
        pos = i * tm + r0 + lax.broadcasted_iota(jnp.int32, (sub, POOL_GC), 0)
        for gi, w in enumerate(POOL_WINDOWS):
            left = w // 2
            right = w - 1 - left
            sl = slice(gi * POOL_GC, (gi + 1) * POOL_GC)
            run = pcx_ref[r0:r0 + sub + 2 * HALO + SUBLANES, sl]
            span = 1
            while span < w:
                n_run = run.shape[0]
                run = (run + pltpu.roll(run, n_run - span, 0))[0:n_run - SUBLANES]
                span *= 2
            win = pltpu.roll(run, run.shape[0] - (HALO - left), 0)[0:sub]
            cnt = jnp.minimum(pos + right, seq_len - 1) - jnp.maximum(pos - left, 0) + 1
            m = win / cnt.astype(F32) - pcx_ref[HALO + r0:HALO + r0 + sub, sl]
            po = _dot(m.astype(BF16), pw_ref[gi]) * ps_ref[:, sl]
            mix_ref[rows, sl] = po.astype(BF16)

        pd = _dot(h_ref[HALO + r0:HALO + r0 + sub], wi_ref[:, POOL_CH:ODD_IN])
        z = 0.5 * pd * (1.0 + lax.erf(pd * (2.0 ** -0.5)))
        zu = z[:, 0:SG_CH]
        zv = _layer_norm(z[:, SG_CH:2 * SG_CH], lnw_ref[...], lnb_ref[...]).astype(BF16)
        for c in range(sub // CHUNK):
            crows = slice(c * CHUNK, (c + 1) * CHUNK)
            for g in range(SG_GROUPS):
                sl = slice(g * SG_GC, (g + 1) * SG_GC)
                s = _dot(sgw_ref[g], zv[crows, sl]) + sgb_ref[g]
                mix_ref[r0 + c * CHUNK:r0 + (c + 1) * CHUNK, POOL_CH + g * SG_GC:POOL_CH + (g + 1) * SG_GC] = (
                    zu[crows, sl] * s).astype(BF16)

        out = _dot(mix_ref[rows], wo_ref[...])
        o_ref[0, rows] = x_ref[0, rows] + mod_ref[b * 6 + 2] * out


def _odd_call(x, mod, nw, w_in, pool_w, pool_scale, ln_w, ln_b, sg_w, sg_b_full, w_out, tm):
    bsz, l, d = x.shape
    hpt = tm // HALO
    n_halo = l // HALO
    tok = pl.BlockSpec((1, tm, d), lambda b, i: (b, i, 0))
    return pl.pallas_call(
        functools.partial(_odd_kernel, tm=tm, seq_len=l),
        grid=(bsz, l // tm),
        in_specs=[tok,
                  pl.BlockSpec((1, HALO, d), lambda b, i: (b, jnp.maximum(i * hpt - 1, 0), 0)),
                  pl.BlockSpec((1, HALO, d), lambda b, i: (b, jnp.minimum((i + 1) * hpt, n_halo - 1), 0)),
                  _const_spec(mod.shape),
                  _const_spec((1, d)),
                  _const_spec(w_in.shape),
                  _const_spec(pool_w.shape),
                  _const_spec((1, POOL_CH)),
                  _const_spec((1, SG_CH)),
                  _const_spec((1, SG_CH)),
                  _const_spec(sg_w.shape),
                  _const_spec(sg_b_full.shape),
                  _const_spec(w_out.shape)],
        out_specs=tok,
        out_shape=jax.ShapeDtypeStruct((bsz, l, d), F32),
        scratch_shapes=[pltpu.VMEM((tm + 2 * HALO, d), BF16),
                        pltpu.VMEM((tm + 2 * HALO + SUBLANES, POOL_CH), F32),
                        pltpu.VMEM((tm, POOL_CH + SG_CH), BF16)],
        compiler_params=_params(("arbitrary", "arbitrary")),
        name="odd_mix",
    )(x, x, x, mod, nw, w_in, pool_w, pool_scale, ln_w, ln_b, sg_w, sg_b_full, w_out)


def _ffn_kernel(x_ref, mod_ref, nw_ref, wg_ref, wu_ref, wd_ref, fnw_ref, *rest, n_cast, n_ada, final_norm):
    cast_in, rest = rest[:n_cast], rest[n_cast:]
    ada_in, rest = rest[:3 * n_ada], rest[3 * n_ada:]
    o_ref, rest = rest[0], rest[1:]
    cast_out, rest = rest[:n_cast], rest[n_cast:]
    ada_out, (h_ref, act_ref) = rest[:n_ada], rest[n_ada:]
    _cast_bands(cast_in, cast_out)
    if n_ada:
        c_ref, aw_ref, ab_ref = ada_in
        mo_ref, = ada_out

        @pl.when(jnp.logical_and(pl.program_id(0) == 0, pl.program_id(1) == 0))
        def _():
            mo_ref[...] = jnp.broadcast_to(ab_ref[...], mo_ref.shape)

        mo_ref[...] += _dot(_silu(c_ref[...]).astype(BF16), aw_ref[...].astype(BF16))
    tm = x_ref.shape[1]
    subs = []
    for r0 in range(0, tm, min(tm, FFN_ROWS)):
        rows = pl.ds(r0, min(tm, FFN_ROWS))
        subs.append(_ffn_pieces(x_ref.at[0, rows], o_ref.at[0, rows], pl.program_id(0), mod_ref, nw_ref,
                                wg_ref, wu_ref, wd_ref, fnw_ref, h_ref.at[rows], act_ref.at[rows], final_norm))
    order = [subs[0][0]]
    for k, pieces in enumerate(subs):
        order.append(pieces[1])
        if k + 1 < len(subs):
            order.append(subs[k + 1][0])
        order += pieces[2:]
    for piece in order:
        piece()


def _ffn_call(x, mod, nw, wg, wu, wd, fnw, tm, final_norm, cast=(), ada=None):
    bsz, l, d = x.shape
    n_tiles = l // tm
    n_steps = bsz * n_tiles
    tok = pl.BlockSpec((1, tm, d), lambda b, i: (b, i, 0))
    specs = [_cast_band_specs(stack, layer, bsz, n_tiles) for stack, layer in cast]
    ada_ops, ada_in_specs, ada_out_specs, ada_out_shape = [], [], [], []
    if ada is not None:
        c_rows, ada_w, ada_b, layer = ada
        n_mod = ada_w.shape[2]
        band = d // n_steps
        assert d % n_steps == 0 and band % SUBLANES == 0
        c_bands = c_rows.reshape(MOD_ROWS, n_steps, band).transpose(1, 0, 2)
        ada_ops = [c_bands, ada_w, ada_b.reshape(ada_b.shape[0], 1, n_mod)]
        ada_in_specs = [pl.BlockSpec((None, MOD_ROWS, band), lambda b, i: (b * n_tiles + i, 0, 0)),
                        pl.BlockSpec((None, band, n_mod), lambda b, i: (layer, b * n_tiles + i, 0)),
                        pl.BlockSpec((None, 1, n_mod), lambda b, i: (layer, 0, 0))]
        ada_out_specs = [pl.BlockSpec((MOD_ROWS, n_mod), lambda b, i: (0, 0))]
        ada_out_shape = [jax.ShapeDtypeStruct((MOD_ROWS, n_mod), F32)]
    outs = pl.pallas_call(
        functools.partial(_ffn_kernel, n_cast=len(cast), n_ada=len(ada_out_specs), final_norm=final_norm),
        grid=(bsz, n_tiles),
        in_specs=[tok,
                  _const_spec(mod.shape),
                  _const_spec((1, d)),
                  _const_spec(wg.shape),
                  _const_spec(wu.shape),
                  _const_spec(wd.shape),
                  _const_spec((1, d))] + [s[0] for s in specs] + ada_in_specs,
        out_specs=[tok] + [s[1] for s in specs] + ada_out_specs,
        out_shape=[jax.ShapeDtypeStruct((bsz, l, d), F32)]
        + [jax.ShapeDtypeStruct(stack.shape[1:], BF16) for stack, _ in cast] + ada_out_shape,
        scratch_shapes=[pltpu.VMEM((tm, d), BF16), pltpu.VMEM((tm, D_FF), BF16)],
        compiler_params=_params(("arbitrary", "arbitrary")),
        name="ffn",
    )(x, mod, nw, wg, wu, wd, fnw, *[stack for stack, _ in cast], *ada_ops)
    n_cast = len(cast)
    return outs[0], tuple(outs[1:1 + n_cast]), (outs[1 + n_cast] if ada is not None else None)


def _rope_tables(p_seq, p_row, p_col):
    parts = []
    for p, n in zip((p_seq, p_row, p_col), ROPE_PAIRS):
        freq = ROPE_BASE ** (-jnp.arange(n, dtype=F32) / n)
        parts.append(p[:, None] * freq[None, :])
    ang = jnp.concatenate(parts, axis=-1)
    cos, sin = jnp.cos(ang), jnp.sin(ang)
    return jnp.concatenate([cos, cos], axis=-1), jnp.concatenate([-sin, sin], axis=-1)


def _tile(l):
    return min(l, 1024)


def kernel(x, c, ctx, c_ctx, ada_w, ada_b, norm_w, even_w_in, even_w_out, ret_decay_logit, conv_dw_w, conv_ln_w, conv_ln_b, odd_w_in, odd_w_out, pool_w, pool_scale, sg_ln_w, sg_ln_b, sg_w, sg_b, ffn_w_gate, ffn_w_up, ffn_w_down, final_norm_w):
    bsz, l, d = x.shape
    lc = ctx.shape[1]
    assert d == D_MODEL and l % _tile(l) == 0 and l % CHUNK == 0 and lc % CHUNK == 0 and bsz + 1 <= MOD_ROWS
    rows = l // GRID_W
    grid_r = jnp.broadcast_to(jnp.arange(rows, dtype=F32)[:, None], (rows, GRID_W)).reshape(-1)
    grid_c = jnp.broadcast_to(jnp.arange(GRID_W, dtype=F32)[None, :], (rows, GRID_W)).reshape(-1)
    cos_x, sin_x = _rope_tables(jnp.full((l,), lc, F32), grid_r, grid_c)
    zeros_c = jnp.zeros((lc,), F32)
    cos_c, sin_c = _rope_tables(jnp.arange(lc, dtype=F32), zeros_c, zeros_c)

    cvec = jnp.concatenate([c, c_ctx[None, :], jnp.zeros((MOD_ROWS - bsz - 1, d), F32)], axis=0)
    mod_rows = _mod_call(cvec, ada_w, ada_b, 0)
    dtabs = _decay_call(ret_decay_logit)
    zero_state = jnp.zeros((bsz, HEADS, CHUNK, CHUNK), F32)
    fnw = final_norm_w.reshape(1, d)
    tx, tc = _tile(l), _tile(lc)

    def layer_matrices(i):
        mix_in, mix_out = (even_w_in, even_w_out) if i % 2 == 0 else (odd_w_in, odd_w_out)
        return (mix_in, i // 2), (mix_out, i // 2), (ffn_w_gate, i), (ffn_w_up, i), (ffn_w_down, i)

    first = layer_matrices(0)
    bf16_mats = (first[0][0][first[0][1]].astype(BF16),) + (None,) * (len(first) - 1)
    for i in range(DEPTH):
        j = i // 2
        even = i % 2 == 0
        ctx_after = any(m % 2 == 0 for m in range(i + 1, DEPTH))
        mod_x = mod_rows[:bsz].reshape(bsz * 6, 1, d)
        mod_c = jnp.broadcast_to(mod_rows[bsz].reshape(1, 6, 1, d), (bsz, 6, 1, d)).reshape(bsz * 6, 1, d)
        nw1 = norm_w[i, 0].reshape(1, d)
        nw2 = norm_w[i, 1].reshape(1, d)
        w_in, w_out, wg, wu, wd = bf16_mats
        if even:
            conv_w = conv_dw_w[j]
            lnw = conv_ln_w[j].reshape(1, CONV_CH)
            lnb = conv_ln_b[j].reshape(1, CONV_CH)
            dtab = dtabs[j]
            if ctx_after:
                *mix_c, s_f, _ = _even_in_call(ctx, mod_c, nw1, w_in, cos_c, sin_c, zero_state, dtab, tc)
            else:
                s_f, s_b = _ctx_state_call(ctx, mod_c, nw1, w_in[:, RET_W:3 * RET_W], cos_c, sin_c, dtab)
            pending = layer_matrices(i)[1:] if w_out is None else ()
            *mix_x, _, cast_now = _even_in_call(x, mod_x, nw1, w_in, cos_x, sin_x, s_f, dtab, tx, pending)
            if pending:
                w_out, wg, wu, wd = cast_now
            if ctx_after:
                ctx, s_b = _even_mix_call(ctx, *mix_c, zero_state, dtab, mod_c, conv_w, lnw, lnb, w_out, tc)
            x, _ = _even_mix_call(x, *mix_x, s_b, dtab, mod_x, conv_w, lnw, lnb, w_out, tx)
        else:
            args = (nw1, w_in, pool_w[j].astype(BF16), pool_scale[j].reshape(1, POOL_CH),
                    sg_ln_w[j].reshape(1, SG_CH), sg_ln_b[j].reshape(1, SG_CH), sg_w[j].astype(BF16),
                    jnp.broadcast_to(sg_b[j][:, :, None], (SG_GROUPS, CHUNK, SG_GC)), w_out)
            if ctx_after:
                ctx = _odd_call(ctx, mod_c, *args, tc)
            x = _odd_call(x, mod_x, *args, tx)
        if ctx_after:
            ctx, _, _ = _ffn_call(ctx, mod_c, nw2, wg, wu, wd, fnw, tc, False)
        later = layer_matrices(i + 1) if i + 1 < DEPTH else ()
        ada_next = (cvec, ada_w, ada_b, i + 1) if i + 1 < DEPTH else None
        x, bf16_mats, mod_rows = _ffn_call(x, mod_x, nw2, wg, wu, wd, fnw, min(l, 2 * FFN_ROWS), i == DEPTH - 1,
                                           later, ada_next)
    return x
```

```python
import functools

import jax
import jax.numpy as jnp
from jax import lax
from jax.experimental import pallas as pl
from jax.experimental.pallas import tpu as pltpu

F32 = jnp.float32
BF16 = jnp.bfloat16

D_MODEL = 1024
DEPTH = 4
GRID_W = 64
EPS = 1e-6
HEADS = 4
HEAD_DIM = 128
RET_W = HEADS * HEAD_DIM
CHUNK = 128
ROPE_BASE = 10000.0
ROPE_PAIRS = (HEAD_DIM // 8, 3 * HEAD_DIM // 16, 3 * HEAD_DIM // 16)
CONV_CH = 512
CONV_K = 31
CONV_PAD = CONV_K // 2
EVEN_IN = 4 * RET_W + 2 * CONV_CH
POOL_CH = 512
POOL_WINDOWS = (2, 4, 8, 16)
POOL_GC = 128
SG_CH = 512
SG_GROUPS = 4
SG_GC = 128
ODD_IN = POOL_CH + 2 * SG_CH
D_FF = 2816
SUBLANES = 8
BF16_SUBLANES = 16
MOD_COLS = 1024
LANES = 128
FF_BLOCK = 256
OUT_BLOCK = 256
CONV_ROWS = 256
FFN_ROWS = 512
EVEN_IN_ROWS = 512
ODD_ROWS = 512
HALO = 16
N_DECAY_TABLES = 7
MOD_ROWS = 8
VMEM_LIMIT = 56 * 1024 * 1024


def _dot(a, b):
    return jnp.dot(a, b, preferred_element_type=F32)


def _dot_nt(a, b):
    return lax.dot_general(a, b, (((1,), (1,)), ((), ())), preferred_element_type=F32)


def _dot_tn(a, b):
    return lax.dot_general(a, b, (((0,), (0,)), ((), ())), preferred_element_type=F32)


def _silu(x):
    return x * jax.nn.sigmoid(x)


def _rms_mod(x32, nw, shift, scale):
    y = x32 * lax.rsqrt(jnp.mean(x32 * x32, axis=-1, keepdims=True) + EPS) * nw
    return y * (1.0 + scale) + shift


def _layer_norm(x32, w, b):
    mu = jnp.mean(x32, axis=-1, keepdims=True)
    xc = x32 - mu
    return xc * lax.rsqrt(jnp.mean(xc * xc, axis=-1, keepdims=True) + EPS) * w + b


def _const_spec(shape):
    return pl.BlockSpec(shape, lambda *_: (0,) * len(shape), pipeline_mode=pl.Buffered(1))


def _params(sem):
    return pltpu.CompilerParams(dimension_semantics=sem, vmem_limit_bytes=VMEM_LIMIT)


def _cast_band_specs(stack, layer, bsz, n_tiles):
    _, rows, cols = stack.shape
    n_steps = bsz * n_tiles
    hold = next(h for h in range(1, n_steps + 1)
                if n_steps % h == 0 and rows % (n_steps // h) == 0 and (rows * h // n_steps) % BF16_SUBLANES == 0)
    band = rows * hold // n_steps
    return (pl.BlockSpec((None, band, cols), lambda b, i: (layer, (b * n_tiles + i) // hold, 0)),
            pl.BlockSpec((band, cols), lambda b, i: ((b * n_tiles + i) // hold, 0)))


def _cast_bands(srcs, dsts):
    for src, dst in zip(srcs, dsts):
        dst[...] = src[...].astype(BF16)


def _mod_kernel(c_ref, w_ref, b_ref, o_ref):
    o_ref[...] = _dot(_silu(c_ref[...]).astype(BF16), w_ref[...].astype(BF16)) + b_ref[...]


def _mod_call(c_rows, ada_w, ada_b, layer):
    depth, d, n = ada_w.shape
    tn = MOD_COLS
    return pl.pallas_call(
        _mod_kernel,
        grid=(n // tn,),
        in_specs=[_const_spec((MOD_ROWS, d)),
                  pl.BlockSpec((None, d, tn), lambda j: (layer, 0, j)),
                  pl.BlockSpec((None, 1, tn), lambda j: (layer, 0, j))],
        out_specs=pl.BlockSpec((MOD_ROWS, tn), lambda j: (0, j)),
        out_shape=jax.ShapeDtypeStruct((MOD_ROWS, n), F32),
        compiler_params=_params(("arbitrary",)),
        name="adaln_mod",
    )(c_rows, ada_w, ada_b.reshape(depth, 1, n))


def _log_sigmoid(x):
    return -(jnp.maximum(-x, 0.0) + jnp.log1p(jnp.exp(-jnp.abs(x))))


def _decay_kernel(logit_ref, o_ref):
    j = pl.program_id(0)
    row = lax.broadcasted_iota(jnp.int32, (CHUNK, CHUNK), 0).astype(F32)
    col = lax.broadcasted_iota(jnp.int32, (CHUNK, CHUNK), 1).astype(F32)
    diff = row - col
    lower = diff >= 0.0
    for hd in range(HEADS):
        lgf = _log_sigmoid(jnp.full((CHUNK, CHUNK), logit_ref[(j * 2 + 0) * HEADS + hd], F32))
        lgb = _log_sigmoid(jnp.full((CHUNK, CHUNK), logit_ref[(j * 2 + 1) * HEADS + hd], F32))
        o_ref[0, 0, hd] = jnp.where(lower, jnp.exp(lgf * jnp.where(lower, diff, 0.0)),
                                    jnp.exp(lgb * jnp.where(lower, 0.0, -diff - 1.0)))
        o_ref[0, 1, hd] = jnp.exp(lgf * (CHUNK - 1.0 - row))
        o_ref[0, 2, hd] = jnp.exp(lgb * row)
        o_ref[0, 3, hd] = jnp.exp(lgf * (row + 1.0))
        o_ref[0, 4, hd] = jnp.exp(lgb * (CHUNK - 1.0 - row))
        o_ref[0, 5, hd] = jnp.exp(lgf * float(CHUNK))
        o_ref[0, 6, hd] = jnp.exp(lgb * float(CHUNK))


def _decay_call(ret_decay_logit):
    n_even = ret_decay_logit.shape[0]
    return pl.pallas_call(
        _decay_kernel,
        grid=(n_even,),
        in_specs=[pl.BlockSpec(memory_space=pltpu.SMEM)],
        out_specs=pl.BlockSpec((1, N_DECAY_TABLES, HEADS, CHUNK, CHUNK), lambda j: (j, 0, 0, 0, 0)),
        out_shape=jax.ShapeDtypeStruct((n_even, N_DECAY_TABLES, HEADS, CHUNK, CHUNK), F32),
        compiler_params=_params(("arbitrary",)),
        name="ret_decay",
    )(ret_decay_logit.reshape(-1).astype(F32))


def _even_in_kernel(x_ref, mod_ref, nw_ref, w_ref, cos_ref, sin_ref, s0_ref, dt_ref, *rest, n_cast, tm):
    cast_in = rest[:n_cast]
    q_ref, k_ref, v_ref, gs_ref, u_ref, sprev_ref, sfin_ref = rest[n_cast:n_cast + 7]
    cast_out = rest[n_cast + 7:2 * n_cast + 7]
    state_ref = rest[2 * n_cast + 7]
    b = pl.program_id(0)
    i = pl.program_id(1)
    _cast_bands(cast_in, cast_out)

    @pl.when(i == 0)
    def _():
        state_ref[...] = s0_ref[0]

    scale = HEAD_DIM ** -0.5
    sub = min(tm, EVEN_IN_ROWS)
    for r0 in range(0, tm, sub):
        tile = slice(r0, r0 + sub)
        h = _rms_mod(x_ref[0, tile], nw_ref[...], mod_ref[b * 6 + 0], mod_ref[b * 6 + 1]).astype(BF16)
        cosf = cos_ref[tile]
        sinf = sin_ref[tile]

        def rope(t):
            return t * cosf + pltpu.roll(t, HEAD_DIM // 2, 1) * sinf

        pq = _dot(h, w_ref[:, 0:RET_W])
        pk = _dot(h, w_ref[:, RET_W:2 * RET_W])
        pv = _dot(h, w_ref[:, 2 * RET_W:3 * RET_W])
        for hd in range(HEADS):
            sl = slice(hd * HEAD_DIM, (hd + 1) * HEAD_DIM)
            q_ref[0, tile, sl] = (rope(pq[:, sl]) * scale).astype(BF16)
            kr = rope(pk[:, sl]).astype(BF16)
            k_ref[0, tile, sl] = kr
            vh = pv[:, sl]
            v_ref[0, tile, sl] = vh.astype(BF16)
            zeta = dt_ref[1, hd]
            cdec = dt_ref[5, hd]
            s_cur = state_ref[hd]
            for c in range(sub // CHUNK):
                rows = slice(c * CHUNK, (c + 1) * CHUNK)
                sprev_ref[0, r0 // CHUNK + c, hd] = s_cur.astype(BF16)
                wv = (vh[rows] * zeta).astype(BF16)
                s_cur = s_cur * cdec + _dot_tn(kr[rows], wv)
            state_ref[hd] = s_cur

        pg = _dot(h, w_ref[:, 3 * RET_W:4 * RET_W])
        gs_ref[0, tile] = _silu(pg).astype(BF16)
        pa = _dot(h, w_ref[:, 4 * RET_W:4 * RET_W + CONV_CH])
        pb = _dot(h, w_ref[:, 4 * RET_W + CONV_CH:EVEN_IN])
        u_ref[0, tile] = (pa * jax.nn.sigmoid(pb)).astype(BF16)

    @pl.when(i == pl.num_programs(1) - 1)
    def _():
        sfin_ref[0] = state_ref[...]


def _even_in_call(x, mod, nw, w_in, cosf, sinf, s0, dtab, tm, cast=()):
    bsz, l, d = x.shape
    n_chunks = l // CHUNK
    cpt = tm // CHUNK
    tok = lambda w: pl.BlockSpec((1, tm, w), lambda b, i: (b, i, 0))
    state_spec = pl.BlockSpec((1, HEADS, CHUNK, CHUNK), lambda b, i: (b, 0, 0, 0))
    specs = [_cast_band_specs(stack, layer, bsz, l // tm) for stack, layer in cast]
    outs = pl.pallas_call(
        functools.partial(_even_in_kernel, n_cast=len(cast), tm=tm),
        grid=(bsz, l // tm),
        in_specs=[tok(d),
                  _const_spec(mod.shape),
                  _const_spec((1, d)),
                  _const_spec(w_in.shape),
                  pl.BlockSpec((tm, HEAD_DIM), lambda b, i: (i, 0)),
                  pl.BlockSpec((tm, HEAD_DIM), lambda b, i: (i, 0)),
                  state_spec,
                  _const_spec(dtab.shape)] + [s[0] for s in specs],
        out_specs=[tok(RET_W), tok(RET_W), tok(RET_W), tok(RET_W), tok(CONV_CH),
                   pl.BlockSpec((1, cpt, HEADS, CHUNK, CHUNK), lambda b, i: (b, i, 0, 0, 0)),
                   state_spec] + [s[1] for s in specs],
        out_shape=[jax.ShapeDtypeStruct((bsz, l, RET_W), BF16),
                   jax.ShapeDtypeStruct((bsz, l, RET_W), BF16),
                   jax.ShapeDtypeStruct((bsz, l, RET_W), BF16),
                   jax.ShapeDtypeStruct((bsz, l, RET_W), BF16),
                   jax.ShapeDtypeStruct((bsz, l, CONV_CH), BF16),
                   jax.ShapeDtypeStruct((bsz, n_chunks, HEADS, CHUNK, CHUNK), BF16),
                   jax.ShapeDtypeStruct((bsz, HEADS, CHUNK, CHUNK), F32)]
        + [jax.ShapeDtypeStruct(stack.shape[1:], BF16) for stack, _ in cast],
        scratch_shapes=[pltpu.VMEM((HEADS, CHUNK, CHUNK), F32)],
        compiler_params=_params(("arbitrary", "arbitrary")),
        name="even_in",
    )(x, mod, nw, w_in, cosf, sinf, s0, dtab, *[stack for stack, _ in cast])
    return (*outs[:7], tuple(outs[7:]))


def _ctx_state_kernel(x_ref, mod_ref, nw_ref, w_ref, cos_ref, sin_ref, dt_ref, sf_ref, sb_ref):
    b = pl.program_id(0)
    h = _rms_mod(x_ref[0], nw_ref[...], mod_ref[b * 6 + 0], mod_ref[b * 6 + 1]).astype(BF16)
    cosf = cos_ref[...]
    sinf = sin_ref[...]
    pk = _dot(h, w_ref[:, 0:RET_W])
    pv = _dot(h, w_ref[:, RET_W:2 * RET_W])
    n_chunks = x_ref.shape[1] // CHUNK
    for hd in range(HEADS):
        sl = slice(hd * HEAD_DIM, (hd + 1) * HEAD_DIM)
        t = pk[:, sl]
        kr = (t * cosf + pltpu.roll(t, HEAD_DIM // 2, 1) * sinf).astype(BF16)
        vh = pv[:, sl]
        s_f = jnp.zeros((CHUNK, CHUNK), F32)
        s_b = jnp.zeros((CHUNK, CHUNK), F32)
        for c in range(n_chunks):
            rows = slice(c * CHUNK, (c + 1) * CHUNK)
            s_f = s_f * dt_ref[5, hd] + _dot_tn(kr[rows], (vh[rows] * dt_ref[1, hd]).astype(BF16))
        for c in reversed(range(n_chunks)):
            rows = slice(c * CHUNK, (c + 1) * CHUNK)
            s_b = s_b * dt_ref[6, hd] + _dot_tn(kr[rows], (vh[rows] * dt_ref[2, hd]).astype(BF16))
        sf_ref[0, hd] = s_f
        sb_ref[0, hd] = s_b


def _ctx_state_call(ctx, mod, nw, w_kv, cosf, sinf, dtab):
    bsz, lc, d = ctx.shape
    state_spec = pl.BlockSpec((1, HEADS, CHUNK, CHUNK), lambda b: (b, 0, 0, 0))
    return pl.pallas_call(
        _ctx_state_kernel,
        grid=(bsz,),
        in_specs=[pl.BlockSpec((1, lc, d), lambda b: (b, 0, 0)),
                  _const_spec(mod.shape),
                  _const_spec((1, d)),
                  _const_spec(w_kv.shape),
                  _const_spec(cosf.shape),
                  _const_spec(sinf.shape),
                  _const_spec(dtab.shape)],
        out_specs=[state_spec, state_spec],
        out_shape=[jax.ShapeDtypeStruct((bsz, HEADS, CHUNK, CHUNK), F32)] * 2,
        compiler_params=_params(("arbitrary",)),
        name="ctx_state",
    )(ctx, mod, nw, w_kv, cosf, sinf, dtab)


def _even_mix_pieces(b, i, n_tiles, x_ref, q_ref, k_ref, v_ref, gs_ref, u_ref, up_ref, un_ref, sprev_ref,
                     dt_ref, mod_ref, cw_ref, lnw_ref, lnb_ref, wo_ref, state_ref, uext_ref, mix_ref, xo_ref, tm):
    def retention(c, hd):
        rows = slice(c * CHUNK, (c + 1) * CHUNK)
        sl = slice(hd * HEAD_DIM, (hd + 1) * HEAD_DIM)
        qc = q_ref[0, rows, sl]
        kc = k_ref[0, rows, sl]
        vc = v_ref[0, rows, sl]
        s_b = state_ref[hd]
        p = (_dot_nt(qc, kc) * dt_ref[0, hd]).astype(BF16)
        y = _dot(p, vc)
        y = y + _dot(qc, sprev_ref[0, c, hd]) * dt_ref[3, hd]
        y = y + _dot(qc, s_b.astype(BF16)) * dt_ref[4, hd]
        y = y * lax.rsqrt(jnp.mean(y * y, axis=-1, keepdims=True) + EPS)
        mix_ref[rows, sl] = gs_ref[0, rows, sl] * y.astype(BF16)
        wv = (vc.astype(F32) * dt_ref[2, hd]).astype(BF16)
        state_ref[hd] = s_b * dt_ref[6, hd] + _dot_tn(kc, wv)

    def halo():
        uext_ref[0:HALO] = jnp.where(i < n_tiles - 1, up_ref[0].astype(F32), 0.0)
        uext_ref[HALO:HALO + tm] = u_ref[0].astype(F32)
        uext_ref[HALO + tm:2 * HALO + tm] = jnp.where(i > 0, un_ref[0].astype(F32), 0.0)

    def conv(r):
        blocks = []
        for cb in range(CONV_CH // LANES):
            cols = slice(cb * LANES, (cb + 1) * LANES)
            acc = None
            for sub in range(SUBLANES):
                part = None
                for t in range(sub, CONV_K, SUBLANES):
                    start = r * CONV_ROWS + t - sub
                    term = uext_ref[start:start + CONV_ROWS + SUBLANES, cols] * cw_ref[t:t + 1, cols]
                    part = term if part is None else part + term
                shifted = pltpu.roll(part, CONV_ROWS + SUBLANES - (sub + 1), 0)[0:CONV_ROWS]
                acc = shifted if acc is None else acc + shifted
            blocks.append(acc)
        cv = _silu(_layer_norm(jnp.concatenate(blocks, axis=-1), lnw_ref[...], lnb_ref[...]))
        mix_ref[r * CONV_ROWS:(r + 1) * CONV_ROWS, RET_W:RET_W + CONV_CH] = cv.astype(BF16)

    def out_proj(n):
        cols = slice(n * OUT_BLOCK, (n + 1) * OUT_BLOCK)
        out = _dot(mix_ref[...], wo_ref[:, cols])
        xo_ref[:, cols] = x_ref[0, :, cols] + mod_ref[b * 6 + 2][:, cols] * out

    pieces = [halo]
    convs = [functools.partial(conv, r) for r in range(tm // CONV_ROWS)]
    rets = [functools.partial(retention, c, hd) for c in reversed(range(tm // CHUNK)) for hd in range(HEADS)]
    per = len(rets) // len(convs)
    for r, cv in enumerate(convs):
        pieces += rets[r * per:(r + 1) * per] + [cv]
    pieces += rets[len(convs) * per:]
    pieces += [functools.partial(out_proj, n) for n in range(D_MODEL // OUT_BLOCK)]
    return pieces


def _ffn_pieces(xs_ref, o_ref, b, mod_ref, nw_ref, wg_ref, wu_ref, wd_ref, fnw_ref, h_ref, act_ref, final_norm):
    def prologue():
        x32 = xs_ref[...]
        o_ref[...] = x32
        h_ref[...] = _rms_mod(x32, nw_ref[...], mod_ref[b * 6 + 3], mod_ref[b * 6 + 4]).astype(BF16)

    def gate_up(j):
        cols = slice(j * FF_BLOCK, (j + 1) * FF_BLOCK)
        h = h_ref[...]
        act_ref[:, cols] = (_silu(_dot(h, wg_ref[:, cols])) * _dot(h, wu_ref[:, cols])).astype(BF16)

    def down(n):
        cols = slice(n * OUT_BLOCK, (n + 1) * OUT_BLOCK)
        o_ref[:, cols] = o_ref[:, cols] + mod_ref[b * 6 + 5][:, cols] * _dot(act_ref[...], wd_ref[:, cols])

    def final():
        y = o_ref[...]
        o_ref[...] = y * lax.rsqrt(jnp.mean(y * y, axis=-1, keepdims=True) + EPS) * fnw_ref[...]

    pieces = [prologue]
    pieces += [functools.partial(gate_up, j) for j in range(D_FF // FF_BLOCK)]
    pieces += [functools.partial(down, n) for n in range(D_MODEL // OUT_BLOCK)]
    if final_norm:
        pieces.append(final)
    return pieces


def _even_mix_kernel(x_ref, q_ref, k_ref, v_ref, gs_ref, u_ref, up_ref, un_ref, sprev_ref, sb0_ref,
                     dt_ref, mod_ref, cw_ref, lnw_ref, lnb_ref, wo_ref,
                     o_ref, sbfin_ref, state_ref, uext_ref, mix_ref, *, tm):
    b = pl.program_id(0)
    i = pl.program_id(1)
    n_tiles = pl.num_programs(1)

    @pl.when(i == 0)
    def _():
        state_ref[...] = sb0_ref[0]

    for piece in _even_mix_pieces(b, i, n_tiles, x_ref, q_ref, k_ref, v_ref, gs_ref, u_ref, up_ref, un_ref,
                                  sprev_ref, dt_ref, mod_ref, cw_ref, lnw_ref, lnb_ref, wo_ref,
                                  state_ref, uext_ref, mix_ref, o_ref.at[0], tm):
        piece()

    @pl.when(i == n_tiles - 1)
    def _():
        sbfin_ref[0] = state_ref[...]


def _even_mix_call(x, q, k, v, gs, u, sprev, sb0, dtab, mod, conv_w, ln_w, ln_b, w_out, tm):
    bsz, l, d = x.shape
    n_tiles = l // tm
    cpt = tm // CHUNK
    hpt = tm // HALO
    n_halo = l // HALO
    rev = lambda i: n_tiles - 1 - i
    tok = lambda w: pl.BlockSpec((1, tm, w), lambda b, i: (b, rev(i), 0))
    state_spec = pl.BlockSpec((1, HEADS, CHUNK, CHUNK), lambda b, i: (b, 0, 0, 0))
    return pl.pallas_call(
        functools.partial(_even_mix_kernel, tm=tm),
        grid=(bsz, n_tiles),
        in_specs=[tok(d), tok(RET_W), tok(RET_W), tok(RET_W), tok(RET_W), tok(CONV_CH),
                  pl.BlockSpec((1, HALO, CONV_CH), lambda b, i: (b, jnp.maximum(rev(i) * hpt - 1, 0), 0)),
                  pl.BlockSpec((1, HALO, CONV_CH),
                               lambda b, i: (b, jnp.minimum((rev(i) + 1) * hpt, n_halo - 1), 0)),
                  pl.BlockSpec((1, cpt, HEADS, CHUNK, CHUNK), lambda b, i: (b, rev(i), 0, 0, 0)),
                  state_spec,
                  _const_spec(dtab.shape),
                  _const_spec(mod.shape),
                  _const_spec(conv_w.shape),
                  _const_spec((1, CONV_CH)),
                  _const_spec((1, CONV_CH)),
                  _const_spec(w_out.shape)],
        out_specs=[tok(d), state_spec],
        out_shape=[jax.ShapeDtypeStruct((bsz, l, d), F32),
                   jax.ShapeDtypeStruct((bsz, HEADS, CHUNK, CHUNK), F32)],
        scratch_shapes=[pltpu.VMEM((HEADS, CHUNK, CHUNK), F32),
                        pltpu.VMEM((tm + 2 * HALO, CONV_CH), F32),
                        pltpu.VMEM((tm, RET_W + CONV_CH), BF16)],
        compiler_params=_params(("arbitrary", "arbitrary")),
        name="even_mix",
    )(x, q, k, v, gs, u, u, u, sprev, sb0, dtab, mod, conv_w, ln_w, ln_b, w_out)


def _odd_kernel(x_ref, xp_ref, xn_ref, mod_ref, nw_ref, wi_ref, pw_ref, ps_ref, lnw_ref, lnb_ref,
                sgw_ref, sgb_ref, wo_ref, o_ref, h_ref, pcx_ref, mix_ref, *, tm, seq_len):
    b = pl.program_id(0)
    i = pl.program_id(1)
    n_tiles = pl.num_programs(1)
    nw = nw_ref[...]
    shift = mod_ref[b * 6 + 0]
    scale = mod_ref[b * 6 + 1]

    h_ref[0:HALO] = _rms_mod(xp_ref[0], nw, shift, scale).astype(BF16)
    h_ref[HALO:HALO + tm] = _rms_mod(x_ref[0], nw, shift, scale).astype(BF16)
    h_ref[HALO + tm:2 * HALO + tm] = _rms_mod(xn_ref[0], nw, shift, scale).astype(BF16)

    pcx_ref[0:tm + 2 * HALO] = _dot(h_ref[...], wi_ref[:, 0:POOL_CH])
    pcx_ref[0:HALO] = jnp.where(i > 0, pcx_ref[0:HALO], 0.0)
    pcx_ref[HALO + tm:2 * HALO + tm] = jnp.where(i < n_tiles - 1, pcx_ref[HALO + tm:2 * HALO + tm], 0.0)
    pcx_ref[tm + 2 * HALO:tm + 2 * HALO + SUBLANES] = jnp.zeros((SUBLANES, POOL_CH), F32)

    sub = min(tm, ODD_ROWS)
    for r0 in range(0, tm, sub):
        rows = slice(r0, r0 + sub)
        pos = i * tm + r0 + lax.broadcasted_iota(jnp.int32, (sub, POOL_GC), 0)
        for gi, w in enumerate(POOL_WINDOWS):
            left = w // 2
            right = w - 1 - left
            sl = slice(gi * POOL_GC, (gi + 1) * POOL_GC)
            run = pcx_ref[r0:r0 + sub + 2 * HALO + SUBLANES, sl]
            span = 1
            while span < w:
                n_run = run.shape[0]
                run = (run + pltpu.roll(run, n_run - span, 0))[0:n_run - SUBLANES]
                span *= 2
            win = pltpu.roll(run, run.shape[0] - (HALO - left), 0)[0:sub]
            cnt = jnp.minimum(pos + right, seq_len - 1) - jnp.maximum(pos - left, 0) + 1
            m = win / cnt.astype(F32) - pcx_ref[HALO + r0:HALO + r0 + sub, sl]
            po = _dot(m.astype(BF16), pw_ref[gi]) * ps_ref[:, sl]
            mix_ref[rows, sl] = po.astype(BF16)

        pd = _dot(h_ref[HALO + r0:HALO + r0 + sub], wi_ref[:, POOL_CH:ODD_IN])
        z = 0.5 * pd * (1.0 + lax.erf(pd * (2.0 ** -0.5)))
        zu = z[:, 0:SG_CH]
        zv = _layer_norm(z[:, SG_CH:2 * SG_CH], lnw_ref[...], lnb_ref[...]).astype(BF16)
        for c in range(sub // CHUNK):
            crows = slice(c * CHUNK, (c + 1) * CHUNK)
            for g in range(SG_GROUPS):
                sl = slice(g * SG_GC, (g + 1) * SG_GC)
                s = _dot(sgw_ref[g], zv[crows, sl]) + sgb_ref[g]
                mix_ref[r0 + c * CHUNK:r0 + (c + 1) * CHUNK, POOL_CH + g * SG_GC:POOL_CH + (g + 1) * SG_GC] = (
                    zu[crows, sl] * s).astype(BF16)

        out = _dot(mix_ref[rows], wo_ref[...])
        o_ref[0, rows] = x_ref[0, rows] + mod_ref[b * 6 + 2] * out


def _odd_call(x, mod, nw, w_in, pool_w, pool_scale, ln_w, ln_b, sg_w, sg_b_full, w_out, tm):
    bsz, l, d = x.shape
    hpt = tm // HALO
    n_halo = l // HALO
    tok = pl.BlockSpec((1, tm, d), lambda b, i: (b, i, 0))
    return pl.pallas_call(
        functools.partial(_odd_kernel, tm=tm, seq_len=l),
        grid=(bsz, l // tm),
        in_specs=[tok,
                  pl.BlockSpec((1, HALO, d), lambda b, i: (b, jnp.maximum(i * hpt - 1, 0), 0)),
                  pl.BlockSpec((1, HALO, d), lambda b, i: (b, jnp.minimum((i + 1) * hpt, n_halo - 1), 0)),
                  _const_spec(mod.shape),
                  _const_spec((1, d)),
                  _const_spec(w_in.shape),
                  _const_spec(pool_w.shape),
                  _const_spec((1, POOL_CH)),
                  _const_spec((1, SG_CH)),
                  _const_spec((1, SG_CH)),
                  _const_spec(sg_w.shape),
                  _const_spec(sg_b_full.shape),
                  _const_spec(w_out.shape)],
        out_specs=tok,
        out_shape=jax.ShapeDtypeStruct((bsz, l, d), F32),
        scratch_shapes=[pltpu.VMEM((tm + 2 * HALO, d), BF16),
                        pltpu.VMEM((tm + 2 * HALO + SUBLANES, POOL_CH), F32),
                        pltpu.VMEM((tm, POOL_CH + SG_CH), BF16)],
        compiler_params=_params(("arbitrary", "arbitrary")),
        name="odd_mix",
    )(x, x, x, mod, nw, w_in, pool_w, pool_scale, ln_w, ln_b, sg_w, sg_b_full, w_out)


def _ffn_kernel(x_ref, mod_ref, nw_ref, wg_ref, wu_ref, wd_ref, fnw_ref, *rest, n_cast, n_ada, final_norm):
    cast_in, rest = rest[:n_cast], rest[n_cast:]
    ada_in, rest = rest[:3 * n_ada], rest[3 * n_ada:]
    o_ref, rest = rest[0], rest[1:]
    cast_out, rest = rest[:n_cast], rest[n_cast:]
    ada_out, (h_ref, act_ref) = rest[:n_ada], rest[n_ada:]
    _cast_bands(cast_in, cast_out)
    if n_ada:
        c_ref, aw_ref, ab_ref = ada_in
        mo_ref, = ada_out

        @pl.when(jnp.logical_and(pl.program_id(0) == 0, pl.program_id(1) == 0))
        def _():
            mo_ref[...] = jnp.broadcast_to(ab_ref[...], mo_ref.shape)

        mo_ref[...] += _dot(_silu(c_ref[...]).astype(BF16), aw_ref[...].astype(BF16))
    tm = x_ref.shape[1]
    subs = []
    for r0 in range(0, tm, min(tm, FFN_ROWS)):
        rows = pl.ds(r0, min(tm, FFN_ROWS))
        subs.append(_ffn_pieces(x_ref.at[0, rows], o_ref.at[0, rows], pl.program_id(0), mod_ref, nw_ref,
                                wg_ref, wu_ref, wd_ref, fnw_ref, h_ref.at[rows], act_ref.at[rows], final_norm))
    order = [subs[0][0]]
    for k, pieces in enumerate(subs):
        order.append(pieces[1])
        if k + 1 < len(subs):
            order.append(subs[k + 1][0])
        order += pieces[2:]
    for piece in order:
        piece()


def _ffn_call(x, mod, nw, wg, wu, wd, fnw, tm, final_norm, cast=(), ada=None):
    bsz, l, d = x.shape
    n_tiles = l // tm
    n_steps = bsz * n_tiles
    tok = pl.BlockSpec((1, tm, d), lambda b, i: (b, i, 0))
    specs = [_cast_band_specs(stack, layer, bsz, n_tiles) for stack, layer in cast]
    ada_ops, ada_in_specs, ada_out_specs, ada_out_shape = [], [], [], []
    if ada is not None:
        c_rows, ada_w, ada_b, layer = ada
        n_mod = ada_w.shape[2]
        band = d // n_steps
        assert d % n_steps == 0 and band % SUBLANES == 0
        c_bands = c_rows.reshape(MOD_ROWS, n_steps, band).transpose(1, 0, 2)
        ada_ops = [c_bands, ada_w, ada_b.reshape(ada_b.shape[0], 1, n_mod)]
        ada_in_specs = [pl.BlockSpec((None, MOD_ROWS, band), lambda b, i: (b * n_tiles + i, 0, 0)),
                        pl.BlockSpec((None, band, n_mod), lambda b, i: (layer, b * n_tiles + i, 0)),
                        pl.BlockSpec((None, 1, n_mod), lambda b, i: (layer, 0, 0))]
        ada_out_specs = [pl.BlockSpec((MOD_ROWS, n_mod), lambda b, i: (0, 0))]
        ada_out_shape = [jax.ShapeDtypeStruct((MOD_ROWS, n_mod), F32)]
    outs = pl.pallas_call(
        functools.partial(_ffn_kernel, n_cast=len(cast), n_ada=len(ada_out_specs), final_norm=final_norm),
        grid=(bsz, n_tiles),
        in_specs=[tok,
                  _const_spec(mod.shape),
                  _const_spec((1, d)),
                  _const_spec(wg.shape),
                  _const_spec(wu.shape),
                  _const_spec(wd.shape),
                  _const_spec((1, d))] + [s[0] for s in specs] + ada_in_specs,
        out_specs=[tok] + [s[1] for s in specs] + ada_out_specs,
        out_shape=[jax.ShapeDtypeStruct((bsz, l, d), F32)]
        + [jax.ShapeDtypeStruct(stack.shape[1:], BF16) for stack, _ in cast] + ada_out_shape,
        scratch_shapes=[pltpu.VMEM((tm, d), BF16), pltpu.VMEM((tm, D_FF), BF16)],
        compiler_params=_params(("arbitrary", "arbitrary")),
        name="ffn",
    )(x, mod, nw, wg, wu, wd, fnw, *[stack for stack, _ in cast], *ada_ops)
    n_cast = len(cast)
    return outs[0], tuple(outs[1:1 + n_cast]), (outs[1 + n_cast] if ada is not None else None)


def _rope_tables(p_seq, p_row, p_col):
    parts = []
    for p, n in zip((p_seq, p_row, p_col), ROPE_PAIRS):
        freq = ROPE_BASE ** (-jnp.arange(n, dtype=F32) / n)
        parts.append(p[:, None] * freq[None, :])
    ang = jnp.concatenate(parts, axis=-1)
    cos, sin = jnp.cos(ang), jnp.sin(ang)
    return jnp.concatenate([cos, cos], axis=-1), jnp.concatenate([-sin, sin], axis=-1)


def _tile(l):
    return min(l, 1024)


def kernel(x, c, ctx, c_ctx, ada_w, ada_b, norm_w, even_w_in, even_w_out, ret_decay_logit, conv_dw_w, conv_ln_w, conv_ln_b, odd_w_in, odd_w_out, pool_w, pool_scale, sg_ln_w, sg_ln_b, sg_w, sg_b, ffn_w_gate, ffn_w_up, ffn_w_down, final_norm_w):
    bsz, l, d = x.shape
    lc = ctx.shape[1]
    assert d == D_MODEL and l % _tile(l) == 0 and l % CHUNK == 0 and lc % CHUNK == 0 and bsz + 1 <= MOD_ROWS
    rows = l // GRID_W
    grid_r = jnp.broadcast_to(jnp.arange(rows, dtype=F32)[:, None], (rows, GRID_W)).reshape(-1)
    grid_c = jnp.broadcast_to(jnp.arange(GRID_W, dtype=F32)[None, :], (rows, GRID_W)).reshape(-1)
    cos_x, sin_x = _rope_tables(jnp.full((l,), lc, F32), grid_r, grid_c)
    zeros_c = jnp.zeros((lc,), F32)
    cos_c, sin_c = _rope_tables(jnp.arange(lc, dtype=F32), zeros_c, zeros_c)

    cvec = jnp.concatenate([c, c_ctx[None, :], jnp.zeros((MOD_ROWS - bsz - 1, d), F32)], axis=0)
    mod_rows = _mod_call(cvec, ada_w, ada_b, 0)
    dtabs = _decay_call(ret_decay_logit)
    zero_state = jnp.zeros((bsz, HEADS, CHUNK, CHUNK), F32)
    fnw = final_norm_w.reshape(1, d)
    tx, tc = _tile(l), _tile(lc)

    def layer_matrices(i):
        mix_in, mix_out = (even_w_in, even_w_out) if i % 2 == 0 else (odd_w_in, odd_w_out)
        return (mix_in, i // 2), (mix_out, i // 2), (ffn_w_gate, i), (ffn_w_up, i), (ffn_w_down, i)

    first = layer_matrices(0)
    bf16_mats = (first[0][0][first[0][1]].astype(BF16),) + (None,) * (len(first) - 1)
    for i in range(DEPTH):
        j = i // 2
        even = i % 2 == 0
        ctx_after = any(m % 2 == 0 for m in range(i + 1, DEPTH))
        mod_x = mod_rows[:bsz].reshape(bsz * 6, 1, d)
        mod_c = jnp.broadcast_to(mod_rows[bsz].reshape(1, 6, 1, d), (bsz, 6, 1, d)).reshape(bsz * 6, 1, d)
        nw1 = norm_w[i, 0].reshape(1, d)
        nw2 = norm_w[i, 1].reshape(1, d)
        w_in, w_out, wg, wu, wd = bf16_mats
        if even:
            conv_w = conv_dw_w[j]
            lnw = conv_ln_w[j].reshape(1, CONV_CH)
            lnb = conv_ln_b[j].reshape(1, CONV_CH)
            dtab = dtabs[j]
            if ctx_after:
                *mix_c, s_f, _ = _even_in_call(ctx, mod_c, nw1, w_in, cos_c, sin_c, zero_state, dtab, tc)
            else:
                s_f, s_b = _ctx_state_call(ctx, mod_c, nw1, w_in[:, RET_W:3 * RET_W], cos_c, sin_c, dtab)
            pending = layer_matrices(i)[1:] if w_out is None else ()
            *mix_x, _, cast_now = _even_in_call(x, mod_x, nw1, w_in, cos_x, sin_x, s_f, dtab, tx, pending)
            if pending:
                w_out, wg, wu, wd = cast_now
            if ctx_after:
                ctx, s_b = _even_mix_call(ctx, *mix_c, zero_state, dtab, mod_c, conv_w, lnw, lnb, w_out, tc)
            x, _ = _even_mix_call(x, *mix_x, s_b, dtab, mod_x, conv_w, lnw, lnb, w_out, tx)
        else:
            args = (nw1, w_in, pool_w[j].astype(BF16), pool_scale[j].reshape(1, POOL_CH),
                    sg_ln_w[j].reshape(1, SG_CH), sg_ln_b[j].reshape(1, SG_CH), sg_w[j].astype(BF16),
                    jnp.broadcast_to(sg_b[j][:, :, None], (SG_GROUPS, CHUNK, SG_GC)), w_out)
            if ctx_after:
                ctx = _odd_call(ctx, mod_c, *args, tc)
            x = _odd_call(x, mod_x, *args, tx)
        if ctx_after:
            ctx, _, _ = _ffn_call(ctx, mod_c, nw2, wg, wu, wd, fnw, tc, False)
        later = layer_matrices(i + 1) if i + 1 < DEPTH else ()
        ada_next = (cvec, ada_w, ada_b, i + 1) if i + 1 < DEPTH else None
        x, bf16_mats, mod_rows = _ffn_call(x, mod_x, nw2, wg, wu, wd, fnw, min(l, 2 * FFN_ROWS), i == DEPTH - 1,
                                           later, ada_next)
    return x
```

```python
import functools

import jax
import jax.numpy as jnp
from jax import lax
from jax.experimental import pallas as pl
from jax.experimental.pallas import tpu as pltpu

F32 = jnp.float32
BF16 = jnp.bfloat16

D_MODEL = 1024
DEPTH = 4
GRID_W = 64
EPS = 1e-6
HEADS = 4
HEAD_DIM = 128
RET_W = HEADS * HEAD_DIM
CHUNK = 128
ROPE_BASE = 10000.0
ROPE_PAIRS = (HEAD_DIM // 8, 3 * HEAD_DIM // 16, 3 * HEAD_DIM // 16)
CONV_CH = 512
CONV_K = 31
CONV_PAD = CONV_K // 2
EVEN_IN = 4 * RET_W + 2 * CONV_CH
POOL_CH = 512
POOL_WINDOWS = (2, 4, 8, 16)
POOL_GC = 128
SG_CH = 512
SG_GROUPS = 4
SG_GC = 128
ODD_IN = POOL_CH + 2 * SG_CH
D_FF = 2816
SUBLANES = 8
BF16_SUBLANES = 16
MOD_COLS = 1024
LANES = 128
FF_BLOCK = 256
OUT_BLOCK = 256
CONV_ROWS = 256
FFN_ROWS = 512
ODD_ROWS = 512
HALO = 16
N_DECAY_TABLES = 7
MOD_ROWS = 8
VMEM_LIMIT = 56 * 1024 * 1024


def _dot(a, b):
    return jnp.dot(a, b, preferred_element_type=F32)


def _dot_nt(a, b):
    return lax.dot_general(a, b, (((1,), (1,)), ((), ())), preferred_element_type=F32)


def _dot_tn(a, b):
    return lax.dot_general(a, b, (((0,), (0,)), ((), ())), preferred_element_type=F32)


def _silu(x):
    return x * jax.nn.sigmoid(x)


def _rms_mod(x32, nw, shift, scale):
    y = x32 * lax.rsqrt(jnp.mean(x32 * x32, axis=-1, keepdims=True) + EPS) * nw
    return y * (1.0 + scale) + shift


def _layer_norm(x32, w, b):
    mu = jnp.mean(x32, axis=-1, keepdims=True)
    xc = x32 - mu
    return xc * lax.rsqrt(jnp.mean(xc * xc, axis=-1, keepdims=True) + EPS) * w + b


def _const_spec(shape):
    return pl.BlockSpec(shape, lambda *_: (0,) * len(shape), pipeline_mode=pl.Buffered(1))


def _params(sem):
    return pltpu.CompilerParams(dimension_semantics=sem, vmem_limit_bytes=VMEM_LIMIT)


def _cast_band_specs(stack, layer, bsz, n_tiles):
    _, rows, cols = stack.shape
    n_steps = bsz * n_tiles
    hold = next(h for h in range(1, n_steps + 1)
                if n_steps % h == 0 and rows % (n_steps // h) == 0 and (rows * h // n_steps) % BF16_SUBLANES == 0)
    band = rows * hold // n_steps
    return (pl.BlockSpec((None, band, cols), lambda b, i: (layer, (b * n_tiles + i) // hold, 0)),
            pl.BlockSpec((band, cols), lambda b, i: ((b * n_tiles + i) // hold, 0)))


def _cast_bands(srcs, dsts):
    for src, dst in zip(srcs, dsts):
        dst[...] = src[...].astype(BF16)


def _mod_kernel(c_ref, w_ref, b_ref, o_ref):
    o_ref[...] = _dot(_silu(c_ref[...]).astype(BF16), w_ref[...].astype(BF16)) + b_ref[...]


def _mod_call(c_rows, ada_w, ada_b, layer):
    depth, d, n = ada_w.shape
    tn = MOD_COLS
    return pl.pallas_call(
        _mod_kernel,
        grid=(n // tn,),
        in_specs=[_const_spec((MOD_ROWS, d)),
                  pl.BlockSpec((None, d, tn), lambda j: (layer, 0, j)),
                  pl.BlockSpec((None, 1, tn), lambda j: (layer, 0, j))],
        out_specs=pl.BlockSpec((MOD_ROWS, tn), lambda j: (0, j)),
        out_shape=jax.ShapeDtypeStruct((MOD_ROWS, n), F32),
        compiler_params=_params(("arbitrary",)),
        name="adaln_mod",
    )(c_rows, ada_w, ada_b.reshape(depth, 1, n))


def _log_sigmoid(x):
    return -(jnp.maximum(-x, 0.0) + jnp.log1p(jnp.exp(-jnp.abs(x))))


def _decay_kernel(logit_ref, o_ref):
    j = pl.program_id(0)
    row = lax.broadcasted_iota(jnp.int32, (CHUNK, CHUNK), 0).astype(F32)
    col = lax.broadcasted_iota(jnp.int32, (CHUNK, CHUNK), 1).astype(F32)
    diff = row - col
    lower = diff >= 0.0
    for hd in range(HEADS):
        lgf = _log_sigmoid(jnp.full((CHUNK, CHUNK), logit_ref[(j * 2 + 0) * HEADS + hd], F32))
        lgb = _log_sigmoid(jnp.full((CHUNK, CHUNK), logit_ref[(j * 2 + 1) * HEADS + hd], F32))
        o_ref[0, 0, hd] = jnp.where(lower, jnp.exp(lgf * jnp.where(lower, diff, 0.0)),
                                    jnp.exp(lgb * jnp.where(lower, 0.0, -diff - 1.0)))
        o_ref[0, 1, hd] = jnp.exp(lgf * (CHUNK - 1.0 - row))
        o_ref[0, 2, hd] = jnp.exp(lgb * row)
        o_ref[0, 3, hd] = jnp.exp(lgf * (row + 1.0))
        o_ref[0, 4, hd] = jnp.exp(lgb * (CHUNK - 1.0 - row))
        o_ref[0, 5, hd] = jnp.exp(lgf * float(CHUNK))
        o_ref[0, 6, hd] = jnp.exp(lgb * float(CHUNK))


def _decay_call(ret_decay_logit):
    n_even = ret_decay_logit.shape[0]
    return pl.pallas_call(
        _decay_kernel,
        grid=(n_even,),
        in_specs=[pl.BlockSpec(memory_space=pltpu.SMEM)],
        out_specs=pl.BlockSpec((1, N_DECAY_TABLES, HEADS, CHUNK, CHUNK), lambda j: (j, 0, 0, 0, 0)),
        out_shape=jax.ShapeDtypeStruct((n_even, N_DECAY_TABLES, HEADS, CHUNK, CHUNK), F32),
        compiler_params=_params(("arbitrary",)),
        name="ret_decay",
    )(ret_decay_logit.reshape(-1).astype(F32))


def _even_in_kernel(x_ref, mod_ref, nw_ref, w_ref, cos_ref, sin_ref, s0_ref, dt_ref, *rest, n_cast, tm):
    cast_in = rest[:n_cast]
    q_ref, k_ref, v_ref, gs_ref, u_ref, sprev_ref, sfin_ref = rest[n_cast:n_cast + 7]
    cast_out = rest[n_cast + 7:2 * n_cast + 7]
    state_ref = rest[2 * n_cast + 7]
    b = pl.program_id(0)
    i = pl.program_id(1)
    _cast_bands(cast_in, cast_out)

    @pl.when(i == 0)
    def _():
        state_ref[...] = s0_ref[0]

    h = _rms_mod(x_ref[0], nw_ref[...], mod_ref[b * 6 + 0], mod_ref[b * 6 + 1]).astype(BF16)
    cosf = cos_ref[...]
    sinf = sin_ref[...]

    def rope(t):
        return t * cosf + pltpu.roll(t, HEAD_DIM // 2, 1) * sinf

    pq = _dot(h, w_ref[:, 0:RET_W])
    pk = _dot(h, w_ref[:, RET_W:2 * RET_W])
    pv = _dot(h, w_ref[:, 2 * RET_W:3 * RET_W])
    scale = HEAD_DIM ** -0.5
    for hd in range(HEADS):
        sl = slice(hd * HEAD_DIM, (hd + 1) * HEAD_DIM)
        q_ref[0, :, sl] = (rope(pq[:, sl]) * scale).astype(BF16)
        kr = rope(pk[:, sl]).astype(BF16)
        k_ref[0, :, sl] = kr
        vh = pv[:, sl]
        v_ref[0, :, sl] = vh.astype(BF16)
        zeta = dt_ref[1, hd]
        cdec = dt_ref[5, hd]
        s_cur = state_ref[hd]
        for c in range(tm // CHUNK):
            rows = slice(c * CHUNK, (c + 1) * CHUNK)
            sprev_ref[0, c, hd] = s_cur.astype(BF16)
            wv = (vh[rows] * zeta).astype(BF16)
            s_cur = s_cur * cdec + _dot_tn(kr[rows], wv)
        state_ref[hd] = s_cur

    pg = _dot(h, w_ref[:, 3 * RET_W:4 * RET_W])
    gs_ref[0] = _silu(pg).astype(BF16)
    pa = _dot(h, w_ref[:, 4 * RET_W:4 * RET_W + CONV_CH])
    pb = _dot(h, w_ref[:, 4 * RET_W + CONV_CH:EVEN_IN])
    u_ref[0] = (pa * jax.nn.sigmoid(pb)).astype(BF16)

    @pl.when(i == pl.num_programs(1) - 1)
    def _():
        sfin_ref[0] = state_ref[...]


def _even_in_call(x, mod, nw, w_in, cosf, sinf, s0, dtab, tm, cast=()):
    bsz, l, d = x.shape
    n_chunks = l // CHUNK
    cpt = tm // CHUNK
    tok = lambda w: pl.BlockSpec((1, tm, w), lambda b, i: (b, i, 0))
    state_spec = pl.BlockSpec((1, HEADS, CHUNK, CHUNK), lambda b, i: (b, 0, 0, 0))
    specs = [_cast_band_specs(stack, layer, bsz, l // tm) for stack, layer in cast]
    outs = pl.pallas_call(
        functools.partial(_even_in_kernel, n_cast=len(cast), tm=tm),
        grid=(bsz, l // tm),
        in_specs=[tok(d),
                  _const_spec(mod.shape),
                  _const_spec((1, d)),
                  _const_spec(w_in.shape),
                  pl.BlockSpec((tm, HEAD_DIM), lambda b, i: (i, 0)),
                  pl.BlockSpec((tm, HEAD_DIM), lambda b, i: (i, 0)),
                  state_spec,
                  _const_spec(dtab.shape)] + [s[0] for s in specs],
        out_specs=[tok(RET_W), tok(RET_W), tok(RET_W), tok(RET_W), tok(CONV_CH),
                   pl.BlockSpec((1, cpt, HEADS, CHUNK, CHUNK), lambda b, i: (b, i, 0, 0, 0)),
                   state_spec] + [s[1] for s in specs],
        out_shape=[jax.ShapeDtypeStruct((bsz, l, RET_W), BF16),
                   jax.ShapeDtypeStruct((bsz, l, RET_W), BF16),
                   jax.ShapeDtypeStruct((bsz, l, RET_W), BF16),
                   jax.ShapeDtypeStruct((bsz, l, RET_W), BF16),
                   jax.ShapeDtypeStruct((bsz, l, CONV_CH), BF16),
                   jax.ShapeDtypeStruct((bsz, n_chunks, HEADS, CHUNK, CHUNK), BF16),
                   jax.ShapeDtypeStruct((bsz, HEADS, CHUNK, CHUNK), F32)]
        + [jax.ShapeDtypeStruct(stack.shape[1:], BF16) for stack, _ in cast],
        scratch_shapes=[pltpu.VMEM((HEADS, CHUNK, CHUNK), F32)],
        compiler_params=_params(("arbitrary", "arbitrary")),
        name="even_in",
    )(x, mod, nw, w_in, cosf, sinf, s0, dtab, *[stack for stack, _ in cast])
    return (*outs[:7], tuple(outs[7:]))


def _ctx_state_kernel(x_ref, mod_ref, nw_ref, w_ref, cos_ref, sin_ref, dt_ref, sf_ref, sb_ref):
    b = pl.program_id(0)
    h = _rms_mod(x_ref[0], nw_ref[...], mod_ref[b * 6 + 0], mod_ref[b * 6 + 1]).astype(BF16)
    cosf = cos_ref[...]
    sinf = sin_ref[...]
    pk = _dot(h, w_ref[:, 0:RET_W])
    pv = _dot(h, w_ref[:, RET_W:2 * RET_W])
    n_chunks = x_ref.shape[1] // CHUNK
    for hd in range(HEADS):
        sl = slice(hd * HEAD_DIM, (hd + 1) * HEAD_DIM)
        t = pk[:, sl]
        kr = (t * cosf + pltpu.roll(t, HEAD_DIM // 2, 1) * sinf).astype(BF16)
        vh = pv[:, sl]
        s_f = jnp.zeros((CHUNK, CHUNK), F32)
        s_b = jnp.zeros((CHUNK, CHUNK), F32)
        for c in range(n_chunks):
            rows = slice(c * CHUNK, (c + 1) * CHUNK)
            s_f = s_f * dt_ref[5, hd] + _dot_tn(kr[rows], (vh[rows] * dt_ref[1, hd]).astype(BF16))
        for c in reversed(range(n_chunks)):
            rows = slice(c * CHUNK, (c + 1) * CHUNK)
            s_b = s_b * dt_ref[6, hd] + _dot_tn(kr[rows], (vh[rows] * dt_ref[2, hd]).astype(BF16))
        sf_ref[0, hd] = s_f
        sb_ref[0, hd] = s_b


def _ctx_state_call(ctx, mod, nw, w_kv, cosf, sinf, dtab):
    bsz, lc, d = ctx.shape
    state_spec = pl.BlockSpec((1, HEADS, CHUNK, CHUNK), lambda b: (b, 0, 0, 0))
    return pl.pallas_call(
        _ctx_state_kernel,
        grid=(bsz,),
        in_specs=[pl.BlockSpec((1, lc, d), lambda b: (b, 0, 0)),
                  _const_spec(mod.shape),
                  _const_spec((1, d)),
                  _const_spec(w_kv.shape),
                  _const_spec(cosf.shape),
                  _const_spec(sinf.shape),
                  _const_spec(dtab.shape)],
        out_specs=[state_spec, state_spec],
        out_shape=[jax.ShapeDtypeStruct((bsz, HEADS, CHUNK, CHUNK), F32)] * 2,
        compiler_params=_params(("arbitrary",)),
        name="ctx_state",
    )(ctx, mod, nw, w_kv, cosf, sinf, dtab)


def _even_mix_pieces(b, i, n_tiles, x_ref, q_ref, k_ref, v_ref, gs_ref, u_ref, up_ref, un_ref, sprev_ref,
                     dt_ref, mod_ref, cw_ref, lnw_ref, lnb_ref, wo_ref, state_ref, uext_ref, mix_ref, xo_ref, tm):
    def retention(c, hd):
        rows = slice(c * CHUNK, (c + 1) * CHUNK)
        sl = slice(hd * HEAD_DIM, (hd + 1) * HEAD_DIM)
        qc = q_ref[0, rows, sl]
        kc = k_ref[0, rows, sl]
        vc = v_ref[0, rows, sl]
        s_b = state_ref[hd]
        p = (_dot_nt(qc, kc) * dt_ref[0, hd]).astype(BF16)
        y = _dot(p, vc)
        y = y + _dot(qc, sprev_ref[0, c, hd]) * dt_ref[3, hd]
        y = y + _dot(qc, s_b.astype(BF16)) * dt_ref[4, hd]
        y = y * lax.rsqrt(jnp.mean(y * y, axis=-1, keepdims=True) + EPS)
        mix_ref[rows, sl] = gs_ref[0, rows, sl] * y.astype(BF16)
        wv = (vc.astype(F32) * dt_ref[2, hd]).astype(BF16)
        state_ref[hd] = s_b * dt_ref[6, hd] + _dot_tn(kc, wv)

    def halo():
        uext_ref[0:HALO] = jnp.where(i < n_tiles - 1, up_ref[0].astype(F32), 0.0)
        uext_ref[HALO:HALO + tm] = u_ref[0].astype(F32)
        uext_ref[HALO + tm:2 * HALO + tm] = jnp.where(i > 0, un_ref[0].astype(F32), 0.0)

    def conv(r):
        blocks = []
        for cb in range(CONV_CH // LANES):
            cols = slice(cb * LANES, (cb + 1) * LANES)
            acc = None
            for sub in range(SUBLANES):
                part = None
                for t in range(sub, CONV_K, SUBLANES):
                    start = r * CONV_ROWS + t - sub
                    term = uext_ref[start:start + CONV_ROWS + SUBLANES, cols] * cw_ref[t:t + 1, cols]
                    part = term if part is None else part + term
                shifted = pltpu.roll(part, CONV_ROWS + SUBLANES - (sub + 1), 0)[0:CONV_ROWS]
                acc = shifted if acc is None else acc + shifted
            blocks.append(acc)
        cv = _silu(_layer_norm(jnp.concatenate(blocks, axis=-1), lnw_ref[...], lnb_ref[...]))
        mix_ref[r * CONV_ROWS:(r + 1) * CONV_ROWS, RET_W:RET_W + CONV_CH] = cv.astype(BF16)

    def out_proj(n):
        cols = slice(n * OUT_BLOCK, (n + 1) * OUT_BLOCK)
        out = _dot(mix_ref[...], wo_ref[:, cols])
        xo_ref[:, cols] = x_ref[0, :, cols] + mod_ref[b * 6 + 2][:, cols] * out

    pieces = [halo]
    convs = [functools.partial(conv, r) for r in range(tm // CONV_ROWS)]
    rets = [functools.partial(retention, c, hd) for c in reversed(range(tm // CHUNK)) for hd in range(HEADS)]
    per = len(rets) // len(convs)
    for r, cv in enumerate(convs):
        pieces += rets[r * per:(r + 1) * per] + [cv]
    pieces += rets[len(convs) * per:]
    pieces += [functools.partial(out_proj, n) for n in range(D_MODEL // OUT_BLOCK)]
    return pieces


def _ffn_norm(xs_ref, b, mod_ref, nw_ref):
    return _rms_mod(xs_ref[...], nw_ref[...], mod_ref[b * 6 + 3], mod_ref[b * 6 + 4]).astype(BF16)


def _even_mix_kernel(x_ref, q_ref, k_ref, v_ref, gs_ref, u_ref, up_ref, un_ref, sprev_ref, sb0_ref,
                     dt_ref, mod_ref, cw_ref, lnw_ref, lnb_ref, wo_ref,
                     o_ref, sbfin_ref, state_ref, uext_ref, mix_ref, *, tm):
    b = pl.program_id(0)
    i = pl.program_id(1)
    n_tiles = pl.num_programs(1)

    @pl.when(i == 0)
    def _():
        state_ref[...] = sb0_ref[0]

    for piece in _even_mix_pieces(b, i, n_tiles, x_ref, q_ref, k_ref, v_ref, gs_ref, u_ref, up_ref, un_ref,
                                  sprev_ref, dt_ref, mod_ref, cw_ref, lnw_ref, lnb_ref, wo_ref,
                                  state_ref, uext_ref, mix_ref, o_ref.at[0], tm):
        piece()

    @pl.when(i == n_tiles - 1)
    def _():
        sbfin_ref[0] = state_ref[...]


def _even_mix_call(x, q, k, v, gs, u, sprev, sb0, dtab, mod, conv_w, ln_w, ln_b, w_out, tm):
    bsz, l, d = x.shape
    n_tiles = l // tm
    cpt = tm // CHUNK
    hpt = tm // HALO
    n_halo = l // HALO
    rev = lambda i: n_tiles - 1 - i
    tok = lambda w: pl.BlockSpec((1, tm, w), lambda b, i: (b, rev(i), 0))
    state_spec = pl.BlockSpec((1, HEADS, CHUNK, CHUNK), lambda b, i: (b, 0, 0, 0))
    return pl.pallas_call(
        functools.partial(_even_mix_kernel, tm=tm),
        grid=(bsz, n_tiles),
        in_specs=[tok(d), tok(RET_W), tok(RET_W), tok(RET_W), tok(RET_W), tok(CONV_CH),
                  pl.BlockSpec((1, HALO, CONV_CH), lambda b, i: (b, jnp.maximum(rev(i) * hpt - 1, 0), 0)),
                  pl.BlockSpec((1, HALO, CONV_CH),
                               lambda b, i: (b, jnp.minimum((rev(i) + 1) * hpt, n_halo - 1), 0)),
                  pl.BlockSpec((1, cpt, HEADS, CHUNK, CHUNK), lambda b, i: (b, rev(i), 0, 0, 0)),
                  state_spec,
                  _const_spec(dtab.shape),
                  _const_spec(mod.shape),
                  _const_spec(conv_w.shape),
                  _const_spec((1, CONV_CH)),
                  _const_spec((1, CONV_CH)),
                  _const_spec(w_out.shape)],
        out_specs=[tok(d), state_spec],
        out_shape=[jax.ShapeDtypeStruct((bsz, l, d), F32),
                   jax.ShapeDtypeStruct((bsz, HEADS, CHUNK, CHUNK), F32)],
        scratch_shapes=[pltpu.VMEM((HEADS, CHUNK, CHUNK), F32),
                        pltpu.VMEM((tm + 2 * HALO, CONV_CH), F32),
                        pltpu.VMEM((tm, RET_W + CONV_CH), BF16)],
        compiler_params=_params(("arbitrary", "arbitrary")),
        name="even_mix",
    )(x, q, k, v, gs, u, u, u, sprev, sb0, dtab, mod, conv_w, ln_w, ln_b, w_out)


def _odd_kernel(x_ref, xp_ref, xn_ref, mod_ref, nw_ref, wi_ref, pw_ref, ps_ref, lnw_ref, lnb_ref,
                sgw_ref, sgb_ref, wo_ref, o_ref, h_ref, pcx_ref, mix_ref, *, tm, seq_len):
    b = pl.program_id(0)
    i = pl.program_id(1)
    n_tiles = pl.num_programs(1)
    nw = nw_ref[...]
    shift = mod_ref[b * 6 + 0]
    scale = mod_ref[b * 6 + 1]

    h_ref[0:HALO] = _rms_mod(xp_ref[0], nw, shift, scale).astype(BF16)
    h_ref[HALO:HALO + tm] = _rms_mod(x_ref[0], nw, shift, scale).astype(BF16)
    h_ref[HALO + tm:2 * HALO + tm] = _rms_mod(xn_ref[0], nw, shift, scale).astype(BF16)

    pcx_ref[0:tm + 2 * HALO] = _dot(h_ref[...], wi_ref[:, 0:POOL_CH])
    pcx_ref[0:HALO] = jnp.where(i > 0, pcx_ref[0:HALO], 0.0)
    pcx_ref[HALO + tm:2 * HALO + tm] = jnp.where(i < n_tiles - 1, pcx_ref[HALO + tm:2 * HALO + tm], 0.0)
    pcx_ref[tm + 2 * HALO:tm + 2 * HALO + SUBLANES] = jnp.zeros((SUBLANES, POOL_CH), F32)

    sub = min(tm, ODD_ROWS)
    for r0 in range(0, tm, sub):
        rows = slice(r0, r0 + sub)
        pos = i * tm + r0 + lax.broadcasted_iota(jnp.int32, (sub, POOL_GC), 0)
        for gi, w in enumerate(POOL_WINDOWS):
            left = w // 2
            right = w - 1 - left
            sl = slice(gi * POOL_GC, (gi + 1) * POOL_GC)
            run = pcx_ref[r0:r0 + sub + 2 * HALO + SUBLANES, sl]
            span = 1
            while span < w:
                n_run = run.shape[0]
                run = (run + pltpu.roll(run, n_run - span, 0))[0:n_run - SUBLANES]
                span *= 2
            win = pltpu.roll(run, run.shape[0] - (HALO - left), 0)[0:sub]
            cnt = jnp.minimum(pos + right, seq_len - 1) - jnp.maximum(pos - left, 0) + 1
            m = win / cnt.astype(F32) - pcx_ref[HALO + r0:HALO + r0 + sub, sl]
            po = _dot(m.astype(BF16), pw_ref[gi]) * ps_ref[:, sl]
            mix_ref[rows, sl] = po.astype(BF16)

        pd = _dot(h_ref[HALO + r0:HALO + r0 + sub], wi_ref[:, POOL_CH:ODD_IN])
        z = 0.5 * pd * (1.0 + lax.erf(pd * (2.0 ** -0.5)))
        zu = z[:, 0:SG_CH]
        zv = _layer_norm(z[:, SG_CH:2 * SG_CH], lnw_ref[...], lnb_ref[...]).astype(BF16)
        for c in range(sub // CHUNK):
            crows = slice(c * CHUNK, (c + 1) * CHUNK)
            for g in range(SG_GROUPS):
                sl = slice(g * SG_GC, (g + 1) * SG_GC)
                s = _dot(sgw_ref[g], zv[crows, sl]) + sgb_ref[g]
                mix_ref[r0 + c * CHUNK:r0 + (c + 1) * CHUNK, POOL_CH + g * SG_GC:POOL_CH + (g + 1) * SG_GC] = (
                    zu[crows, sl] * s).astype(BF16)

        out = _dot(mix_ref[rows], wo_ref[...])
        o_ref[0, rows] = x_ref[0, rows] + mod_ref[b * 6 + 2] * out


def _odd_call(x, mod, nw, w_in, pool_w, pool_scale, ln_w, ln_b, sg_w, sg_b_full, w_out, tm):
    bsz, l, d = x.shape
    hpt = tm // HALO
    n_halo = l // HALO
    tok = pl.BlockSpec((1, tm, d), lambda b, i: (b, i, 0))
    return pl.pallas_call(
        functools.partial(_odd_kernel, tm=tm, seq_len=l),
        grid=(bsz, l // tm),
        in_specs=[tok,
                  pl.BlockSpec((1, HALO, d), lambda b, i: (b, jnp.maximum(i * hpt - 1, 0), 0)),
                  pl.BlockSpec((1, HALO, d), lambda b, i: (b, jnp.minimum((i + 1) * hpt, n_halo - 1), 0)),
                  _const_spec(mod.shape),
                  _const_spec((1, d)),
                  _const_spec(w_in.shape),
                  _const_spec(pool_w.shape),
                  _const_spec((1, POOL_CH)),
                  _const_spec((1, SG_CH)),
                  _const_spec((1, SG_CH)),
                  _const_spec(sg_w.shape),
                  _const_spec(sg_b_full.shape),
                  _const_spec(w_out.shape)],
        out_specs=tok,
        out_shape=jax.ShapeDtypeStruct((bsz, l, d), F32),
        scratch_shapes=[pltpu.VMEM((tm + 2 * HALO, d), BF16),
                        pltpu.VMEM((tm + 2 * HALO + SUBLANES, POOL_CH), F32),
                        pltpu.VMEM((tm, POOL_CH + SG_CH), BF16)],
        compiler_params=_params(("arbitrary", "arbitrary")),
        name="odd_mix",
    )(x, x, x, mod, nw, w_in, pool_w, pool_scale, ln_w, ln_b, sg_w, sg_b_full, w_out)


def _ffn_kernel(x_ref, xn_ref, mod_ref, nw_ref, wg_ref, wu_ref, wd_ref, fnw_ref, *rest, n_cast, n_ada, final_norm):
    cast_in, rest = rest[:n_cast], rest[n_cast:]
    ada_in, rest = rest[:3 * n_ada], rest[3 * n_ada:]
    o_ref, rest = rest[0], rest[1:]
    cast_out, rest = rest[:n_cast], rest[n_cast:]
    ada_out, (h_ref, hn_ref, act_ref) = rest[:n_ada], rest[n_ada:]
    b = pl.program_id(0)
    n_tiles = pl.num_programs(1)
    step = b * n_tiles + pl.program_id(1)
    next_b = jnp.minimum(step + 1, pl.num_programs(0) * n_tiles - 1) // n_tiles

    @pl.when(step == 0)
    def _():
        hn_ref[...] = _ffn_norm(x_ref.at[0], b, mod_ref, nw_ref)

    h_ref[...] = hn_ref[...]
    _cast_bands(cast_in, cast_out)
    if n_ada:
        c_ref, aw_ref, ab_ref = ada_in
        mo_ref, = ada_out

        @pl.when(jnp.logical_and(pl.program_id(0) == 0, pl.program_id(1) == 0))
        def _():
            mo_ref[...] = jnp.broadcast_to(ab_ref[...], mo_ref.shape)

        mo_ref[...] += _dot(_silu(c_ref[...]).astype(BF16), aw_ref[...].astype(BF16))
    h = h_ref[...]
    for j in range(D_FF // FF_BLOCK):
        cols = slice(j * FF_BLOCK, (j + 1) * FF_BLOCK)
        act_ref[:, cols] = (_silu(_dot(h, wg_ref[:, cols])) * _dot(h, wu_ref[:, cols])).astype(BF16)
    h_next = _ffn_norm(xn_ref.at[0], next_b, mod_ref, nw_ref)
    hn_ref[...] = h_next
    gate = mod_ref[b * 6 + 5] + jnp.max(h_next, axis=0, keepdims=True).astype(F32) * 0.0
    act = act_ref[...]
    for n in range(D_MODEL // OUT_BLOCK):
        cols = slice(n * OUT_BLOCK, (n + 1) * OUT_BLOCK)
        o_ref[0, :, cols] = x_ref[0, :, cols] + gate[:, cols] * _dot(act, wd_ref[:, cols])
    if final_norm:
        y = o_ref[0]
        o_ref[0] = y * lax.rsqrt(jnp.mean(y * y, axis=-1, keepdims=True) + EPS) * fnw_ref[...]


def _ffn_call(x, mod, nw, wg, wu, wd, fnw, tm, final_norm, cast=(), ada=None):
    bsz, l, d = x.shape
    n_tiles = l // tm
    n_steps = bsz * n_tiles
    tok = pl.BlockSpec((1, tm, d), lambda b, i: (b, i, 0))

    def next_tile(b, i):
        nxt = jnp.minimum(b * n_tiles + i + 1, n_steps - 1)
        return nxt // n_tiles, nxt % n_tiles, 0

    specs = [_cast_band_specs(stack, layer, bsz, n_tiles) for stack, layer in cast]
    ada_ops, ada_in_specs, ada_out_specs, ada_out_shape = [], [], [], []
    if ada is not None:
        c_rows, ada_w, ada_b, layer = ada
        n_mod = ada_w.shape[2]
        band = d // n_steps
        assert d % n_steps == 0 and band % SUBLANES == 0
        c_bands = c_rows.reshape(MOD_ROWS, n_steps, band).transpose(1, 0, 2)
        ada_ops = [c_bands, ada_w, ada_b.reshape(ada_b.shape[0], 1, n_mod)]
        ada_in_specs = [pl.BlockSpec((None, MOD_ROWS, band), lambda b, i: (b * n_tiles + i, 0, 0)),
                        pl.BlockSpec((None, band, n_mod), lambda b, i: (layer, b * n_tiles + i, 0)),
                        pl.BlockSpec((None, 1, n_mod), lambda b, i: (layer, 0, 0))]
        ada_out_specs = [pl.BlockSpec((MOD_ROWS, n_mod), lambda b, i: (0, 0))]
        ada_out_shape = [jax.ShapeDtypeStruct((MOD_ROWS, n_mod), F32)]
    outs = pl.pallas_call(
        functools.partial(_ffn_kernel, n_cast=len(cast), n_ada=len(ada_out_specs), final_norm=final_norm),
        grid=(bsz, n_tiles),
        in_specs=[tok,
                  pl.BlockSpec((1, tm, d), next_tile),
                  _const_spec(mod.shape),
                  _const_spec((1, d)),
                  _const_spec(wg.shape),
                  _const_spec(wu.shape),
                  _const_spec(wd.shape),
                  _const_spec((1, d))] + [s[0] for s in specs] + ada_in_specs,
        out_specs=[tok] + [s[1] for s in specs] + ada_out_specs,
        out_shape=[jax.ShapeDtypeStruct((bsz, l, d), F32)]
        + [jax.ShapeDtypeStruct(stack.shape[1:], BF16) for stack, _ in cast] + ada_out_shape,
        scratch_shapes=[pltpu.VMEM((tm, d), BF16), pltpu.VMEM((tm, d), BF16), pltpu.VMEM((tm, D_FF), BF16)],
        compiler_params=_params(("arbitrary", "arbitrary")),
        name="ffn",
    )(x, x, mod, nw, wg, wu, wd, fnw, *[stack for stack, _ in cast], *ada_ops)
    n_cast = len(cast)
    return outs[0], tuple(outs[1:1 + n_cast]), (outs[1 + n_cast] if ada is not None else None)


def _rope_tables(p_seq, p_row, p_col):
    parts = []
    for p, n in zip((p_seq, p_row, p_col), ROPE_PAIRS):
        freq = ROPE_BASE ** (-jnp.arange(n, dtype=F32) / n)
        parts.append(p[:, None] * freq[None, :])
    ang = jnp.concatenate(parts, axis=-1)
    cos, sin = jnp.cos(ang), jnp.sin(ang)
    return jnp.concatenate([cos, cos], axis=-1), jnp.concatenate([-sin, sin], axis=-1)


def _tile(l):
    return min(l, 1024)


def kernel(x, c, ctx, c_ctx, ada_w, ada_b, norm_w, even_w_in, even_w_out, ret_decay_logit, conv_dw_w, conv_ln_w, conv_ln_b, odd_w_in, odd_w_out, pool_w, pool_scale, sg_ln_w, sg_ln_b, sg_w, sg_b, ffn_w_gate, ffn_w_up, ffn_w_down, final_norm_w):
    bsz, l, d = x.shape
    lc = ctx.shape[1]
    assert d == D_MODEL and l % _tile(l) == 0 and l % CHUNK == 0 and lc % CHUNK == 0 and bsz + 1 <= MOD_ROWS
    rows = l // GRID_W
    grid_r = jnp.broadcast_to(jnp.arange(rows, dtype=F32)[:, None], (rows, GRID_W)).reshape(-1)
    grid_c = jnp.broadcast_to(jnp.arange(GRID_W, dtype=F32)[None, :], (rows, GRID_W)).reshape(-1)
    cos_x, sin_x = _rope_tables(jnp.full((l,), lc, F32), grid_r, grid_c)
    zeros_c = jnp.zeros((lc,), F32)
    cos_c, sin_c = _rope_tables(jnp.arange(lc, dtype=F32), zeros_c, zeros_c)

    cvec = jnp.concatenate([c, c_ctx[None, :], jnp.zeros((MOD_ROWS - bsz - 1, d), F32)], axis=0)
    mod_rows = _mod_call(cvec, ada_w, ada_b, 0)
    dtabs = _decay_call(ret_decay_logit)
    zero_state = jnp.zeros((bsz, HEADS, CHUNK, CHUNK), F32)
    fnw = final_norm_w.reshape(1, d)
    tx, tc = _tile(l), _tile(lc)

    def layer_matrices(i):
        mix_in, mix_out = (even_w_in, even_w_out) if i % 2 == 0 else (odd_w_in, odd_w_out)
        return (mix_in, i // 2), (mix_out, i // 2), (ffn_w_gate, i), (ffn_w_up, i), (ffn_w_down, i)

    first = layer_matrices(0)
    bf16_mats = (first[0][0][first[0][1]].astype(BF16),) + (None,) * (len(first) - 1)
    for i in range(DEPTH):
        j = i // 2
        even = i % 2 == 0
        ctx_after = any(m % 2 == 0 for m in range(i + 1, DEPTH))
        mod_x = mod_rows[:bsz].reshape(bsz * 6, 1, d)
        mod_c = jnp.broadcast_to(mod_rows[bsz].reshape(1, 6, 1, d), (bsz, 6, 1, d)).reshape(bsz * 6, 1, d)
        nw1 = norm_w[i, 0].reshape(1, d)
        nw2 = norm_w[i, 1].reshape(1, d)
        w_in, w_out, wg, wu, wd = bf16_mats
        if even:
            conv_w = conv_dw_w[j]
            lnw = conv_ln_w[j].reshape(1, CONV_CH)
            lnb = conv_ln_b[j].reshape(1, CONV_CH)
            dtab = dtabs[j]
            if ctx_after:
                *mix_c, s_f, _ = _even_in_call(ctx, mod_c, nw1, w_in, cos_c, sin_c, zero_state, dtab, tc)
            else:
                s_f, s_b = _ctx_state_call(ctx, mod_c, nw1, w_in[:, RET_W:3 * RET_W], cos_c, sin_c, dtab)
            pending = layer_matrices(i)[1:] if w_out is None else ()
            *mix_x, _, cast_now = _even_in_call(x, mod_x, nw1, w_in, cos_x, sin_x, s_f, dtab, tx, pending)
            if pending:
                w_out, wg, wu, wd = cast_now
            if ctx_after:
                ctx, s_b = _even_mix_call(ctx, *mix_c, zero_state, dtab, mod_c, conv_w, lnw, lnb, w_out, tc)
            x, _ = _even_mix_call(x, *mix_x, s_b, dtab, mod_x, conv_w, lnw, lnb, w_out, tx)
        else:
            args = (nw1, w_in, pool_w[j].astype(BF16), pool_scale[j].reshape(1, POOL_CH),
                    sg_ln_w[j].reshape(1, SG_CH), sg_ln_b[j].reshape(1, SG_CH), sg_w[j].astype(BF16),
                    jnp.broadcast_to(sg_b[j][:, :, None], (SG_GROUPS, CHUNK, SG_GC)), w_out)
            if ctx_after:
                ctx = _odd_call(ctx, mod_c, *args, tc)
            x = _odd_call(x, mod_x, *args, tx)
        if ctx_after:
            ctx, _, _ = _ffn_call(ctx, mod_c, nw2, wg, wu, wd, fnw, tc, False)
        later = layer_matrices(i + 1) if i + 1 < DEPTH else ()
        ada_next = (cvec, ada_w, ada_b, i + 1) if i + 1 < DEPTH else None
        x, bf16_mats, mod_rows = _ffn_call(x, mod_x, nw2, wg, wu, wd, fnw, min(l, FFN_ROWS), i == DEPTH - 1,
                                           later, ada_next)
    return x
```

```python
import functools

import jax
import jax.numpy as jnp
from jax import lax
from jax.experimental import pallas as pl
from jax.experimental.pallas import tpu as pltpu

F32 = jnp.float32
BF16 = jnp.bfloat16

D_MODEL = 1024
DEPTH = 4
GRID_W = 64
EPS = 1e-6
HEADS = 4
HEAD_DIM = 128
RET_W = HEADS * HEAD_DIM
CHUNK = 128
ROPE_BASE = 10000.0
ROPE_PAIRS = (HEAD_DIM // 8, 3 * HEAD_DIM // 16, 3 * HEAD_DIM // 16)
CONV_CH = 512
CONV_K = 31
CONV_PAD = CONV_K // 2
EVEN_IN = 4 * RET_W + 2 * CONV_CH
POOL_CH = 512
POOL_WINDOWS = (2, 4, 8, 16)
POOL_GC = 128
SG_CH = 512
SG_GROUPS = 4
SG_GC = 128
ODD_IN = POOL_CH + 2 * SG_CH
D_FF = 2816
SUBLANES = 8
BF16_SUBLANES = 16
MOD_COLS = 1024
LANES = 128
FF_BLOCK = 256
OUT_BLOCK = 256
CONV_ROWS = 256
FFN_ROWS = 512
ODD_ROWS = 512
HALO = 16
N_DECAY_TABLES = 7
MOD_ROWS = 8
VMEM_LIMIT = 56 * 1024 * 1024


def _dot(a, b):
    return jnp.dot(a, b, preferred_element_type=F32)


def _dot_nt(a, b):
    return lax.dot_general(a, b, (((1,), (1,)), ((), ())), preferred_element_type=F32)


def _dot_tn(a, b):
    return lax.dot_general(a, b, (((0,), (0,)), ((), ())), preferred_element_type=F32)


def _silu(x):
    return x * jax.nn.sigmoid(x)


def _rms_mod(x32, nw, shift, scale):
    y = x32 * lax.rsqrt(jnp.mean(x32 * x32, axis=-1, keepdims=True) + EPS) * nw
    return y * (1.0 + scale) + shift


def _layer_norm(x32, w, b):
    mu = jnp.mean(x32, axis=-1, keepdims=True)
    xc = x32 - mu
    return xc * lax.rsqrt(jnp.mean(xc * xc, axis=-1, keepdims=True) + EPS) * w + b


def _const_spec(shape):
    return pl.BlockSpec(shape, lambda *_: (0,) * len(shape), pipeline_mode=pl.Buffered(1))


def _params(sem):
    return pltpu.CompilerParams(dimension_semantics=sem, vmem_limit_bytes=VMEM_LIMIT)


def _cast_band_specs(stack, layer, bsz, n_tiles):
    _, rows, cols = stack.shape
    n_steps = bsz * n_tiles
    hold = next(h for h in range(1, n_steps + 1)
                if n_steps % h == 0 and rows % (n_steps // h) == 0 and (rows * h // n_steps) % BF16_SUBLANES == 0)
    band = rows * hold // n_steps
    return (pl.BlockSpec((None, band, cols), lambda b, i: (layer, (b * n_tiles + i) // hold, 0)),
            pl.BlockSpec((band, cols), lambda b, i: ((b * n_tiles + i) // hold, 0)))


def _cast_bands(srcs, dsts):
    for src, dst in zip(srcs, dsts):
        dst[...] = src[...].astype(BF16)


def _mod_kernel(c_ref, w_ref, b_ref, o_ref):
    o_ref[...] = _dot(_silu(c_ref[...]).astype(BF16), w_ref[...].astype(BF16)) + b_ref[...]


def _mod_call(c_rows, ada_w, ada_b, layer):
    depth, d, n = ada_w.shape
    tn = MOD_COLS
    return pl.pallas_call(
        _mod_kernel,
        grid=(n // tn,),
        in_specs=[_const_spec((MOD_ROWS, d)),
                  pl.BlockSpec((None, d, tn), lambda j: (layer, 0, j)),
                  pl.BlockSpec((None, 1, tn), lambda j: (layer, 0, j))],
        out_specs=pl.BlockSpec((MOD_ROWS, tn), lambda j: (0, j)),
        out_shape=jax.ShapeDtypeStruct((MOD_ROWS, n), F32),
        compiler_params=_params(("arbitrary",)),
        name="adaln_mod",
    )(c_rows, ada_w, ada_b.reshape(depth, 1, n))


def _log_sigmoid(x):
    return -(jnp.maximum(-x, 0.0) + jnp.log1p(jnp.exp(-jnp.abs(x))))


def _decay_kernel(logit_ref, o_ref):
    j = pl.program_id(0)
    row = lax.broadcasted_iota(jnp.int32, (CHUNK, CHUNK), 0).astype(F32)
    col = lax.broadcasted_iota(jnp.int32, (CHUNK, CHUNK), 1).astype(F32)
    diff = row - col
    lower = diff >= 0.0
    for hd in range(HEADS):
        lgf = _log_sigmoid(jnp.full((CHUNK, CHUNK), logit_ref[(j * 2 + 0) * HEADS + hd], F32))
        lgb = _log_sigmoid(jnp.full((CHUNK, CHUNK), logit_ref[(j * 2 + 1) * HEADS + hd], F32))
        o_ref[0, 0, hd] = jnp.where(lower, jnp.exp(lgf * jnp.where(lower, diff, 0.0)),
                                    jnp.exp(lgb * jnp.where(lower, 0.0, -diff - 1.0)))
        o_ref[0, 1, hd] = jnp.exp(lgf * (CHUNK - 1.0 - row))
        o_ref[0, 2, hd] = jnp.exp(lgb * row)
        o_ref[0, 3, hd] = jnp.exp(lgf * (row + 1.0))
        o_ref[0, 4, hd] = jnp.exp(lgb * (CHUNK - 1.0 - row))
        o_ref[0, 5, hd] = jnp.exp(lgf * float(CHUNK))
        o_ref[0, 6, hd] = jnp.exp(lgb * float(CHUNK))


def _decay_call(ret_decay_logit):
    n_even = ret_decay_logit.shape[0]
    return pl.pallas_call(
        _decay_kernel,
        grid=(n_even,),
        in_specs=[pl.BlockSpec(memory_space=pltpu.SMEM)],
        out_specs=pl.BlockSpec((1, N_DECAY_TABLES, HEADS, CHUNK, CHUNK), lambda j: (j, 0, 0, 0, 0)),
        out_shape=jax.ShapeDtypeStruct((n_even, N_DECAY_TABLES, HEADS, CHUNK, CHUNK), F32),
        compiler_params=_params(("arbitrary",)),
        name="ret_decay",
    )(ret_decay_logit.reshape(-1).astype(F32))


def _even_in_kernel(x_ref, mod_ref, nw_ref, w_ref, cos_ref, sin_ref, s0_ref, dt_ref, *rest, n_cast, tm):
    cast_in = rest[:n_cast]
    q_ref, k_ref, v_ref, gs_ref, u_ref, sprev_ref, sfin_ref = rest[n_cast:n_cast + 7]
    cast_out = rest[n_cast + 7:2 * n_cast + 7]
    state_ref = rest[2 * n_cast + 7]
    b = pl.program_id(0)
    i = pl.program_id(1)
    _cast_bands(cast_in, cast_out)

    @pl.when(i == 0)
    def _():
        state_ref[...] = s0_ref[0]

    h = _rms_mod(x_ref[0], nw_ref[...], mod_ref[b * 6 + 0], mod_ref[b * 6 + 1]).astype(BF16)
    cosf = cos_ref[...]
    sinf = sin_ref[...]

    def rope(t):
        return t * cosf + pltpu.roll(t, HEAD_DIM // 2, 1) * sinf

    pq = _dot(h, w_ref[:, 0:RET_W])
    pk = _dot(h, w_ref[:, RET_W:2 * RET_W])
    pv = _dot(h, w_ref[:, 2 * RET_W:3 * RET_W])
    scale = HEAD_DIM ** -0.5
    for hd in range(HEADS):
        sl = slice(hd * HEAD_DIM, (hd + 1) * HEAD_DIM)
        q_ref[0, :, sl] = (rope(pq[:, sl]) * scale).astype(BF16)
        kr = rope(pk[:, sl]).astype(BF16)
        k_ref[0, :, sl] = kr
        vh = pv[:, sl]
        v_ref[0, :, sl] = vh.astype(BF16)
        zeta = dt_ref[1, hd]
        cdec = dt_ref[5, hd]
        s_cur = state_ref[hd]
        for c in range(tm // CHUNK):
            rows = slice(c * CHUNK, (c + 1) * CHUNK)
            sprev_ref[0, c, hd] = s_cur.astype(BF16)
            wv = (vh[rows] * zeta).astype(BF16)
            s_cur = s_cur * cdec + _dot_tn(kr[rows], wv)
        state_ref[hd] = s_cur

    pg = _dot(h, w_ref[:, 3 * RET_W:4 * RET_W])
    gs_ref[0] = _silu(pg).astype(BF16)
    pa = _dot(h, w_ref[:, 4 * RET_W:4 * RET_W + CONV_CH])
    pb = _dot(h, w_ref[:, 4 * RET_W + CONV_CH:EVEN_IN])
    u_ref[0] = (pa * jax.nn.sigmoid(pb)).astype(BF16)

    @pl.when(i == pl.num_programs(1) - 1)
    def _():
        sfin_ref[0] = state_ref[...]


def _even_in_call(x, mod, nw, w_in, cosf, sinf, s0, dtab, tm, cast=()):
    bsz, l, d = x.shape
    n_chunks = l // CHUNK
    cpt = tm // CHUNK
    tok = lambda w: pl.BlockSpec((1, tm, w), lambda b, i: (b, i, 0))
    state_spec = pl.BlockSpec((1, HEADS, CHUNK, CHUNK), lambda b, i: (b, 0, 0, 0))
    specs = [_cast_band_specs(stack, layer, bsz, l // tm) for stack, layer in cast]
    outs = pl.pallas_call(
        functools.partial(_even_in_kernel, n_cast=len(cast), tm=tm),
        grid=(bsz, l // tm),
        in_specs=[tok(d),
                  _const_spec(mod.shape),
                  _const_spec((1, d)),
                  _const_spec(w_in.shape),
                  pl.BlockSpec((tm, HEAD_DIM), lambda b, i: (i, 0)),
                  pl.BlockSpec((tm, HEAD_DIM), lambda b, i: (i, 0)),
                  state_spec,
                  _const_spec(dtab.shape)] + [s[0] for s in specs],
        out_specs=[tok(RET_W), tok(RET_W), tok(RET_W), tok(RET_W), tok(CONV_CH),
                   pl.BlockSpec((1, cpt, HEADS, CHUNK, CHUNK), lambda b, i: (b, i, 0, 0, 0)),
                   state_spec] + [s[1] for s in specs],
        out_shape=[jax.ShapeDtypeStruct((bsz, l, RET_W), BF16),
                   jax.ShapeDtypeStruct((bsz, l, RET_W), BF16),
                   jax.ShapeDtypeStruct((bsz, l, RET_W), BF16),
                   jax.ShapeDtypeStruct((bsz, l, RET_W), BF16),
                   jax.ShapeDtypeStruct((bsz, l, CONV_CH), BF16),
                   jax.ShapeDtypeStruct((bsz, n_chunks, HEADS, CHUNK, CHUNK), BF16),
                   jax.ShapeDtypeStruct((bsz, HEADS, CHUNK, CHUNK), F32)]
        + [jax.ShapeDtypeStruct(stack.shape[1:], BF16) for stack, _ in cast],
        scratch_shapes=[pltpu.VMEM((HEADS, CHUNK, CHUNK), F32)],
        compiler_params=_params(("arbitrary", "arbitrary")),
        name="even_in",
    )(x, mod, nw, w_in, cosf, sinf, s0, dtab, *[stack for stack, _ in cast])
    return (*outs[:7], tuple(outs[7:]))


def _ctx_state_kernel(x_ref, mod_ref, nw_ref, w_ref, cos_ref, sin_ref, dt_ref, sf_ref, sb_ref):
    b = pl.program_id(0)
    h = _rms_mod(x_ref[0], nw_ref[...], mod_ref[b * 6 + 0], mod_ref[b * 6 + 1]).astype(BF16)
    cosf = cos_ref[...]
    sinf = sin_ref[...]
    pk = _dot(h, w_ref[:, 0:RET_W])
    pv = _dot(h, w_ref[:, RET_W:2 * RET_W])
    n_chunks = x_ref.shape[1] // CHUNK
    for hd in range(HEADS):
        sl = slice(hd * HEAD_DIM, (hd + 1) * HEAD_DIM)
        t = pk[:, sl]
        kr = (t * cosf + pltpu.roll(t, HEAD_DIM // 2, 1) * sinf).astype(BF16)
        vh = pv[:, sl]
        s_f = jnp.zeros((CHUNK, CHUNK), F32)
        s_b = jnp.zeros((CHUNK, CHUNK), F32)
        for c in range(n_chunks):
            rows = slice(c * CHUNK, (c + 1) * CHUNK)
            s_f = s_f * dt_ref[5, hd] + _dot_tn(kr[rows], (vh[rows] * dt_ref[1, hd]).astype(BF16))
        for c in reversed(range(n_chunks)):
            rows = slice(c * CHUNK, (c + 1) * CHUNK)
            s_b = s_b * dt_ref[6, hd] + _dot_tn(kr[rows], (vh[rows] * dt_ref[2, hd]).astype(BF16))
        sf_ref[0, hd] = s_f
        sb_ref[0, hd] = s_b


def _ctx_state_call(ctx, mod, nw, w_kv, cosf, sinf, dtab):
    bsz, lc, d = ctx.shape
    state_spec = pl.BlockSpec((1, HEADS, CHUNK, CHUNK), lambda b: (b, 0, 0, 0))
    return pl.pallas_call(
        _ctx_state_kernel,
        grid=(bsz,),
        in_specs=[pl.BlockSpec((1, lc, d), lambda b: (b, 0, 0)),
                  _const_spec(mod.shape),
                  _const_spec((1, d)),
                  _const_spec(w_kv.shape),
                  _const_spec(cosf.shape),
                  _const_spec(sinf.shape),
                  _const_spec(dtab.shape)],
        out_specs=[state_spec, state_spec],
        out_shape=[jax.ShapeDtypeStruct((bsz, HEADS, CHUNK, CHUNK), F32)] * 2,
        compiler_params=_params(("arbitrary",)),
        name="ctx_state",
    )(ctx, mod, nw, w_kv, cosf, sinf, dtab)


def _even_mix_pieces(b, i, n_tiles, x_ref, q_ref, k_ref, v_ref, gs_ref, u_ref, up_ref, un_ref, sprev_ref,
                     dt_ref, mod_ref, cw_ref, lnw_ref, lnb_ref, wo_ref, state_ref, uext_ref, mix_ref, xo_ref, tm):
    def retention(c, hd):
        rows = slice(c * CHUNK, (c + 1) * CHUNK)
        sl = slice(hd * HEAD_DIM, (hd + 1) * HEAD_DIM)
        qc = q_ref[0, rows, sl]
        kc = k_ref[0, rows, sl]
        vc = v_ref[0, rows, sl]
        s_b = state_ref[hd]
        p = (_dot_nt(qc, kc) * dt_ref[0, hd]).astype(BF16)
        y = _dot(p, vc)
        y = y + _dot(qc, sprev_ref[0, c, hd]) * dt_ref[3, hd]
        y = y + _dot(qc, s_b.astype(BF16)) * dt_ref[4, hd]
        y = y * lax.rsqrt(jnp.mean(y * y, axis=-1, keepdims=True) + EPS)
        mix_ref[rows, sl] = gs_ref[0, rows, sl] * y.astype(BF16)
        wv = (vc.astype(F32) * dt_ref[2, hd]).astype(BF16)
        state_ref[hd] = s_b * dt_ref[6, hd] + _dot_tn(kc, wv)

    def halo():
        uext_ref[0:HALO] = jnp.where(i < n_tiles - 1, up_ref[0].astype(F32), 0.0)
        uext_ref[HALO:HALO + tm] = u_ref[0].astype(F32)
        uext_ref[HALO + tm:2 * HALO + tm] = jnp.where(i > 0, un_ref[0].astype(F32), 0.0)

    def conv(r):
        blocks = []
        for cb in range(CONV_CH // LANES):
            cols = slice(cb * LANES, (cb + 1) * LANES)
            acc = None
            for sub in range(SUBLANES):
                part = None
                for t in range(sub, CONV_K, SUBLANES):
                    start = r * CONV_ROWS + t - sub
                    term = uext_ref[start:start + CONV_ROWS + SUBLANES, cols] * cw_ref[t:t + 1, cols]
                    part = term if part is None else part + term
                shifted = pltpu.roll(part, CONV_ROWS + SUBLANES - (sub + 1), 0)[0:CONV_ROWS]
                acc = shifted if acc is None else acc + shifted
            blocks.append(acc)
        cv = _silu(_layer_norm(jnp.concatenate(blocks, axis=-1), lnw_ref[...], lnb_ref[...]))
        mix_ref[r * CONV_ROWS:(r + 1) * CONV_ROWS, RET_W:RET_W + CONV_CH] = cv.astype(BF16)

    def out_proj(n):
        cols = slice(n * OUT_BLOCK, (n + 1) * OUT_BLOCK)
        out = _dot(mix_ref[...], wo_ref[:, cols])
        xo_ref[:, cols] = x_ref[0, :, cols] + mod_ref[b * 6 + 2][:, cols] * out

    pieces = [halo]
    convs = [functools.partial(conv, r) for r in range(tm // CONV_ROWS)]
    rets = [functools.partial(retention, c, hd) for c in reversed(range(tm // CHUNK)) for hd in range(HEADS)]
    per = len(rets) // len(convs)
    for r, cv in enumerate(convs):
        pieces += rets[r * per:(r + 1) * per] + [cv]
    pieces += rets[len(convs) * per:]
    pieces += [functools.partial(out_proj, n) for n in range(D_MODEL // OUT_BLOCK)]
    return pieces


def _ffn_pieces(xs_ref, o_ref, b, mod_ref, nw_ref, wg_ref, wu_ref, wd_ref, fnw_ref, h_ref, act_ref, final_norm):
    def prologue():
        x32 = xs_ref[...]
        o_ref[...] = x32
        h_ref[...] = _rms_mod(x32, nw_ref[...], mod_ref[b * 6 + 3], mod_ref[b * 6 + 4]).astype(BF16)

    def gate_up(j):
        cols = slice(j * FF_BLOCK, (j + 1) * FF_BLOCK)
        h = h_ref[...]
        act_ref[:, cols] = (_silu(_dot(h, wg_ref[:, cols])) * _dot(h, wu_ref[:, cols])).astype(BF16)

    def down(n):
        cols = slice(n * OUT_BLOCK, (n + 1) * OUT_BLOCK)
        o_ref[:, cols] = o_ref[:, cols] + mod_ref[b * 6 + 5][:, cols] * _dot(act_ref[...], wd_ref[:, cols])

    def final():
        y = o_ref[...]
        o_ref[...] = y * lax.rsqrt(jnp.mean(y * y, axis=-1, keepdims=True) + EPS) * fnw_ref[...]

    pieces = [prologue]
    pieces += [functools.partial(gate_up, j) for j in range(D_FF // FF_BLOCK)]
    pieces += [functools.partial(down, n) for n in range(D_MODEL // OUT_BLOCK)]
    if final_norm:
        pieces.append(final)
    return pieces


def _even_mix_kernel(x_ref, q_ref, k_ref, v_ref, gs_ref, u_ref, up_ref, un_ref, sprev_ref, sb0_ref,
                     dt_ref, mod_ref, cw_ref, lnw_ref, lnb_ref, wo_ref,
                     o_ref, sbfin_ref, state_ref, uext_ref, mix_ref, *, tm):
    b = pl.program_id(0)
    i = pl.program_id(1)
    n_tiles = pl.num_programs(1)

    @pl.when(i == 0)
    def _():
        state_ref[...] = sb0_ref[0]

    for piece in _even_mix_pieces(b, i, n_tiles, x_ref, q_ref, k_ref, v_ref, gs_ref, u_ref, up_ref, un_ref,
                                  sprev_ref, dt_ref, mod_ref, cw_ref, lnw_ref, lnb_ref, wo_ref,
                                  state_ref, uext_ref, mix_ref, o_ref.at[0], tm):
        piece()

    @pl.when(i == n_tiles - 1)
    def _():
        sbfin_ref[0] = state_ref[...]


def _even_mix_call(x, q, k, v, gs, u, sprev, sb0, dtab, mod, conv_w, ln_w, ln_b, w_out, tm):
    bsz, l, d = x.shape
    n_tiles = l // tm
    cpt = tm // CHUNK
    hpt = tm // HALO
    n_halo = l // HALO
    rev = lambda i: n_tiles - 1 - i
    tok = lambda w: pl.BlockSpec((1, tm, w), lambda b, i: (b, rev(i), 0))
    state_spec = pl.BlockSpec((1, HEADS, CHUNK, CHUNK), lambda b, i: (b, 0, 0, 0))
    return pl.pallas_call(
        functools.partial(_even_mix_kernel, tm=tm),
        grid=(bsz, n_tiles),
        in_specs=[tok(d), tok(RET_W), tok(RET_W), tok(RET_W), tok(RET_W), tok(CONV_CH),
                  pl.BlockSpec((1, HALO, CONV_CH), lambda b, i: (b, jnp.maximum(rev(i) * hpt - 1, 0), 0)),
                  pl.BlockSpec((1, HALO, CONV_CH),
                               lambda b, i: (b, jnp.minimum((rev(i) + 1) * hpt, n_halo - 1), 0)),
                  pl.BlockSpec((1, cpt, HEADS, CHUNK, CHUNK), lambda b, i: (b, rev(i), 0, 0, 0)),
                  state_spec,
                  _const_spec(dtab.shape),
                  _const_spec(mod.shape),
                  _const_spec(conv_w.shape),
                  _const_spec((1, CONV_CH)),
                  _const_spec((1, CONV_CH)),
                  _const_spec(w_out.shape)],
        out_specs=[tok(d), state_spec],
        out_shape=[jax.ShapeDtypeStruct((bsz, l, d), F32),
                   jax.ShapeDtypeStruct((bsz, HEADS, CHUNK, CHUNK), F32)],
        scratch_shapes=[pltpu.VMEM((HEADS, CHUNK, CHUNK), F32),
                        pltpu.VMEM((tm + 2 * HALO, CONV_CH), F32),
                        pltpu.VMEM((tm, RET_W + CONV_CH), BF16)],
        compiler_params=_params(("arbitrary", "arbitrary")),
        name="even_mix",
    )(x, q, k, v, gs, u, u, u, sprev, sb0, dtab, mod, conv_w, ln_w, ln_b, w_out)


def _odd_kernel(x_ref, xp_ref, xn_ref, mod_ref, nw_ref, wi_ref, pw_ref, ps_ref, lnw_ref, lnb_ref,
                sgw_ref, sgb_ref, wo_ref, o_ref, h_ref, pcx_ref, mix_ref, *, tm, seq_len):
    b = pl.program_id(0)
    i = pl.program_id(1)
    n_tiles = pl.num_programs(1)
    nw = nw_ref[...]
    shift = mod_ref[b * 6 + 0]
    scale = mod_ref[b * 6 + 1]

    h_ref[0:HALO] = _rms_mod(xp_ref[0], nw, shift, scale).astype(BF16)
    h_ref[HALO:HALO + tm] = _rms_mod(x_ref[0], nw, shift, scale).astype(BF16)
    h_ref[HALO + tm:2 * HALO + tm] = _rms_mod(xn_ref[0], nw, shift, scale).astype(BF16)

    pcx_ref[0:tm + 2 * HALO] = _dot(h_ref[...], wi_ref[:, 0:POOL_CH])
    pcx_ref[0:HALO] = jnp.where(i > 0, pcx_ref[0:HALO], 0.0)
    pcx_ref[HALO + tm:2 * HALO + tm] = jnp.where(i < n_tiles - 1, pcx_ref[HALO + tm:2 * HALO + tm], 0.0)
    pcx_ref[tm + 2 * HALO:tm + 2 * HALO + SUBLANES] = jnp.zeros((SUBLANES, POOL_CH), F32)

    sub = min(tm, ODD_ROWS)
    for r0 in range(0, tm, sub):
        rows = slice(r0, r0 + sub)
        pos = i * tm + r0 + lax.broadcasted_iota(jnp.int32, (sub, POOL_GC), 0)
        for gi, w in enumerate(POOL_WINDOWS):
            left = w // 2
            right = w - 1 - left
            sl = slice(gi * POOL_GC, (gi + 1) * POOL_GC)
            run = pcx_ref[r0:r0 + sub + 2 * HALO + SUBLANES, sl]
            span = 1
            while span < w:
                n_run = run.shape[0]
                run = (run + pltpu.roll(run, n_run - span, 0))[0:n_run - SUBLANES]
                span *= 2
            win = pltpu.roll(run, run.shape[0] - (HALO - left), 0)[0:sub]
            cnt = jnp.minimum(pos + right, seq_len - 1) - jnp.maximum(pos - left, 0) + 1
            m = win / cnt.astype(F32) - pcx_ref[HALO + r0:HALO + r0 + sub, sl]
            po = _dot(m.astype(BF16), pw_ref[gi]) * ps_ref[:, sl]
            mix_ref[rows, sl] = po.astype(BF16)

        pd = _dot(h_ref[HALO + r0:HALO + r0 + sub], wi_ref[:, POOL_CH:ODD_IN])
        z = 0.5 * pd * (1.0 + lax.erf(pd * (2.0 ** -0.5)))
        zu = z[:, 0:SG_CH]
        zv = _layer_norm(z[:, SG_CH:2 * SG_CH], lnw_ref[...], lnb_ref[...]).astype(BF16)
        for c in range(sub // CHUNK):
            crows = slice(c * CHUNK, (c + 1) * CHUNK)
            for g in range(SG_GROUPS):
                sl = slice(g * SG_GC, (g + 1) * SG_GC)
                s = _dot(sgw_ref[g], zv[crows, sl]) + sgb_ref[g]
                mix_ref[r0 + c * CHUNK:r0 + (c + 1) * CHUNK, POOL_CH + g * SG_GC:POOL_CH + (g + 1) * SG_GC] = (
                    zu[crows, sl] * s).astype(BF16)

        out = _dot(mix_ref[rows], wo_ref[...])
        o_ref[0, rows] = x_ref[0, rows] + mod_ref[b * 6 + 2] * out


def _odd_call(x, mod, nw, w_in, pool_w, pool_scale, ln_w, ln_b, sg_w, sg_b_full, w_out, tm):
    bsz, l, d = x.shape
    hpt = tm // HALO
    n_halo = l // HALO
    tok = pl.BlockSpec((1, tm, d), lambda b, i: (b, i, 0))
    return pl.pallas_call(
        functools.partial(_odd_kernel, tm=tm, seq_len=l),
        grid=(bsz, l // tm),
        in_specs=[tok,
                  pl.BlockSpec((1, HALO, d), lambda b, i: (b, jnp.maximum(i * hpt - 1, 0), 0)),
                  pl.BlockSpec((1, HALO, d), lambda b, i: (b, jnp.minimum((i + 1) * hpt, n_halo - 1), 0)),
                  _const_spec(mod.shape),
                  _const_spec((1, d)),
                  _const_spec(w_in.shape),
                  _const_spec(pool_w.shape),
                  _const_spec((1, POOL_CH)),
                  _const_spec((1, SG_CH)),
                  _const_spec((1, SG_CH)),
                  _const_spec(sg_w.shape),
                  _const_spec(sg_b_full.shape),
                  _const_spec(w_out.shape)],
        out_specs=tok,
        out_shape=jax.ShapeDtypeStruct((bsz, l, d), F32),
        scratch_shapes=[pltpu.VMEM((tm + 2 * HALO, d), BF16),
                        pltpu.VMEM((tm + 2 * HALO + SUBLANES, POOL_CH), F32),
                        pltpu.VMEM((tm, POOL_CH + SG_CH), BF16)],
        compiler_params=_params(("arbitrary", "arbitrary")),
        name="odd_mix",
    )(x, x, x, mod, nw, w_in, pool_w, pool_scale, ln_w, ln_b, sg_w, sg_b_full, w_out)


def _ffn_kernel(x_ref, mod_ref, nw_ref, wg_ref, wu_ref, wd_ref, fnw_ref, *rest, n_cast, n_ada, final_norm):
    cast_in, rest = rest[:n_cast], rest[n_cast:]
    ada_in, rest = rest[:3 * n_ada], rest[3 * n_ada:]
    o_ref, rest = rest[0], rest[1:]
    cast_out, rest = rest[:n_cast], rest[n_cast:]
    ada_out, (h_ref, act_ref) = rest[:n_ada], rest[n_ada:]
    _cast_bands(cast_in, cast_out)
    if n_ada:
        c_ref, aw_ref, ab_ref = ada_in
        mo_ref, = ada_out

        @pl.when(jnp.logical_and(pl.program_id(0) == 0, pl.program_id(1) == 0))
        def _():
            mo_ref[...] = jnp.broadcast_to(ab_ref[...], mo_ref.shape)

        mo_ref[...] += _dot(_silu(c_ref[...]).astype(BF16), aw_ref[...].astype(BF16))
    tm = x_ref.shape[1]
    subs = []
    for r0 in range(0, tm, min(tm, FFN_ROWS)):
        rows = pl.ds(r0, min(tm, FFN_ROWS))
        subs.append(_ffn_pieces(x_ref.at[0, rows], o_ref.at[0, rows], pl.program_id(0), mod_ref, nw_ref,
                                wg_ref, wu_ref, wd_ref, fnw_ref, h_ref.at[rows], act_ref.at[rows], final_norm))
    order = [subs[0][0]]
    for k, pieces in enumerate(subs):
        order.append(pieces[1])
        if k + 1 < len(subs):
            order.append(subs[k + 1][0])
        order += pieces[2:]
    for piece in order:
        piece()


def _ffn_call(x, mod, nw, wg, wu, wd, fnw, tm, final_norm, cast=(), ada=None):
    bsz, l, d = x.shape
    n_tiles = l // tm
    n_steps = bsz * n_tiles
    tok = pl.BlockSpec((1, tm, d), lambda b, i: (b, i, 0))
    specs = [_cast_band_specs(stack, layer, bsz, n_tiles) for stack, layer in cast]
    ada_ops, ada_in_specs, ada_out_specs, ada_out_shape = [], [], [], []
    if ada is not None:
        c_rows, ada_w, ada_b, layer = ada
        n_mod = ada_w.shape[2]
        band = d // n_steps
        assert d % n_steps == 0 and band % SUBLANES == 0
        c_bands = c_rows.reshape(MOD_ROWS, n_steps, band).transpose(1, 0, 2)
        ada_ops = [c_bands, ada_w, ada_b.reshape(ada_b.shape[0], 1, n_mod)]
        ada_in_specs = [pl.BlockSpec((None, MOD_ROWS, band), lambda b, i: (b * n_tiles + i, 0, 0)),
                        pl.BlockSpec((None, band, n_mod), lambda b, i: (layer, b * n_tiles + i, 0)),
                        pl.BlockSpec((None, 1, n_mod), lambda b, i: (layer, 0, 0))]
        ada_out_specs = [pl.BlockSpec((MOD_ROWS, n_mod), lambda b, i: (0, 0))]
        ada_out_shape = [jax.ShapeDtypeStruct((MOD_ROWS, n_mod), F32)]
    outs = pl.pallas_call(
        functools.partial(_ffn_kernel, n_cast=len(cast), n_ada=len(ada_out_specs), final_norm=final_norm),
        grid=(bsz, n_tiles),
        in_specs=[tok,
                  _const_spec(mod.shape),
                  _const_spec((1, d)),
                  _const_spec(wg.shape),
                  _const_spec(wu.shape),
                  _const_spec(wd.shape),
                  _const_spec((1, d))] + [s[0] for s in specs] + ada_in_specs,
        out_specs=[tok] + [s[1] for s in specs] + ada_out_specs,
        out_shape=[jax.ShapeDtypeStruct((bsz, l, d), F32)]
        + [jax.ShapeDtypeStruct(stack.shape[1:], BF16) for stack, _ in cast] + ada_out_shape,
        scratch_shapes=[pltpu.VMEM((tm, d), BF16), pltpu.VMEM((tm, D_FF), BF16)],
        compiler_params=_params(("arbitrary", "arbitrary")),
        name="ffn",
    )(x, mod, nw, wg, wu, wd, fnw, *[stack for stack, _ in cast], *ada_ops)
    n_cast = len(cast)
    return outs[0], tuple(outs[1:1 + n_cast]), (outs[1 + n_cast] if ada is not None else None)


def _rope_tables(p_seq, p_row, p_col):
    parts = []
    for p, n in zip((p_seq, p_row, p_col), ROPE_PAIRS):
        freq = ROPE_BASE ** (-jnp.arange(n, dtype=F32) / n)
        parts.append(p[:, None] * freq[None, :])
    ang = jnp.concatenate(parts, axis=-1)
    cos, sin = jnp.cos(ang), jnp.sin(ang)
    return jnp.concatenate([cos, cos], axis=-1), jnp.concatenate([-sin, sin], axis=-1)


def _tile(l):
    return min(l, 1024)


def kernel(x, c, ctx, c_ctx, ada_w, ada_b, norm_w, even_w_in, even_w_out, ret_decay_logit, conv_dw_w, conv_ln_w, conv_ln_b, odd_w_in, odd_w_out, pool_w, pool_scale, sg_ln_w, sg_ln_b, sg_w, sg_b, ffn_w_gate, ffn_w_up, ffn_w_down, final_norm_w):
    bsz, l, d = x.shape
    lc = ctx.shape[1]
    assert d == D_MODEL and l % _tile(l) == 0 and l % CHUNK == 0 and lc % CHUNK == 0 and bsz + 1 <= MOD_ROWS
    rows = l // GRID_W
    grid_r = jnp.broadcast_to(jnp.arange(rows, dtype=F32)[:, None], (rows, GRID_W)).reshape(-1)
    grid_c = jnp.broadcast_to(jnp.arange(GRID_W, dtype=F32)[None, :], (rows, GRID_W)).reshape(-1)
    cos_x, sin_x = _rope_tables(jnp.full((l,), lc, F32), grid_r, grid_c)
    zeros_c = jnp.zeros((lc,), F32)
    cos_c, sin_c = _rope_tables(jnp.arange(lc, dtype=F32), zeros_c, zeros_c)

    cvec = jnp.concatenate([c, c_ctx[None, :], jnp.zeros((MOD_ROWS - bsz - 1, d), F32)], axis=0)
    mod_rows = _mod_call(cvec, ada_w, ada_b, 0)
    dtabs = _decay_call(ret_decay_logit)
    zero_state = jnp.zeros((bsz, HEADS, CHUNK, CHUNK), F32)
    fnw = final_norm_w.reshape(1, d)
    tx, tc = _tile(l), _tile(lc)

    def layer_matrices(i):
        mix_in, mix_out = (even_w_in, even_w_out) if i % 2 == 0 else (odd_w_in, odd_w_out)
        return (mix_in, i // 2), (mix_out, i // 2), (ffn_w_gate, i), (ffn_w_up, i), (ffn_w_down, i)

    first = layer_matrices(0)
    bf16_mats = (first[0][0][first[0][1]].astype(BF16),) + (None,) * (len(first) - 1)
    for i in range(DEPTH):
        j = i // 2
        even = i % 2 == 0
        ctx_after = any(m % 2 == 0 for m in range(i + 1, DEPTH))
        mod_x = mod_rows[:bsz].reshape(bsz * 6, 1, d)
        mod_c = jnp.broadcast_to(mod_rows[bsz].reshape(1, 6, 1, d), (bsz, 6, 1, d)).reshape(bsz * 6, 1, d)
        nw1 = norm_w[i, 0].reshape(1, d)
        nw2 = norm_w[i, 1].reshape(1, d)
        w_in, w_out, wg, wu, wd = bf16_mats
        if even:
            conv_w = conv_dw_w[j]
            lnw = conv_ln_w[j].reshape(1, CONV_CH)
            lnb = conv_ln_b[j].reshape(1, CONV_CH)
            dtab = dtabs[j]
            if ctx_after:
                *mix_c, s_f, _ = _even_in_call(ctx, mod_c, nw1, w_in, cos_c, sin_c, zero_state, dtab, tc)
            else:
                s_f, s_b = _ctx_state_call(ctx, mod_c, nw1, w_in[:, RET_W:3 * RET_W], cos_c, sin_c, dtab)
            pending = layer_matrices(i)[1:] if w_out is None else ()
            *mix_x, _, cast_now = _even_in_call(x, mod_x, nw1, w_in, cos_x, sin_x, s_f, dtab, tx, pending)
            if pending:
                w_out, wg, wu, wd = cast_now
            if ctx_after:
                ctx, s_b = _even_mix_call(ctx, *mix_c, zero_state, dtab, mod_c, conv_w, lnw, lnb, w_out, tc)
            x, _ = _even_mix_call(x, *mix_x, s_b, dtab, mod_x, conv_w, lnw, lnb, w_out, tx)
        else:
            args = (nw1, w_in, pool_w[j].astype(BF16), pool_scale[j].reshape(1, POOL_CH),
                    sg_ln_w[j].reshape(1, SG_CH), sg_ln_b[j].reshape(1, SG_CH), sg_w[j].astype(BF16),
                    jnp.broadcast_to(sg_b[j][:, :, None], (SG_GROUPS, CHUNK, SG_GC)), w_out)
            if ctx_after:
                ctx = _odd_call(ctx, mod_c, *args, tc)
            x = _odd_call(x, mod_x, *args, tx)
        if ctx_after:
            flat, _, _ = _ffn_call(ctx.reshape(1, bsz * lc, d), mod_c[:6], nw2, wg, wu, wd, fnw,
                                   min(bsz * lc, 2 * FFN_ROWS), False)
            ctx = flat.reshape(bsz, lc, d)
        later = layer_matrices(i + 1) if i + 1 < DEPTH else ()
        ada_next = (cvec, ada_w, ada_b, i + 1) if i + 1 < DEPTH else None
        x, bf16_mats, mod_rows = _ffn_call(x, mod_x, nw2, wg, wu, wd, fnw, min(l, 2 * FFN_ROWS), i == DEPTH - 1,
                                           later, ada_next)
    return x
```
